```python
import jax
import jax.numpy as jnp
from jax import lax
import numpy as np

D_MODEL = 1024
BATCH = 2
SEQ = 16384
DEPTH = 2

N_A_LAYERS = DEPTH // 2
N_B_LAYERS = DEPTH - N_A_LAYERS
D_FF = 4 * D_MODEL
NORM_EPS = 1e-6
RWKV_HEAD = 64
RWKV_HEADS = D_MODEL // RWKV_HEAD
D_DECAY_LORA = 64
D_AAA_LORA = 64
D_GATE_LORA = 160
GN_EPS = 64e-5
NSA_HEADS = 16
NSA_GROUPS = 4
NSA_HPG = NSA_HEADS // NSA_GROUPS
HEAD_DIM = 64
CMP_LEN = 32
CMP_STRIDE = 16
CMP_RATIO = CMP_LEN // CMP_STRIDE
SLC_LEN = 64
SLC_RATIO = SLC_LEN // CMP_STRIDE
TOP_N = 16
WINDOW = 512
PHI_HIDDEN = 128
Q_BLOCK = 64
N_KV_SLOTS = 6
FORCED_SCORE = 1e4
NEG_INF = -1e30

kernel_name = 'hybrid_rwkv7_nsa_yoco'


def rms_norm(x, g):
    xf = x.astype(jnp.float32)
    y = xf * lax.rsqrt(jnp.mean(xf * xf, axis=-1, keepdims=True) + NORM_EPS)
    return (y * g.astype(jnp.float32)).astype(x.dtype)


def modulate(h, shift, scale):
    return h * (1 + scale[:, None, :]) + shift[:, None, :]


def masked_softmax(s, mask):
    s = jnp.where(mask, s.astype(jnp.float32), NEG_INF)
    p = jnp.exp(s - jnp.max(s, axis=-1, keepdims=True)) * mask
    return p / jnp.maximum(jnp.sum(p, axis=-1, keepdims=True), 1e-30)


def sq_relu_mlp(h, w_up, w_down):
    return jnp.square(jax.nn.relu(h @ w_up)) @ w_down


def rwkv7_time_mix(h, mu, w_r, w_k, w_v, w_o, w0, wa, wb, a0, aa, ab, ga, gb,
                   k_k, k_a, r_k, ln_g, ln_b):
    B, S, D = h.shape
    H, N = RWKV_HEADS, RWKV_HEAD
    xx = jnp.pad(h, ((0, 0), (1, 0), (0, 0)))[:, :-1] - h
    xr, xw, xk, xv, xa, xg = [h + xx * mu[i] for i in range(6)]
    r = xr @ w_r
    k = xk @ w_k
    v = xv @ w_v
    w_log = -jax.nn.softplus(-(w0 + jnp.tanh(xw @ wa) @ wb)) - 0.5
    a = jax.nn.sigmoid(a0 + (xa @ aa) @ ab)
    g = jax.nn.sigmoid(xg @ ga) @ gb

    def heads(t):
        return t.reshape(B, S, H, N).astype(jnp.float32)

    kk = heads(k * k_k)
    kk = kk / jnp.maximum(jnp.linalg.norm(kk, axis=-1, keepdims=True), 1e-12)
    k = k * (1 + (a - 1) * k_a)
    rf, kf, vf, af = heads(r), heads(k), heads(v), heads(a)
    decay = jnp.exp(-jnp.exp(heads(w_log)))

    def step(state, inp):
        r_t, d_t, k_t, v_t, kk_t, a_t = inp
        sa = jnp.einsum('bhvk,bhk->bhv', state, -kk_t)
        state = (state * d_t[:, :, None, :] + sa[..., None] * (kk_t * a_t)[:, :, None, :]
                 + v_t[..., None] * k_t[:, :, None, :])
        return state, jnp.einsum('bhvk,bhk->bhv', state, r_t)

    tm = lambda t: jnp.swapaxes(t, 0, 1)
    s0 = jnp.zeros((B, H, N, N), jnp.float32)
    _, y = lax.scan(step, s0, (tm(rf), tm(decay), tm(kf), tm(vf), tm(kk), tm(af)))
    y = tm(y)
    mean = jnp.mean(y, axis=-1, keepdims=True)
    var = jnp.mean(jnp.square(y - mean), axis=-1, keepdims=True)
    y = ((y - mean) * lax.rsqrt(var + GN_EPS)).reshape(B, S, D) * ln_g + ln_b
    bonus = jnp.sum(rf * kf * r_k, axis=-1, keepdims=True) * vf
    out = (y + bonus.reshape(B, S, D)) * g.astype(jnp.float32)
    return out.astype(h.dtype) @ w_o


def compress_blocks(tok, pos, w1, w2):
    B, S, G, Dh = tok.shape
    chunks = tok.reshape(B, S // CMP_STRIDE, CMP_STRIDE, G, Dh)
    n_cmp = S // CMP_STRIDE - CMP_RATIO + 1
    blocks = jnp.concatenate([chunks[:, r:r + n_cmp] for r in range(CMP_RATIO)], axis=2)
    blocks = blocks + pos[:, None, :]
    flat = jnp.transpose(blocks, (0, 1, 3, 2, 4)).reshape(B, n_cmp, G, CMP_LEN * Dh)
    return jax.nn.gelu(flat @ w1) @ w2


def build_shared_kv(x, c, kv_norm_g, kv_ada_w, kv_ada_b, kv_w, pos_k, pos_v,
                    k_w1, k_w2, v_w1, v_w2):
    B, S, _ = x.shape
    shift, scale = jnp.split(jax.nn.silu(c) @ kv_ada_w + kv_ada_b, 2, axis=-1)
    h = modulate(rms_norm(x, kv_norm_g), shift, scale)
    kv = (h @ kv_w).reshape(B, S, N_KV_SLOTS, NSA_GROUPS, HEAD_DIM)
    k_c, v_c, k_s, v_s, k_w, v_w = [kv[:, :, i] for i in range(N_KV_SLOTS)]
    k_cmp = compress_blocks(k_c, pos_k, k_w1, k_w2)
    v_cmp = compress_blocks(v_c, pos_v, v_w1, v_w2)
    to_blocks = lambda t: jnp.transpose(
        t.reshape(B, S // SLC_LEN, SLC_LEN, NSA_GROUPS, HEAD_DIM), (0, 3, 1, 2, 4))
    pad = ((0, 0), (WINDOW, 0), (0, 0), (0, 0))
    return (k_cmp, v_cmp, to_blocks(k_s), to_blocks(v_s), jnp.pad(k_w, pad), jnp.pad(v_w, pad))


def gather_blocks(blocks, idx):
    return jax.vmap(jax.vmap(lambda blk, ids: blk[ids]))(blocks, idx)


def nsa_attention(h, w_qg, w_o, k_cmp, v_cmp, k_slc, v_slc, k_win, v_win):
    B, S, _ = h.shape
    G, Hg, Dh, H = NSA_GROUPS, NSA_HPG, HEAD_DIM, NSA_HEADS
    qg = h @ w_qg
    q = (qg[..., :H * Dh] * Dh ** -0.5).reshape(B, S, G, Hg, Dh)
    gates = jax.nn.sigmoid(qg[..., H * Dh:].astype(jnp.float32)).reshape(B, S, G, Hg, 3)
    n_cmp = k_cmp.shape[1]
    n_slc = k_slc.shape[2]
    top_n = min(TOP_N, n_slc)
    cmp_end = jnp.arange(n_cmp) * CMP_STRIDE + CMP_LEN - 1
    blk_ids = jnp.arange(n_slc)
    offsets = [m - n + CMP_RATIO - 1 for m in range(SLC_RATIO) for n in range(CMP_RATIO)]
    pad_l = CMP_RATIO - 1
    pad_r = max(0, SLC_RATIO * (n_slc - 1) + max(offsets) + 1 - (n_cmp + pad_l))
    sel_off = jnp.arange(SLC_LEN)

    def block(i):
        t0 = i * Q_BLOCK
        t = t0 + jnp.arange(Q_BLOCK)
        qb = lax.dynamic_slice_in_dim(q, t0, Q_BLOCK, axis=1)
        gb = lax.dynamic_slice_in_dim(gates, t0, Q_BLOCK, axis=1)
        s_c = jnp.einsum('bqghd,bngd->bghqn', qb, k_cmp)
        p_c = masked_softmax(s_c, cmp_end[None, :] <= t[:, None])
        o_c = jnp.einsum('bghqn,bngd->bqghd', p_c.astype(v_cmp.dtype), v_cmp)
        imp = jnp.pad(jnp.sum(p_c, axis=2), ((0, 0), (0, 0), (0, 0), (pad_l, pad_r)))
        imp = sum(imp[..., off::SLC_RATIO][..., :n_slc] for off in offsets)
        cur = t // SLC_LEN
        forced = ((blk_ids[None] == 0) | (blk_ids[None] == cur[:, None])
                  | (blk_ids[None] == cur[:, None] - 1))
        imp = jnp.where(forced, FORCED_SCORE, imp)
        imp = jnp.where(blk_ids[None] > cur[:, None], -1.0, imp)
        _, idx = lax.top_k(imp, top_n)
        kb = gather_blocks(k_slc, idx).reshape(B, G, Q_BLOCK, top_n * SLC_LEN, Dh)
        vb = gather_blocks(v_slc, idx).reshape(B, G, Q_BLOCK, top_n * SLC_LEN, Dh)
        kpos = (idx[..., None] * SLC_LEN + sel_off).reshape(B, G, Q_BLOCK, top_n * SLC_LEN)
        s_s = jnp.einsum('bqghd,bgqkd->bghqk', qb, kb)
        p_s = masked_softmax(s_s, (kpos <= t[None, None, :, None])[:, :, None])
        o_s = jnp.einsum('bghqk,bgqkd->bqghd', p_s.astype(vb.dtype), vb)
        kw = lax.dynamic_slice_in_dim(k_win, t0, Q_BLOCK + WINDOW, axis=1)
        vw = lax.dynamic_slice_in_dim(v_win, t0, Q_BLOCK + WINDOW, axis=1)
        wpos = t0 - WINDOW + jnp.arange(Q_BLOCK + WINDOW)
        dist = t[:, None] - wpos[None, :]
        win_mask = (dist >= 0) & (dist < WINDOW) & (wpos[None, :] >= 0)
        s_w = jnp.einsum('bqghd,bkgd->bghqk', qb, kw)
        p_w = masked_softmax(s_w, win_mask)
        o_w = jnp.einsum('bghqk,bkgd->bqghd', p_w.astype(vw.dtype), vw)
        o = gb[..., 0:1] * o_c + gb[..., 1:2] * o_s + gb[..., 2:3] * o_w
        return o.astype(h.dtype)

    o = lax.map(block, jnp.arange(S // Q_BLOCK))
    o = jnp.moveaxis(o, 0, 1).reshape(B, S, H * Dh)
    return o @ w_o


def setup_inputs(seed: int = 0) -> dict:
    key = jax.random.key(seed)
    ks = iter(jax.random.split(key, 48))
    f32 = jnp.float32
    nrm = lambda shape, s: s * jax.random.normal(next(ks), shape, f32)
    D, A, Bn = D_MODEL, N_A_LAYERS, N_B_LAYERS
    GD = NSA_GROUPS * HEAD_DIM
    return {
        'x': nrm((BATCH, SEQ, D), 1.0),
        'c': nrm((BATCH, D), 1.0),
        'ada_w': nrm((DEPTH, D, 6 * D), D ** -0.5),
        'ada_b': nrm((DEPTH, 6 * D), 0.01),
        'norm_g': 1.0 + nrm((DEPTH, 4, D), 0.02),
        'mlp_up': nrm((DEPTH, D, D_FF), D ** -0.5),
        'mlp_down': nrm((DEPTH, D_FF, D), D_FF ** -0.5),
        'rw_mu': jax.random.uniform(next(ks), (A, 6, D), f32),
        'rw_r': nrm((A, D, D), D ** -0.5),
        'rw_k': nrm((A, D, D), D ** -0.5),
        'rw_v': nrm((A, D, D), D ** -0.5),
        'rw_o': nrm((A, D, D), D ** -0.5),
        'rw_w0': jnp.linspace(-6.5, -1.5, D, dtype=f32)[None] + nrm((A, D), 0.1),
        'rw_wa': nrm((A, D, D_DECAY_LORA), D ** -0.5),
        'rw_wb': nrm((A, D_DECAY_LORA, D), 0.1 * D_DECAY_LORA ** -0.5),
        'rw_a0': nrm((A, D), 0.1),
        'rw_aa': nrm((A, D, D_AAA_LORA), D ** -0.5),
        'rw_ab': nrm((A, D_AAA_LORA, D), 0.5 * D_AAA_LORA ** -0.5),
        'rw_ga': nrm((A, D, D_GATE_LORA), D ** -0.5),
        'rw_gb': nrm((A, D_GATE_LORA, D), D_GATE_LORA ** -0.5),
        'rw_kk': 0.85 + nrm((A, D), 0.05),
        'rw_ka': 1.0 + nrm((A, D), 0.05),
        'rw_rk': -0.04 + nrm((A, RWKV_HEADS, RWKV_HEAD), 0.05),
        'rw_lng': 1.0 + nrm((A, D), 0.02),
        'rw_lnb': nrm((A, D), 0.01),
        'kv_norm_g': 1.0 + nrm((D,), 0.02),
        'kv_ada_w': nrm((D, 2 * D), D ** -0.5),
        'kv_ada_b': nrm((2 * D,), 0.01),
        'kv_w': nrm((D, N_KV_SLOTS * GD), D ** -0.5),
        'cmp_pos_k': nrm((CMP_LEN, HEAD_DIM), 0.1),
        'cmp_pos_v': nrm((CMP_LEN, HEAD_DIM), 0.1),
        'cmp_k_w1': nrm((CMP_LEN * HEAD_DIM, PHI_HIDDEN), (CMP_LEN * HEAD_DIM) ** -0.5),
        'cmp_k_w2': nrm((PHI_HIDDEN, HEAD_DIM), PHI_HIDDEN ** -0.5),
        'cmp_v_w1': nrm((CMP_LEN * HEAD_DIM, PHI_HIDDEN), (CMP_LEN * HEAD_DIM) ** -0.5),
        'cmp_v_w2': nrm((PHI_HIDDEN, HEAD_DIM), PHI_HIDDEN ** -0.5),
        'nsa_wqg': nrm((Bn, D, NSA_HEADS * HEAD_DIM + 3 * NSA_HEADS), D ** -0.5),
        'nsa_wo': nrm((Bn, NSA_HEADS * HEAD_DIM, D), (NSA_HEADS * HEAD_DIM) ** -0.5),
    }


def reference(x, c, ada_w, ada_b, norm_g, mlp_up, mlp_down,
              rw_mu, rw_r, rw_k, rw_v, rw_o, rw_w0, rw_wa, rw_wb, rw_a0, rw_aa, rw_ab,
              rw_ga, rw_gb, rw_kk, rw_ka, rw_rk, rw_lng, rw_lnb,
              kv_norm_g, kv_ada_w, kv_ada_b, kv_w, cmp_pos_k, cmp_pos_v,
              cmp_k_w1, cmp_k_w2, cmp_v_w1, cmp_v_w2, nsa_wqg, nsa_wo):
    shared = None
    for layer in range(DEPTH):
        sh1, sc1, gt1, sh2, sc2, gt2 = jnp.split(
            jax.nn.silu(c) @ ada_w[layer] + ada_b[layer], 6, axis=-1)
        h = modulate(rms_norm(x, norm_g[layer, 0]), sh1, sc1)
        if layer < N_A_LAYERS:
            i = layer
            y = rwkv7_time_mix(h, rw_mu[i], rw_r[i], rw_k[i], rw_v[i], rw_o[i], rw_w0[i],
                               rw_wa[i], rw_wb[i], rw_a0[i], rw_aa[i], rw_ab[i],
                               rw_ga[i], rw_gb[i], rw_kk[i], rw_ka[i], rw_rk[i],
                               rw_lng[i], rw_lnb[i])
        else:
            if layer == N_A_LAYERS:
                shared = build_shared_kv(x, c, kv_norm_g, kv_ada_w, kv_ada_b, kv_w,
                                         cmp_pos_k, cmp_pos_v, cmp_k_w1, cmp_k_w2,
                                         cmp_v_w1, cmp_v_w2)
            j = layer - N_A_LAYERS
            y = nsa_attention(h, nsa_wqg[j], nsa_wo[j], *shared)
        x = x + gt1[:, None, :] * rms_norm(y, norm_g[layer, 1])
        h = modulate(rms_norm(x, norm_g[layer, 2]), sh2, sc2)
        y = sq_relu_mlp(h, mlp_up[layer], mlp_down[layer])
        x = x + gt2[:, None, :] * rms_norm(y, norm_g[layer, 3])
    return x
```

```python
import functools
import math

import numpy as np
import jax
import jax.numpy as jnp
from jax import lax
from jax.experimental import pallas as pl
from jax.experimental.pallas import tpu as pltpu

F32 = jnp.float32
BF16 = jnp.bfloat16
HI = lax.Precision.HIGHEST

NORM_EPS = 1e-6
GN_EPS = 64e-5
RWKV_HEAD = 64
HEAD_DIM = 64
NSA_GROUPS = 4
NSA_HPG = 4
NSA_HEADS = 16
CMP_LEN = 32
CMP_STRIDE = 16
CMP_RATIO = CMP_LEN // CMP_STRIDE
SLC_LEN = 64
SLC_RATIO = SLC_LEN // CMP_STRIDE
TOP_N = 16
WINDOW = 512
FORCED_SCORE = 1e4
NEG_INF = -1e30

LANES = 128
VMEM_LIMIT = 56 * 1024 * 1024
REC_CHUNK = 64


def _cp(*sem):
    return pltpu.CompilerParams(dimension_semantics=sem, vmem_limit_bytes=VMEM_LIMIT)


def _dot(a, b, **kw):
    return jnp.dot(a, b, preferred_element_type=F32, **kw)


def _dot_nt(a, b, **kw):
    return lax.dot_general(a, b, (((1,), (1,)), ((), ())), preferred_element_type=F32, **kw)


def _dot_tn(a, b, **kw):
    return lax.dot_general(a, b, (((0,), (0,)), ((), ())), preferred_element_type=F32, **kw)


def _rms(x, g):
    return x * lax.rsqrt(jnp.mean(x * x, axis=-1, keepdims=True) + NORM_EPS) * g


def _ada_body(c_ref, w_ref, b_ref, o_ref):
    c = c_ref[...]
    o_ref[...] = _dot(c * jax.nn.sigmoid(c), w_ref[...], precision=HI) + b_ref[...]


def _ada(c8, w, b):
    L, D, N = w.shape
    tn = 1024
    return pl.pallas_call(
        _ada_body, grid=(L, N // tn),
        in_specs=[pl.BlockSpec((8, D), lambda l, j: (0, 0)),
                  pl.BlockSpec((None, D, tn), lambda l, j: (l, 0, j)),
                  pl.BlockSpec((None, 1, tn), lambda l, j: (l, 0, j))],
        out_specs=pl.BlockSpec((None, 8, tn), lambda l, j: (l, 0, j)),
        out_shape=jax.ShapeDtypeStruct((L, 8, N), F32),
        compiler_params=_cp("parallel", "parallel"), name="ada",
    )(c8, w, b.reshape(L, 1, N))


def _rw_front_body(x_ref, xp_ref, mod_ref, g_ref, mu_ref, wr, wk, wv, wa, wb, w0, aa, ab, a0, ga, gb,
                   kk_ref, ka_ref, r_o, k_o, v_o, lw_o, kk_o, a_o, g_o):
    i = pl.program_id(1)
    g = g_ref[...]
    sh, sc = mod_ref[0:1, :], mod_ref[1:2, :]
    h = _rms(x_ref[...], g) * (1 + sc) + sh
    hp = _rms(xp_ref[...], g) * (1 + sc) + sh
    hp = jnp.where(i > 0, hp[7:8, :], 0.0)
    row = lax.broadcasted_iota(jnp.int32, h.shape, 0)
    hs = jnp.where(row == 0, hp, pltpu.roll(h, 1, axis=0))
    xx = hs - h
    mu = mu_ref[...]
    xr, xw, xk, xv, xa, xg = [(h + xx * mu[j:j + 1, :]).astype(BF16) for j in range(6)]
    r = _dot(xr, wr[...])
    k = _dot(xk, wk[...])
    v = _dot(xv, wv[...])
    wl = w0[...] + _dot(jnp.tanh(_dot(xw, wa[...])).astype(BF16), wb[...])
    lw = -jax.nn.sigmoid(wl) * math.exp(-0.5)
    a = jax.nn.sigmoid(a0[...] + _dot(_dot(xa, aa[...]).astype(BF16), ab[...]))
    gate = _dot(jax.nn.sigmoid(_dot(xg, ga[...])).astype(BF16), gb[...])
    r_o[...] = r
    k_o[...] = k * (1 + (a - 1) * ka_ref[...])
    v_o[...] = v
    lw_o[...] = lw
    kk_o[...] = k * kk_ref[...]
    a_o[...] = a
    g_o[...] = gate


def _rw_front(x, mod, g, mu, wr, wk, wv, wa, wb, w0, aa, ab, a0, ga, gb, kk, ka, ts):
    B, S, D = x.shape
    full = lambda a: pl.BlockSpec(a.shape, lambda b, i: (0,) * a.ndim)
    tok = pl.BlockSpec((None, ts, D), lambda b, i: (b, i, 0))
    ws = [wr, wk, wv, wa, wb, w0, aa, ab, a0, ga, gb, kk, ka]
    return pl.pallas_call(
        _rw_front_body, grid=(B, S // ts),
        in_specs=[tok,
                  pl.BlockSpec((None, 8, D), lambda b, i: (b, jnp.maximum(i * (ts // 8) - 1, 0), 0)),
                  pl.BlockSpec((None, 6, D), lambda b, i: (b, 0, 0)),
                  full(g), full(mu)] + [full(w) for w in ws],
        out_specs=[tok] * 7,
        out_shape=[jax.ShapeDtypeStruct((B, S, D), F32)] * 7,
        compiler_params=_cp("parallel", "parallel"), name="rwkv_front",
    )(x, x, mod, g, mu, *ws)


def _rec_body(r_ref, k_ref, v_ref, lw_ref, kk_ref, a_ref, g_ref, rk_ref, lng_ref, lnb_ref, o_ref, st_ref,
              *, n_chunks):
    C = REC_CHUNK
    L = LANES

    @pl.when(pl.program_id(2) == 0)
    def _():
        st_ref[...] = jnp.zeros_like(st_ref)

    lane = lax.broadcasted_iota(jnp.int32, (C, L), 1)
    h0 = lane < RWKV_HEAD
    bd = (lax.broadcasted_iota(jnp.int32, (L, L), 0) // RWKV_HEAD
          == lax.broadcasted_iota(jnp.int32, (L, L), 1) // RWKV_HEAD)
    bd_f = bd.astype(F32)
    rr = lax.broadcasted_iota(jnp.int32, (C, C), 0)
    cc = lax.broadcasted_iota(jnp.int32, (C, C), 1)
    incl = rr >= cc
    strict = rr > cc
    incl_f = incl.astype(F32)
    eye = (rr == cc).astype(F32)
    lvl_masks = [(rr // sz == cc // sz) & (rr // (sz // 2) != cc // (sz // 2))
                 for sz in (2 ** e for e in range(1, int(math.log2(C)) + 1))]
    rk = rk_ref[...]
    lng = lng_ref[...]
    lnb = lnb_ref[...]
    bf = lambda t: t.astype(BF16)

    def head_sum(t):
        return _dot(t, bd_f, precision=HI)

    for c in range(n_chunks):
        sl = pl.ds(c * C, C)
        r, k, v = r_ref[sl, :], k_ref[sl, :], v_ref[sl, :]
        lw, kkr, a, gate = lw_ref[sl, :], kk_ref[sl, :], a_ref[sl, :], g_ref[sl, :]
        kkn = kkr / jnp.maximum(jnp.sqrt(head_sum(kkr * kkr)), 1e-12)
        cl = _dot(incl_f, lw, precision=HI)
        cl_end = cl[C - 1:C, :]
        rt = r * jnp.exp(cl)
        at = -kkn * jnp.exp(cl - lw)
        e_neg = jnp.exp(-cl)
        bt = kkn * a * e_neg
        kt = k * e_neg
        e_end = jnp.exp(cl_end - cl)
        b_end = kkn * a * e_end
        k_end = k * e_end
        zero = jnp.zeros_like(at)
        lhs = bf(jnp.concatenate([jnp.where(h0, at, zero), jnp.where(h0, zero, at),
                                  jnp.where(h0, rt, zero), jnp.where(h0, zero, rt)], axis=0))
        gb = _dot_nt(lhs, bf(bt))
        gk = _dot_nt(lhs, bf(kt))
        v_b, at_b = bf(v), bf(at)
        w_parts, uh_parts = [], []
        for hd in range(2):
            a_ab = jnp.where(strict, gb[hd * C:(hd + 1) * C], 0.0)
            a_ak = jnp.where(strict, gk[hd * C:(hd + 1) * C], 0.0)
            x = eye + jnp.where(lvl_masks[0], a_ab, 0.0)
            for lm in lvl_masks[1:]:
                x_b = bf(x)
                x = x + _dot(bf(_dot(x_b, bf(jnp.where(lm, a_ab, 0.0)))), x_b)
            w_parts.append(_dot(bf(x), at_b))
            uh_parts.append(_dot(bf(_dot(bf(x), bf(a_ak))), v_b))
        wmat = jnp.where(h0, w_parts[0], w_parts[1])
        uh = jnp.where(h0, uh_parts[0], uh_parts[1])

        n0 = st_ref[...]
        n0_b = bf(n0)
        u = _dot_nt(bf(wmat), n0_b) + uh
        u_b = bf(u)
        y_parts = []
        for hd in range(2):
            a_rb = jnp.where(incl, gb[(2 + hd) * C:(3 + hd) * C], 0.0)
            a_rk = jnp.where(incl, gk[(2 + hd) * C:(3 + hd) * C], 0.0)
            y_parts.append(_dot(bf(a_rb), u_b) + _dot(bf(a_rk), v_b))
        y = _dot_nt(bf(rt), n0_b) + jnp.where(h0, y_parts[0], y_parts[1])
        upd = _dot_tn(bf(jnp.concatenate([u, v], axis=0)), bf(jnp.concatenate([b_end, k_end], axis=0)))
        st_ref[...] = n0 * jnp.exp(cl_end) + jnp.where(bd, upd, 0.0)

        mean = head_sum(y) * (1.0 / RWKV_HEAD)
        yc = y - mean
        var = head_sum(yc * yc) * (1.0 / RWKV_HEAD)
        yn = yc * lax.rsqrt(var + GN_EPS) * lng + lnb
        bonus = head_sum(r * k * rk) * v
        o_ref[sl, :] = (yn + bonus) * gate


def _rw_recurrence(r, k, v, lw, kk, a, g, rk, lng, lnb, tt):
    B, S, D = r.shape
    tok = pl.BlockSpec((None, tt, LANES), lambda b, hp, t: (b, t, hp))
    vec = pl.BlockSpec((1, LANES), lambda b, hp, t: (0, hp))
    return pl.pallas_call(
        functools.partial(_rec_body, n_chunks=tt // REC_CHUNK),
        grid=(B, D // LANES, S // tt),
        in_specs=[tok] * 7 + [vec] * 3,
        out_specs=tok,
        out_shape=jax.ShapeDtypeStruct((B, S, D), F32),
        scratch_shapes=[pltpu.VMEM((LANES, LANES), F32)],
        compiler_params=_cp("parallel", "parallel", "arbitrary"), name="rwkv_recurrence",
    )(r, k, v, lw, kk, a, g, rk, lng, lnb)


def _proj_res_body(y_ref, w_ref, x_ref, mod_ref, g_ref, o_ref):
    y = _dot(y_ref[...].astype(BF16), w_ref[...])
    o_ref[...] = x_ref[...] + mod_ref[2:3, :] * _rms(y, g_ref[...])


def _proj_res(y, w, x, mod, g, ts):
    B, S, D = x.shape
    tok = pl.BlockSpec((None, ts, D), lambda b, i: (b, i, 0))
    return pl.pallas_call(
        _proj_res_body, grid=(B, S // ts),
        in_specs=[tok, pl.BlockSpec(w.shape, lambda b, i: (0, 0)), tok,
                  pl.BlockSpec((None, 6, D), lambda b, i: (b, 0, 0)),
                  pl.BlockSpec(g.shape, lambda b, i: (0, 0))],
        out_specs=tok, out_shape=jax.ShapeDtypeStruct((B, S, D), F32),
        compiler_params=_cp("parallel", "parallel"), name="proj_residual",
    )(y, w, x, mod, g)


def _mlp_body(x_ref, mod_ref, g2_ref, g3_ref, up_ref, dn_ref, o_ref, h_ref, acc_ref):
    f = pl.program_id(2)

    @pl.when(f == 0)
    def _():
        h = _rms(x_ref[...], g2_ref[...]) * (1 + mod_ref[4:5, :]) + mod_ref[3:4, :]
        h_ref[...] = h.astype(BF16)
        acc_ref[...] = jnp.zeros_like(acc_ref)

    z = jnp.maximum(_dot(h_ref[...], up_ref[...]), 0.0)
    acc_ref[...] += _dot((z * z).astype(BF16), dn_ref[...])

    @pl.when(f == pl.num_programs(2) - 1)
    def _():
        o_ref[...] = x_ref[...] + mod_ref[5:6, :] * _rms(acc_ref[...], g3_ref[...])


def _mlp(x, mod, g2, g3, up, dn, ts, tf):
    B, S, D = x.shape
    F = up.shape[1]
    tok = pl.BlockSpec((None, ts, D), lambda b, i, f: (b, i, 0))
    vec = pl.BlockSpec((1, D), lambda b, i, f: (0, 0))
    return pl.pallas_call(
        _mlp_body, grid=(B, S // ts, F // tf),
        in_specs=[tok, pl.BlockSpec((None, 6, D), lambda b, i, f: (b, 0, 0)), vec, vec,
                  pl.BlockSpec((D, tf), lambda b, i, f: (0, f)),
                  pl.BlockSpec((tf, D), lambda b, i, f: (f, 0))],
        out_specs=tok, out_shape=jax.ShapeDtypeStruct((B, S, D), F32),
        scratch_shapes=[pltpu.VMEM((ts, D), BF16), pltpu.VMEM((ts, D), F32)],
        compiler_params=_cp("parallel", "parallel", "arbitrary"), name="mlp",
    )(x, mod, g2, g3, up, dn)


def _nsa_proj_body(x_ref, mod_ref, mkv_ref, g_ref, gkv_ref, wq_ref, wg_ref, wc_ref, wt_ref,
                   q_o, gt_o, c_o, t_o):
    x = x_ref[...]
    xn = x * lax.rsqrt(jnp.mean(x * x, axis=-1, keepdims=True) + NORM_EPS)
    h = ((xn * g_ref[...]) * (1 + mod_ref[1:2, :]) + mod_ref[0:1, :]).astype(BF16)
    hkv = ((xn * gkv_ref[...]) * (1 + mkv_ref[1:2, :]) + mkv_ref[0:1, :]).astype(BF16)
    q_o[...] = (_dot(h, wq_ref[...]) * HEAD_DIM ** -0.5).astype(BF16)
    gt_o[...] = jax.nn.sigmoid(_dot(h, wg_ref[...]))
    c_o[...] = _dot(hkv, wc_ref[...])
    t_o[...] = _dot_nt(wt_ref[...], hkv).astype(BF16)


def _nsa_proj(x, mod, mkv, g, gkv, wq, wg, wc, wt, ts):
    B, S, D = x.shape
    tok = lambda n: pl.BlockSpec((None, ts, n), lambda b, i: (b, i, 0))
    full = lambda a: pl.BlockSpec(a.shape, lambda b, i: (0,) * a.ndim)
    return pl.pallas_call(
        _nsa_proj_body, grid=(B, S // ts),
        in_specs=[tok(D), pl.BlockSpec((None, 6, D), lambda b, i: (b, 0, 0)),
                  pl.BlockSpec((None, 2, D), lambda b, i: (b, 0, 0)),
                  full(g), full(gkv), full(wq), full(wg), full(wc), full(wt)],
        out_specs=[tok(wq.shape[1]), tok(wg.shape[1]), tok(wc.shape[1]),
                   pl.BlockSpec((None, wt.shape[0], ts), lambda b, i: (b, 0, i))],
        out_shape=[jax.ShapeDtypeStruct((B, S, wq.shape[1]), BF16),
                   jax.ShapeDtypeStruct((B, S, wg.shape[1]), F32),
                   jax.ShapeDtypeStruct((B, S, wc.shape[1]), F32),
                   jax.ShapeDtypeStruct((B, wt.shape[0], S), BF16)],
        compiler_params=_cp("parallel", "parallel"), name="nsa_proj",
    )(x, mod, mkv, g, gkv, wq, wg, wc, wt)


def _cmp1_body(k0_ref, k1_ref, v0_ref, v1_ref, pk_ref, pv_ref, wk_ref, wv_ref, fk_o, fv_o, *, n_chunk):
    hid = wk_ref.shape[1]
    gpr = LANES // HEAD_DIM
    for srcs, pos, w1, out in (((k0_ref, k1_ref), pk_ref, wk_ref, fk_o), ((v0_ref, v1_ref), pv_ref, wv_ref, fv_o)):
        acc = [[jnp.zeros((n_chunk, hid), F32) for _ in range(CMP_RATIO)] for _ in range(NSA_GROUPS)]
        for l in range(CMP_STRIDE):
            for si, src in enumerate(srcs):
                rows = src[pl.ds(l, n_chunk, stride=CMP_STRIDE), :]
                for gi in range(gpr):
                    g = si * gpr + gi
                    t = rows[:, gi * HEAD_DIM:(gi + 1) * HEAD_DIM]
                    for half in range(CMP_RATIO):
                        p = half * CMP_STRIDE + l
                        lhs = (t + pos[p:p + 1, :]).astype(BF16)
                        acc[g][half] = acc[g][half] + _dot(lhs, w1[p * HEAD_DIM:(p + 1) * HEAD_DIM, :])
        for g in range(NSA_GROUPS):
            for half in range(CMP_RATIO):
                out[g, :, half * hid:(half + 1) * hid] = acc[g][half]


def _cmp1(kvc, pos_k, pos_v, w1k, w1v, ts):
    B, S, _ = kvc.shape
    G = NSA_GROUPS
    hid = w1k.shape[1]
    nch = ts // CMP_STRIDE
    full = lambda a: pl.BlockSpec(a.shape, lambda b, i: (0,) * a.ndim)
    src = lambda j: pl.BlockSpec((None, ts, LANES), lambda b, i: (b, i, j))
    outs = pl.BlockSpec((None, G, nch, CMP_RATIO * hid), lambda b, i: (b, 0, i, 0))
    return pl.pallas_call(
        functools.partial(_cmp1_body, n_chunk=nch), grid=(B, S // ts),
        in_specs=[src(0), src(1), src(2), src(3), full(pos_k), full(pos_v), full(w1k), full(w1v)],
        out_specs=[outs, outs],
        out_shape=[jax.ShapeDtypeStruct((B, G, S // CMP_STRIDE, CMP_RATIO * hid), F32)] * 2,
        compiler_params=_cp("parallel", "parallel"), name="cmp_stage1",
    )(kvc, kvc, kvc, kvc, pos_k, pos_v, w1k, w1v)


def _gelu_tanh(x):
    return 0.5 * x * (1.0 + jnp.tanh(math.sqrt(2.0 / math.pi) * (x + 0.044715 * (x * x * x))))


def _cmp2_body(fk_ref, fv_ref, w2kt_ref, w2v_ref, kt_o, v_o):
    nc = fk_ref.shape[0]
    hid = w2v_ref.shape[0]
    row = lax.broadcasted_iota(jnp.int32, (nc, hid), 0)

    def hidden(f):
        nxt = jnp.where(row == nc - 1, 0.0, pltpu.roll(f[:, hid:2 * hid], nc - 1, axis=0))
        return _gelu_tanh(f[:, 0:hid] + nxt).astype(BF16)

    kt_o[...] = _dot_nt(w2kt_ref[...], hidden(fk_ref[...])).astype(BF16)
    v_o[...] = _dot(hidden(fv_ref[...]), w2v_ref[...]).astype(BF16)


def _cmp2(fk, fv, w2kt, w2v):
    B, G, NC, H2 = fk.shape
    fin = pl.BlockSpec((None, None, NC, H2), lambda b, g: (b, g, 0, 0))
    return pl.pallas_call(
        _cmp2_body, grid=(B, G),
        in_specs=[fin, fin, pl.BlockSpec(w2kt.shape, lambda b, g: (0, 0)),
                  pl.BlockSpec(w2v.shape, lambda b, g: (0, 0))],
        out_specs=[pl.BlockSpec((None, None, HEAD_DIM, NC), lambda b, g: (b, g, 0, 0)),
                   pl.BlockSpec((None, None, NC, HEAD_DIM), lambda b, g: (b, g, 0, 0))],
        out_shape=[jax.ShapeDtypeStruct((B, G, HEAD_DIM, NC), BF16),
                   jax.ShapeDtypeStruct((B, G, NC, HEAD_DIM), BF16)],
        compiler_params=_cp("parallel", "parallel"), name="cmp_stage2",
    )(fk, fv, w2kt, w2v)


def _stack_heads(q):
    return jnp.concatenate([q[:, h * HEAD_DIM:(h + 1) * HEAD_DIM] for h in range(NSA_HPG)], axis=0)


def _store_heads(o_ref, o, tq):
    for h in range(NSA_HPG):
        o_ref[:, h * HEAD_DIM:(h + 1) * HEAD_DIM] = o[h * tq:(h + 1) * tq, :]


def _cmp_attn_body(q_ref, kt_ref, v_ref, wimp_ref, o_ref, sel_ref, *, tq):
    i = pl.program_id(2)
    nc = kt_ref.shape[1]
    nb = wimp_ref.shape[1]
    q = _stack_heads(q_ref[...])
    s = _dot(q, kt_ref[...]).reshape(NSA_HPG, tq, nc)
    t = i * tq + lax.broadcasted_iota(jnp.int32, (tq, nc), 0)
    cmp_end = lax.broadcasted_iota(jnp.int32, (tq, nc), 1) * CMP_STRIDE + (CMP_LEN - 1)
    mask = (cmp_end <= t)[None]
    s = jnp.where(mask, s, NEG_INF)
    p = jnp.where(mask, jnp.exp(s - jnp.max(s, axis=-1, keepdims=True)), 0.0)
    p = p / jnp.maximum(jnp.sum(p, axis=-1, keepdims=True), 1e-30)
    o = _dot(p.reshape(NSA_HPG * tq, nc).astype(BF16), v_ref[...])
    _store_heads(o_ref, o, tq)

    imp = _dot(jnp.sum(p, axis=0), wimp_ref[...], precision=HI)
    blk = lax.broadcasted_iota(jnp.int32, (tq, nb), 1)
    cur = (i * tq + lax.broadcasted_iota(jnp.int32, (tq, nb), 0)) // SLC_LEN
    forced = (blk == 0) | (blk == cur) | (blk == cur - 1)
    imp = jnp.where(forced, FORCED_SCORE, imp)
    imp = jnp.where(blk > cur, -1.0, imp)
    blk_f = blk.astype(F32)

    def pick(_, carry):
        imp, sel = carry
        m = jnp.max(imp, axis=-1, keepdims=True)
        first = jnp.min(jnp.where(imp == m, blk_f, float(nb)), axis=-1, keepdims=True)
        hit = blk_f == first
        return jnp.where(hit, -3e38, imp), jnp.where(hit, 1.0, sel)

    _, sel = lax.fori_loop(0, min(TOP_N, nb), pick, (imp, jnp.zeros((tq, nb), F32)))
    sel_ref[...] = sel.astype(BF16)


def _cmp_attn(q, kct, vc, wimp, tq):
    B, S, _ = q.shape
    G = NSA_GROUPS
    NC = kct.shape[3]
    NB = wimp.shape[1]
    GD = NSA_HPG * HEAD_DIM
    return pl.pallas_call(
        functools.partial(_cmp_attn_body, tq=tq), grid=(B, G, S // tq),
        in_specs=[pl.BlockSpec((None, tq, GD), lambda b, g, i: (b, i, g)),
                  pl.BlockSpec((None, None, HEAD_DIM, NC), lambda b, g, i: (b, g, 0, 0)),
                  pl.BlockSpec((None, None, NC, HEAD_DIM), lambda b, g, i: (b, g, 0, 0)),
                  pl.BlockSpec(wimp.shape, lambda b, g, i: (0, 0))],
        out_specs=[pl.BlockSpec((None, tq, GD), lambda b, g, i: (b, i, g)),
                   pl.BlockSpec((None, None, tq, NB), lambda b, g, i: (b, g, i, 0))],
        out_shape=[jax.ShapeDtypeStruct((B, S, G * GD), F32),
                   jax.ShapeDtypeStruct((B, G, S, NB), BF16)],
        compiler_params=_cp("parallel", "parallel", "parallel"), name="cmp_attention_topk",
    )(q, kct, vc, wimp)


def _slc_attn_body(q_ref, sel_ref, kt_ref, vt_ref, e_ref, o_ref, *, tq, tk):
    i = pl.program_id(2)
    rows = NSA_HPG * tq
    q = _stack_heads(q_ref[...])
    sel = sel_ref[...]
    t = i * tq + lax.broadcasted_iota(jnp.int32, (tq, tk), 0)
    kofs = lax.broadcasted_iota(jnp.int32, (tq, tk), 1)

    def step(j, carry):
        m, l, acc = carry
        ks = pl.ds(pl.multiple_of(j * tk, tk), tk)
        s = _dot(q, kt_ref[:, ks]).reshape(NSA_HPG, tq, tk)
        chosen = _dot(sel, e_ref[:, ks])
        mask = ((chosen > 0.5) & (j * tk + kofs <= t))[None]
        s = jnp.where(mask, s, NEG_INF).reshape(rows, tk)
        m_new = jnp.maximum(m, jnp.max(s, axis=-1, keepdims=True))
        alpha = jnp.exp(m - m_new)
        p = jnp.exp(s - m_new)
        l = l * alpha + jnp.sum(p, axis=-1, keepdims=True)
        acc = acc * alpha + _dot_nt(p.astype(BF16), vt_ref[:, ks])
        return m_new, l, acc

    n_tiles = (i * tq + tq - 1) // tk + 1
    m, l, acc = lax.fori_loop(0, n_tiles, step, (jnp.full((rows, 1), NEG_INF, F32),
                                                 jnp.zeros((rows, 1), F32),
                                                 jnp.zeros((rows, HEAD_DIM), F32)))
    _store_heads(o_ref, acc / jnp.maximum(l, 1e-30), tq)


def _slc_attn(q, sel, kvt, e, tq, tk):
    B, S, _ = q.shape
    G = NSA_GROUPS
    NB = sel.shape[3]
    GD = NSA_HPG * HEAD_DIM
    return pl.pallas_call(
        functools.partial(_slc_attn_body, tq=tq, tk=tk), grid=(B, G, S // tq),
        in_specs=[pl.BlockSpec((None, tq, GD), lambda b, g, i: (b, i, g)),
                  pl.BlockSpec((None, None, tq, NB), lambda b, g, i: (b, g, i, 0)),
                  pl.BlockSpec((None, HEAD_DIM, S), lambda b, g, i: (b, g, 0)),
                  pl.BlockSpec((None, HEAD_DIM, S), lambda b, g, i: (b, G + g, 0)),
                  pl.BlockSpec(e.shape, lambda b, g, i: (0, 0))],
        out_specs=pl.BlockSpec((None, tq, GD), lambda b, g, i: (b, i, g)),
        out_shape=jax.ShapeDtypeStruct((B, S, G * GD), F32),
        compiler_params=_cp("parallel", "parallel", "parallel"), name="slc_attention",
    )(q, sel, kvt, kvt, e)


def _win_attn_body(q_ref, kt_ref, vt_ref, o_ref, *, tq, span):
    i = pl.program_id(2)
    q = _stack_heads(q_ref[...])
    start = pl.multiple_of(jnp.maximum(i * tq - WINDOW, 0), LANES)
    ks = pl.ds(start, span)
    s = _dot(q, kt_ref[:, ks]).reshape(NSA_HPG, tq, span)
    t = i * tq + lax.broadcasted_iota(jnp.int32, (tq, span), 0)
    dist = t - (start + lax.broadcasted_iota(jnp.int32, (tq, span), 1))
    mask = ((dist >= 0) & (dist < WINDOW))[None]
    s = jnp.where(mask, s, NEG_INF)
    p = jnp.where(mask, jnp.exp(s - jnp.max(s, axis=-1, keepdims=True)), 0.0)
    p = p / jnp.maximum(jnp.sum(p, axis=-1, keepdims=True), 1e-30)
    o = _dot_nt(p.reshape(NSA_HPG * tq, span).astype(BF16), vt_ref[:, ks])
    _store_heads(o_ref, o, tq)


def _win_attn(q, kvt, tq):
    B, S, _ = q.shape
    G = NSA_GROUPS
    GD = NSA_HPG * HEAD_DIM
    span = min(tq + WINDOW, S)
    return pl.pallas_call(
        functools.partial(_win_attn_body, tq=tq, span=span), grid=(B, G, S // tq),
        in_specs=[pl.BlockSpec((None, tq, GD), lambda b, g, i: (b, i, g)),
                  pl.BlockSpec((None, HEAD_DIM, S), lambda b, g, i: (b, 2 * G + g, 0)),
                  pl.BlockSpec((None, HEAD_DIM, S), lambda b, g, i: (b, 3 * G + g, 0))],
        out_specs=pl.BlockSpec((None, tq, GD), lambda b, g, i: (b, i, g)),
        out_shape=jax.ShapeDtypeStruct((B, S, G * GD), F32),
        compiler_params=_cp("parallel", "parallel", "parallel"), name="win_attention",
    )(q, kvt, kvt)


def _nsa_out_body(oc_ref, os_ref, ow_ref, gt_ref, ge_ref, w_ref, x_ref, mod_ref, g_ref, o_ref):
    D = oc_ref.shape[1]
    gx = _dot(gt_ref[...], ge_ref[...], precision=HI)
    o = gx[:, 0:D] * oc_ref[...] + gx[:, D:2 * D] * os_ref[...] + gx[:, 2 * D:3 * D] * ow_ref[...]
    y = _dot(o.astype(BF16), w_ref[...])
    o_ref[...] = x_ref[...] + mod_ref[2:3, :] * _rms(y, g_ref[...])


def _nsa_out(oc, osl, ow, gates, gexp, w, x, mod, g, ts):
    B, S, D = x.shape
    tok = lambda n: pl.BlockSpec((None, ts, n), lambda b, i: (b, i, 0))
    full = lambda a: pl.BlockSpec(a.shape, lambda b, i: (0,) * a.ndim)
    return pl.pallas_call(
        _nsa_out_body, grid=(B, S // ts),
        in_specs=[tok(D), tok(D), tok(D), tok(gates.shape[2]), full(gexp), full(w), tok(D),
                  pl.BlockSpec((None, 6, D), lambda b, i: (b, 0, 0)), full(g)],
        out_specs=tok(D), out_shape=jax.ShapeDtypeStruct((B, S, D), F32),
        compiler_params=_cp("parallel", "parallel"), name="nsa_out",
    )(oc, osl, ow, gates, gexp, w, x, mod, g)


def _importance_matrix(nc, nb):
    n_cmp = nc - CMP_RATIO + 1
    w = np.zeros((nc, nb), np.float32)
    for m in range(SLC_RATIO):
        for n in range(CMP_RATIO):
            off = m - n + CMP_RATIO - 1
            for j in range(nb):
                src = SLC_RATIO * j + off - (CMP_RATIO - 1)
                if 0 <= src < n_cmp:
                    w[src, j] += 1.0
    return jnp.asarray(w)


def _block_expand(nb, s):
    return jnp.asarray(np.arange(nb)[:, None] == (np.arange(s)[None, :] // SLC_LEN), BF16)


def _gate_expand(n_in):
    hd = NSA_HEADS * HEAD_DIM
    w = np.zeros((n_in, 3 * hd), np.float32)
    for head in range(NSA_HEADS):
        for br in range(3):
            w[head * 3 + br, br * hd + head * HEAD_DIM: br * hd + (head + 1) * HEAD_DIM] = 1.0
    return jnp.asarray(w)


def kernel(x, c, ada_w, ada_b, norm_g, mlp_up, mlp_down, rw_mu, rw_r, rw_k, rw_v, rw_o, rw_w0, rw_wa, rw_wb,
           rw_a0, rw_aa, rw_ab, rw_ga, rw_gb, rw_kk, rw_ka, rw_rk, rw_lng, rw_lnb, kv_norm_g, kv_ada_w,
           kv_ada_b, kv_w, cmp_pos_k, cmp_pos_v, cmp_k_w1, cmp_k_w2, cmp_v_w1, cmp_v_w2, nsa_wqg, nsa_wo):
    B, S, D = x.shape
    depth = ada_w.shape[0]
    n_a = rw_mu.shape[0]
    bf = lambda a: a.astype(BF16)
    row = lambda a: a.reshape(1, -1)
    ts = min(512, S)

    c8 = jnp.pad(c, ((0, 8 - B), (0, 0)))
    mods = _ada(c8, ada_w, ada_b)[:, :B].reshape(depth, B, 6, D)
    mkv = _ada(c8, kv_ada_w[None], kv_ada_b[None])[0, :B].reshape(B, 2, D)

    GD = NSA_GROUPS * HEAD_DIM
    HD = NSA_HEADS * HEAD_DIM
    shared = None
    for layer in range(depth):
        mod = mods[layer]
        ng = norm_g[layer]
        if layer < n_a:
            i = layer
            r, k, v, lw, kk, a, g = _rw_front(
                x, mod, row(ng[0]), rw_mu[i], bf(rw_r[i]), bf(rw_k[i]), bf(rw_v[i]), bf(rw_wa[i]),
                bf(rw_wb[i]), row(rw_w0[i]), bf(rw_aa[i]), bf(rw_ab[i]), row(rw_a0[i]), bf(rw_ga[i]),
                bf(rw_gb[i]), row(rw_kk[i]), row(rw_ka[i]), min(256, S))
            y = _rw_recurrence(r, k, v, lw, kk, a, g, row(rw_rk[i]), row(rw_lng[i]), row(rw_lnb[i]),
                               min(512, S))
            x = _proj_res(y, bf(rw_o[i]), x, mod, row(ng[1]), ts)
        else:
            j = layer - n_a
            wqg = nsa_wqg[j]
            wq = bf(wqg[:, :HD])
            wg = bf(jnp.pad(wqg[:, HD:], ((0, 0), (0, LANES - 3 * NSA_HEADS))))
            if shared is None:
                wc = bf(kv_w[:, :2 * GD])
                wt = bf(kv_w[:, 2 * GD:].T)
            q, gates, kvc, kvt = _nsa_proj(x, mod, mkv, row(ng[0]), row(kv_norm_g), wq, wg, wc, wt, ts)
            if shared is None:
                fk, fv = _cmp1(kvc, cmp_pos_k, cmp_pos_v, bf(cmp_k_w1), bf(cmp_v_w1), min(1024, S))
                kct, vc = _cmp2(fk, fv, bf(cmp_k_w2.T), bf(cmp_v_w2))
                shared = (kct, vc, kvt)
            kct, vc, kvt_s = shared
            nc, nb = S // CMP_STRIDE, S // SLC_LEN
            tq = min(128, S)
            oc, sel = _cmp_attn(q, kct, vc, _importance_matrix(nc, nb), tq)
            osl = _slc_attn(q, sel, kvt_s, _block_expand(nb, S), tq, min(512, S))
            ow = _win_attn(q, kvt_s, min(256, S))
            x = _nsa_out(oc, osl, ow, gates, _gate_expand(gates.shape[2]), bf(nsa_wo[j]), x, mod,
                         row(ng[1]), ts)
        x = _mlp(x, mod, row(ng[2]), row(ng[3]), bf(mlp_up[layer]), bf(mlp_down[layer]), min(1024, S),
                 1024)
    return x
```

```python
import functools
import math

import numpy as np
import jax
import jax.numpy as jnp
from jax import lax
from jax.experimental import pallas as pl
from jax.experimental.pallas import tpu as pltpu

F32 = jnp.float32
BF16 = jnp.bfloat16
HI = lax.Precision.HIGHEST

NORM_EPS = 1e-6
GN_EPS = 64e-5
RWKV_HEAD = 64
HEAD_DIM = 64
NSA_GROUPS = 4
NSA_HPG = 4
NSA_HEADS = 16
CMP_LEN = 32
CMP_STRIDE = 16
CMP_RATIO = CMP_LEN // CMP_STRIDE
SLC_LEN = 64
SLC_RATIO = SLC_LEN // CMP_STRIDE
TOP_N = 16
WINDOW = 512
FORCED_SCORE = 1e4
NEG_INF = -1e30
LOG2E = math.log2(math.e)

LANES = 128
VMEM_LIMIT = 56 * 1024 * 1024
REC_CHUNK = 64
SOFTMAX_ROWS = 32


def _cp(*sem):
    return pltpu.CompilerParams(dimension_semantics=sem, vmem_limit_bytes=VMEM_LIMIT)


def _dot(a, b, **kw):
    return jnp.dot(a, b, preferred_element_type=F32, **kw)


def _dot_nt(a, b, **kw):
    return lax.dot_general(a, b, (((1,), (1,)), ((), ())), preferred_element_type=F32, **kw)


def _dot_tn(a, b, **kw):
    return lax.dot_general(a, b, (((0,), (0,)), ((), ())), preferred_element_type=F32, **kw)


def _bf(t):
    return t.astype(BF16)


def _split_bf16(x, parts):
    out = []
    for _ in range(parts - 1):
        hi = _bf(x)
        out.append(hi)
        x = x - hi.astype(F32)
    out.append(_bf(x))
    return out


def _rms(x, g):
    return x * lax.rsqrt(jnp.mean(x * x, axis=-1, keepdims=True) + NORM_EPS) * g


def _ada_body(c_ref, w_ref, b_ref, o_ref):
    c = c_ref[...]
    o_ref[...] = _dot(c * jax.nn.sigmoid(c), w_ref[...], precision=HI) + b_ref[...]


def _ada(c8, w, b):
    L, D, N = w.shape
    tn = 1024
    return pl.pallas_call(
        _ada_body, grid=(L, N // tn),
        in_specs=[pl.BlockSpec((8, D), lambda l, j: (0, 0)),
                  pl.BlockSpec((None, D, tn), lambda l, j: (l, 0, j)),
                  pl.BlockSpec((None, 1, tn), lambda l, j: (l, 0, j))],
        out_specs=pl.BlockSpec((None, 8, tn), lambda l, j: (l, 0, j)),
        out_shape=jax.ShapeDtypeStruct((L, 8, N), F32),
        compiler_params=_cp("parallel", "parallel"), name="ada",
    )(c8, w, b.reshape(L, 1, N))


def _rw_front_body(x_ref, xp_ref, mod_ref, g_ref, mu_ref, wr, wk, wv, wa, wb, w0, aa, ab, a0, ga, gb,
                   kk_ref, ka_ref, r_o, k_o, v_o, lw_o, kk_o, a_o, g_o):
    i = pl.program_id(1)
    g = g_ref[...]
    sh, sc = mod_ref[0:1, :], mod_ref[1:2, :]
    h = _rms(x_ref[...], g) * (1 + sc) + sh
    hp = _rms(xp_ref[...], g) * (1 + sc) + sh
    hp = jnp.where(i > 0, hp[7:8, :], 0.0)
    row = lax.broadcasted_iota(jnp.int32, h.shape, 0)
    hs = jnp.where(row == 0, hp, pltpu.roll(h, 1, axis=0))
    xx = hs - h
    mu = mu_ref[...]
    xr, xw, xk, xv, xa, xg = [_bf(h + xx * mu[j:j + 1, :]) for j in range(6)]
    r = _dot(xr, wr[...])
    k = _dot(xk, wk[...])
    v = _dot(xv, wv[...])
    wl = w0[...] + _dot(_bf(jnp.tanh(_dot(xw, wa[...]))), wb[...])
    lw = -jax.nn.sigmoid(wl) * math.exp(-0.5)
    a = jax.nn.sigmoid(a0[...] + _dot(_bf(_dot(xa, aa[...])), ab[...]))
    gate = _dot(_bf(jax.nn.sigmoid(_dot(xg, ga[...]))), gb[...])
    r_o[...] = r
    k_o[...] = k * (1 + (a - 1) * ka_ref[...])
    v_o[...] = v
    lw_o[...] = lw
    kk_o[...] = k * kk_ref[...]
    a_o[...] = a
    g_o[...] = gate


def _rw_front(x, mod, g, mu, wr, wk, wv, wa, wb, w0, aa, ab, a0, ga, gb, kk, ka, ts):
    B, S, D = x.shape
    full = lambda a: pl.BlockSpec(a.shape, lambda b, i: (0,) * a.ndim)
    tok = pl.BlockSpec((None, ts, D), lambda b, i: (b, i, 0))
    ws = [wr, wk, wv, wa, wb, w0, aa, ab, a0, ga, gb, kk, ka]
    return pl.pallas_call(
        _rw_front_body, grid=(B, S // ts),
        in_specs=[tok,
                  pl.BlockSpec((None, 8, D), lambda b, i: (b, jnp.maximum(i * (ts // 8) - 1, 0), 0)),
                  pl.BlockSpec((None, 6, D), lambda b, i: (b, 0, 0)),
                  full(g), full(mu)] + [full(w) for w in ws],
        out_specs=[tok] * 7,
        out_shape=[jax.ShapeDtypeStruct((B, S, D), F32)] * 7,
        compiler_params=_cp("parallel", "parallel"), name="rwkv_front",
    )(x, x, mod, g, mu, *ws)


def _rec_body(r_ref, k_ref, v_ref, lw_ref, kk_ref, a_ref, g_ref, rk_ref, lng_ref, lnb_ref, o_ref, st_ref,
              *, n_chunks):
    C = REC_CHUNK
    L = LANES
    C2 = 2 * C

    @pl.when(pl.program_id(2) == 0)
    def _():
        st_ref[...] = jnp.zeros_like(st_ref)

    h0 = lax.broadcasted_iota(jnp.int32, (C, L), 1) < RWKV_HEAD
    rr = lax.broadcasted_iota(jnp.int32, (C2, C2), 0)
    cc = lax.broadcasted_iota(jnp.int32, (C2, C2), 1)
    same = rr // C == cc // C
    incl = same & (rr >= cc)
    strict = same & (rr > cc)
    eye = (rr == cc).astype(F32)
    lvl_masks = [(rr // sz == cc // sz) & (rr // (sz // 2) != cc // (sz // 2))
                 for sz in (2 ** e for e in range(1, int(math.log2(C)) + 1))]
    tri_b = (lax.broadcasted_iota(jnp.int32, (C, C), 0) >= lax.broadcasted_iota(jnp.int32, (C, C), 1)).astype(BF16)
    bd_b = (lax.broadcasted_iota(jnp.int32, (L, L), 0) // RWKV_HEAD
            == lax.broadcasted_iota(jnp.int32, (L, L), 1) // RWKV_HEAD).astype(BF16)
    rk, lng, lnb = rk_ref[...], lng_ref[...], lnb_ref[...]
    chunks = range(n_chunks)

    def head_sum(t):
        return sum(_dot(p, bd_b) for p in _split_bf16(t, 2))

    def stack(t):
        zero = jnp.zeros_like(t)
        return jnp.concatenate([jnp.where(h0, t, zero), jnp.where(h0, zero, t)], axis=0)

    ld = [dict(r=r_ref[pl.ds(c * C, C), :], k=k_ref[pl.ds(c * C, C), :], v=v_ref[pl.ds(c * C, C), :],
               lw=lw_ref[pl.ds(c * C, C), :], kk=kk_ref[pl.ds(c * C, C), :], a=a_ref[pl.ds(c * C, C), :])
          for c in chunks]
    kkn = [d["kk"] / jnp.maximum(jnp.sqrt(head_sum(d["kk"] * d["kk"])), 1e-12) for d in ld]
    cl = [sum(_dot(tri_b, p) for p in _split_bf16(d["lw"], 3)) for d in ld]
    cl_end = [x[C - 1:C, :] for x in cl]
    e_neg = [jnp.exp(-x) for x in cl]
    e_end = [jnp.exp(ce - x) for x, ce in zip(cl, cl_end)]
    rt2 = [_bf(stack(d["r"] * jnp.exp(x))) for d, x in zip(ld, cl)]
    at2 = [_bf(stack(-kn * jnp.exp(x - d["lw"]))) for d, kn, x in zip(ld, kkn, cl)]
    bt2 = [_bf(stack(kn * d["a"] * e)) for d, kn, e in zip(ld, kkn, e_neg)]
    kt2 = [_bf(stack(d["k"] * e)) for d, e in zip(ld, e_neg)]
    bend2 = [_bf(stack(kn * d["a"] * e)) for d, kn, e in zip(ld, kkn, e_end)]
    kend2 = [_bf(stack(d["k"] * e)) for d, e in zip(ld, e_end)]
    v2 = [_bf(stack(d["v"])) for d in ld]
    decay = [jnp.exp(x) for x in cl_end]

    gm = [_dot_nt(jnp.concatenate([a_, r_], axis=0), jnp.concatenate([b_, k_], axis=0))
          for a_, r_, b_, k_ in zip(at2, rt2, bt2, kt2)]
    a_ab = [jnp.where(strict, g_[:C2, :C2], 0.0) for g_ in gm]
    a_ak = [_bf(jnp.where(strict, g_[:C2, C2:], 0.0)) for g_ in gm]
    a_rb = [_bf(jnp.where(incl, g_[C2:, :C2], 0.0)) for g_ in gm]
    a_rk = [_bf(jnp.where(incl, g_[C2:, C2:], 0.0)) for g_ in gm]

    tm = [eye + jnp.where(lvl_masks[0], a_, 0.0) for a_ in a_ab]
    for lm in lvl_masks[1:]:
        off = [_bf(jnp.where(lm, a_, 0.0)) for a_ in a_ab]
        half = [_bf(_dot(_bf(t_), o_)) for t_, o_ in zip(tm, off)]
        tm = [t_ + _dot(h_, _bf(t_)) for t_, h_ in zip(tm, half)]
    tm = [_bf(t_) for t_ in tm]

    w2 = [_bf(_dot(t_, a_)) for t_, a_ in zip(tm, at2)]
    akv = [_bf(_dot(a_, v_)) for a_, v_ in zip(a_ak, v2)]
    uh2 = [_bf(_dot(t_, x_)) for t_, x_ in zip(tm, akv)]
    rw2 = [_bf(r_.astype(F32) + _dot(a_, w_)) for r_, a_, w_ in zip(rt2, a_rb, w2)]
    yh2 = [_dot(a_, u_) + _dot(b_, v_) for a_, u_, b_, v_ in zip(a_rb, uh2, a_rk, v2)]
    ec = [_bf(_dot_tn(w_, b_)) for w_, b_ in zip(w2, bend2)]
    qc = [_dot_tn(jnp.concatenate([u_, v_], axis=0), jnp.concatenate([b_, k_], axis=0))
          for u_, v_, b_, k_ in zip(uh2, v2, bend2, kend2)]

    n = st_ref[...]
    y2 = []
    for c in chunks:
        n_b = _bf(n)
        y2.append(_dot_nt(rw2[c], n_b) + yh2[c])
        n = n * decay[c] + _dot(n_b, ec[c]) + qc[c]
    st_ref[...] = n

    for c in chunks:
        d = ld[c]
        y = y2[c][:C] + y2[c][C:]
        mean = head_sum(y) * (1.0 / RWKV_HEAD)
        yc = y - mean
        var = head_sum(yc * yc) * (1.0 / RWKV_HEAD)
        yn = yc * lax.rsqrt(var + GN_EPS) * lng + lnb
        bonus = head_sum(d["r"] * d["k"] * rk) * d["v"]
        o_ref[pl.ds(c * C, C), :] = (yn + bonus) * g_ref[pl.ds(c * C, C), :]


def _rw_recurrence(r, k, v, lw, kk, a, g, rk, lng, lnb, tt):
    B, S, D = r.shape
    tok = pl.BlockSpec((None, tt, LANES), lambda b, hp, t: (b, t, hp))
    vec = pl.BlockSpec((1, LANES), lambda b, hp, t: (0, hp))
    return pl.pallas_call(
        functools.partial(_rec_body, n_chunks=tt // REC_CHUNK),
        grid=(B, D // LANES, S // tt),
        in_specs=[tok] * 7 + [vec] * 3,
        out_specs=tok,
        out_shape=jax.ShapeDtypeStruct((B, S, D), F32),
        scratch_shapes=[pltpu.VMEM((LANES, LANES), F32)],
        compiler_params=_cp("parallel", "parallel", "arbitrary"), name="rwkv_recurrence",
    )(r, k, v, lw, kk, a, g, rk, lng, lnb)


def _proj_res_body(y_ref, w_ref, x_ref, mod_ref, g_ref, o_ref):
    y = _dot(_bf(y_ref[...]), w_ref[...])
    o_ref[...] = x_ref[...] + mod_ref[2:3, :] * _rms(y, g_ref[...])


def _proj_res(y, w, x, mod, g, ts):
    B, S, D = x.shape
    tok = pl.BlockSpec((None, ts, D), lambda b, i: (b, i, 0))
    return pl.pallas_call(
        _proj_res_body, grid=(B, S // ts),
        in_specs=[tok, pl.BlockSpec(w.shape, lambda b, i: (0, 0)), tok,
                  pl.BlockSpec((None, 6, D), lambda b, i: (b, 0, 0)),
                  pl.BlockSpec(g.shape, lambda b, i: (0, 0))],
        out_specs=tok, out_shape=jax.ShapeDtypeStruct((B, S, D), F32),
        compiler_params=_cp("parallel", "parallel"), name="proj_residual",
    )(y, w, x, mod, g)


def _mlp_body(x_ref, mod_ref, g2_ref, g3_ref, up_ref, dn_ref, o_ref, h_ref, acc_ref):
    f = pl.program_id(2)

    @pl.when(f == 0)
    def _():
        h = _rms(x_ref[...], g2_ref[...]) * (1 + mod_ref[4:5, :]) + mod_ref[3:4, :]
        h_ref[...] = _bf(h)
        acc_ref[...] = jnp.zeros_like(acc_ref)

    z = jnp.maximum(_dot(h_ref[...], up_ref[...]), 0.0)
    acc_ref[...] += _dot(_bf(z * z), dn_ref[...])

    @pl.when(f == pl.num_programs(2) - 1)
    def _():
        o_ref[...] = x_ref[...] + mod_ref[5:6, :] * _rms(acc_ref[...], g3_ref[...])


def _mlp(x, mod, g2, g3, up, dn, ts, tf):
    B, S, D = x.shape
    F = up.shape[1]
    tok = pl.BlockSpec((None, ts, D), lambda b, i, f: (b, i, 0))
    vec = pl.BlockSpec((1, D), lambda b, i, f: (0, 0))
    return pl.pallas_call(
        _mlp_body, grid=(B, S // ts, F // tf),
        in_specs=[tok, pl.BlockSpec((None, 6, D), lambda b, i, f: (b, 0, 0)), vec, vec,
                  pl.BlockSpec((D, tf), lambda b, i, f: (0, f)),
                  pl.BlockSpec((tf, D), lambda b, i, f: (f, 0))],
        out_specs=tok, out_shape=jax.ShapeDtypeStruct((B, S, D), F32),
        scratch_shapes=[pltpu.VMEM((ts, D), BF16), pltpu.VMEM((ts, D), F32)],
        compiler_params=_cp("parallel", "parallel", "arbitrary"), name="mlp",
    )(x, mod, g2, g3, up, dn)


def _nsa_proj_body(x_ref, mod_ref, mkv_ref, g_ref, gkv_ref, wq_ref, wg_ref, wc_ref, wt_ref,
                   q_o, gt_o, c_o, t_o):
    x = x_ref[...]
    xn = x * lax.rsqrt(jnp.mean(x * x, axis=-1, keepdims=True) + NORM_EPS)
    h = _bf((xn * g_ref[...]) * (1 + mod_ref[1:2, :]) + mod_ref[0:1, :])
    hkv = _bf((xn * gkv_ref[...]) * (1 + mkv_ref[1:2, :]) + mkv_ref[0:1, :])
    q_o[...] = _bf(_dot(h, wq_ref[...]) * (HEAD_DIM ** -0.5 * LOG2E))
    gt_o[...] = jax.nn.sigmoid(_dot(h, wg_ref[...]))
    c_o[...] = _dot(hkv, wc_ref[...])
    t_o[...] = _bf(_dot_nt(wt_ref[...], hkv))


def _nsa_proj(x, mod, mkv, g, gkv, wq, wg, wc, wt, ts):
    B, S, D = x.shape
    tok = lambda n: pl.BlockSpec((None, ts, n), lambda b, i: (b, i, 0))
    full = lambda a: pl.BlockSpec(a.shape, lambda b, i: (0,) * a.ndim)
    return pl.pallas_call(
        _nsa_proj_body, grid=(B, S // ts),
        in_specs=[tok(D), pl.BlockSpec((None, 6, D), lambda b, i: (b, 0, 0)),
                  pl.BlockSpec((None, 2, D), lambda b, i: (b, 0, 0)),
                  full(g), full(gkv), full(wq), full(wg), full(wc), full(wt)],
        out_specs=[tok(wq.shape[1]), tok(wg.shape[1]), tok(wc.shape[1]),
                   pl.BlockSpec((None, wt.shape[0], ts), lambda b, i: (b, 0, i))],
        out_shape=[jax.ShapeDtypeStruct((B, S, wq.shape[1]), BF16),
                   jax.ShapeDtypeStruct((B, S, wg.shape[1]), F32),
                   jax.ShapeDtypeStruct((B, S, wc.shape[1]), F32),
                   jax.ShapeDtypeStruct((B, wt.shape[0], S), BF16)],
        compiler_params=_cp("parallel", "parallel"), name="nsa_proj",
    )(x, mod, mkv, g, gkv, wq, wg, wc, wt)


def _cmp1_body(k0_ref, k1_ref, v0_ref, v1_ref, pk_ref, pv_ref, wk_ref, wv_ref, fk_o, fv_o, *, n_chunk):
    hid = wk_ref.shape[1]
    gpr = LANES // HEAD_DIM
    for srcs, pos, w1, out in (((k0_ref, k1_ref), pk_ref, wk_ref, fk_o), ((v0_ref, v1_ref), pv_ref, wv_ref, fv_o)):
        acc = [[jnp.zeros((n_chunk, hid), F32) for _ in range(CMP_RATIO)] for _ in range(NSA_GROUPS)]
        for l in range(CMP_STRIDE):
            for si, src in enumerate(srcs):
                rows = src[pl.ds(l, n_chunk, stride=CMP_STRIDE), :]
                for gi in range(gpr):
                    g = si * gpr + gi
                    t = rows[:, gi * HEAD_DIM:(gi + 1) * HEAD_DIM]
                    for half in range(CMP_RATIO):
                        p = half * CMP_STRIDE + l
                        lhs = _bf(t + pos[p:p + 1, :])
                        acc[g][half] = acc[g][half] + _dot(lhs, w1[p * HEAD_DIM:(p + 1) * HEAD_DIM, :])
        for g in range(NSA_GROUPS):
            for half in range(CMP_RATIO):
                out[g, :, half * hid:(half + 1) * hid] = acc[g][half]


def _cmp1(kvc, pos_k, pos_v, w1k, w1v, ts):
    B, S, _ = kvc.shape
    G = NSA_GROUPS
    hid = w1k.shape[1]
    nch = ts // CMP_STRIDE
    full = lambda a: pl.BlockSpec(a.shape, lambda b, i: (0,) * a.ndim)
    src = lambda j: pl.BlockSpec((None, ts, LANES), lambda b, i: (b, i, j))
    outs = pl.BlockSpec((None, G, nch, CMP_RATIO * hid), lambda b, i: (b, 0, i, 0))
    return pl.pallas_call(
        functools.partial(_cmp1_body, n_chunk=nch), grid=(B, S // ts),
        in_specs=[src(0), src(1), src(2), src(3), full(pos_k), full(pos_v), full(w1k), full(w1v)],
        out_specs=[outs, outs],
        out_shape=[jax.ShapeDtypeStruct((B, G, S // CMP_STRIDE, CMP_RATIO * hid), F32)] * 2,
        compiler_params=_cp("parallel", "parallel"), name="cmp_stage1",
    )(kvc, kvc, kvc, kvc, pos_k, pos_v, w1k, w1v)


def _gelu_tanh(x):
    return 0.5 * x * (1.0 + jnp.tanh(math.sqrt(2.0 / math.pi) * (x + 0.044715 * (x * x * x))))


def _cmp2_body(fk_ref, fv_ref, w2kt_ref, w2v_ref, kt_o, v_o):
    nc = fk_ref.shape[0]
    hid = w2v_ref.shape[0]
    row = lax.broadcasted_iota(jnp.int32, (nc, hid), 0)

    def hidden(f):
        nxt = jnp.where(row == nc - 1, 0.0, pltpu.roll(f[:, hid:2 * hid], nc - 1, axis=0))
        return _bf(_gelu_tanh(f[:, 0:hid] + nxt))

    kt_o[...] = _bf(_dot_nt(w2kt_ref[...], hidden(fk_ref[...])))
    v_o[...] = _bf(_dot(hidden(fv_ref[...]), w2v_ref[...]))


def _cmp2(fk, fv, w2kt, w2v):
    B, G, NC, H2 = fk.shape
    fin = pl.BlockSpec((None, None, NC, H2), lambda b, g: (b, g, 0, 0))
    return pl.pallas_call(
        _cmp2_body, grid=(B, G),
        in_specs=[fin, fin, pl.BlockSpec(w2kt.shape, lambda b, g: (0, 0)),
                  pl.BlockSpec(w2v.shape, lambda b, g: (0, 0))],
        out_specs=[pl.BlockSpec((None, None, HEAD_DIM, NC), lambda b, g: (b, g, 0, 0)),
                   pl.BlockSpec((None, None, NC, HEAD_DIM), lambda b, g: (b, g, 0, 0))],
        out_shape=[jax.ShapeDtypeStruct((B, G, HEAD_DIM, NC), BF16),
                   jax.ShapeDtypeStruct((B, G, NC, HEAD_DIM), BF16)],
        compiler_params=_cp("parallel", "parallel"), name="cmp_stage2",
    )(fk, fv, w2kt, w2v)


def _head_queries(q_ref):
    return [q_ref[:, h * HEAD_DIM:(h + 1) * HEAD_DIM] for h in range(NSA_HPG)]


def _softmax_attend(q_ref, kt, bias, pv, o_ref, tq, want_psum):
    rb = SOFTMAX_ROWS
    n_blk = tq // rb
    qs = _head_queries(q_ref)
    scores = {0: _dot(qs[0], kt)}
    tot = [None] * n_blk
    for h in range(NSA_HPG):
        if h + 1 < NSA_HPG:
            scores[h + 1] = _dot(qs[h + 1], kt)
        s_h = scores.pop(h)
        blocks = []
        for b in range(n_blk):
            s = s_h[b * rb:(b + 1) * rb, :] + bias[b * rb:(b + 1) * rb, :]
            m = jnp.maximum(jnp.max(s, axis=-1, keepdims=True), 0.1 * NEG_INF)
            e = jnp.exp2(s - m)
            p = e * (1.0 / jnp.maximum(jnp.sum(e, axis=-1, keepdims=True), 1e-30))
            blocks.append(_bf(p))
            if want_psum:
                tot[b] = p if tot[b] is None else tot[b] + p
        o_ref[:, h * HEAD_DIM:(h + 1) * HEAD_DIM] = pv(jnp.concatenate(blocks, axis=0))
    return tot


def _cmp_attn_body(q_ref, kt_ref, v_ref, wimp_ref, o_ref, sel_ref, *, tq):
    i = pl.program_id(2)
    nc = kt_ref.shape[1]
    nb = wimp_ref.shape[1]
    t = i * tq + lax.broadcasted_iota(jnp.int32, (tq, nc), 0)
    cmp_end = lax.broadcasted_iota(jnp.int32, (tq, nc), 1) * CMP_STRIDE + (CMP_LEN - 1)
    bias = jnp.where(cmp_end <= t, 0.0, NEG_INF)
    v = v_ref[...]
    tot = _softmax_attend(q_ref, kt_ref[...], bias, lambda p: _dot(p, v), o_ref, tq, True)

    wimp = wimp_ref[...]
    imp = sum(_dot(part, wimp) for part in _split_bf16(jnp.concatenate(tot, axis=0), 3)).T
    blk = lax.broadcasted_iota(jnp.int32, (nb, tq), 0)
    cur = (i * tq + lax.broadcasted_iota(jnp.int32, (nb, tq), 1)) // SLC_LEN
    forced = (blk == 0) | (blk == cur) | (blk == cur - 1)
    imp = jnp.where(forced, FORCED_SCORE, imp)
    imp = jnp.where(blk > cur, -1.0, imp)
    blk_f = blk.astype(F32)

    def pick(_, carry):
        imp, sel = carry
        m = jnp.max(imp, axis=0, keepdims=True)
        first = jnp.min(jnp.where(imp == m, blk_f, float(nb)), axis=0, keepdims=True)
        hit = blk_f == first
        return jnp.where(hit, -3e38, imp), jnp.where(hit, 1.0, sel)

    _, sel = lax.fori_loop(0, min(TOP_N, nb), pick, (imp, jnp.zeros((nb, tq), F32)))
    sel_ref[...] = _bf(sel.T)


def _cmp_attn(q, kct, vc, wimp, tq):
    B, S, _ = q.shape
    G = NSA_GROUPS
    NC = kct.shape[3]
    NB = wimp.shape[1]
    GD = NSA_HPG * HEAD_DIM
    return pl.pallas_call(
        functools.partial(_cmp_attn_body, tq=tq), grid=(B, G, S // tq),
        in_specs=[pl.BlockSpec((None, tq, GD), lambda b, g, i: (b, i, g)),
                  pl.BlockSpec((None, None, HEAD_DIM, NC), lambda b, g, i: (b, g, 0, 0)),
                  pl.BlockSpec((None, None, NC, HEAD_DIM), lambda b, g, i: (b, g, 0, 0)),
                  pl.BlockSpec(wimp.shape, lambda b, g, i: (0, 0))],
        out_specs=[pl.BlockSpec((None, tq, GD), lambda b, g, i: (b, i, g)),
                   pl.BlockSpec((None, None, tq, NB), lambda b, g, i: (b, g, i, 0))],
        out_shape=[jax.ShapeDtypeStruct((B, S, G * GD), F32),
                   jax.ShapeDtypeStruct((B, G, S, NB), BF16)],
        compiler_params=_cp("parallel", "parallel", "parallel"), name="cmp_attention_topk",
    )(q, kct, vc, wimp)


def _slc_attn_body(q_ref, sel_ref, kt_ref, vt_ref, e_ref, o_ref, m_ref, acc_ref, *, tq, tk):
    i = pl.program_id(2)
    rb = SOFTMAX_ROWS
    qs = _head_queries(q_ref)
    sel = sel_ref[...]
    m_ref[...] = jnp.full(m_ref.shape, NEG_INF, F32)
    acc_ref[...] = jnp.zeros_like(acc_ref)
    t = i * tq + lax.broadcasted_iota(jnp.int32, (tq, tk), 0)
    kofs = lax.broadcasted_iota(jnp.int32, (tq, tk), 1)
    ones = jnp.ones((LANES - HEAD_DIM, tk), BF16)
    tile_keys = lambda j: pl.ds(pl.multiple_of(j * tk, tk), tk)

    def tile(j, s_cur, causal):
        ks = tile_keys(j)
        vt = jnp.concatenate([vt_ref[:, ks], ones], axis=0)
        ok = _dot(sel, e_ref[:, ks]) > 0.5
        if causal:
            ok = ok & (j * tk + kofs <= t)
            s_next = None
        else:
            kt_next = kt_ref[:, tile_keys(j + 1)]
            s_next = [_dot(qs[0], kt_next), _dot(qs[1], kt_next)]
        bias = jnp.where(ok, 0.0, NEG_INF)
        for h in range(NSA_HPG):
            blocks = []
            for b in range(tq // rb):
                rs = pl.ds(h * tq + b * rb, rb)
                s = s_cur[h][b * rb:(b + 1) * rb, :] + bias[b * rb:(b + 1) * rb, :]
                m_old = m_ref[rs, :]
                m_new = jnp.maximum(m_old, jnp.max(s, axis=-1, keepdims=True))
                ps = [jnp.exp2(s[:, c * LANES:(c + 1) * LANES] - m_new) for c in range(tk // LANES)]
                m_ref[rs, :] = m_new
                acc_ref[rs, :] = acc_ref[rs, :] * jnp.exp2(m_old - m_new)
                blocks.append(_bf(jnp.concatenate(ps, axis=1)))
            hs = pl.ds(h * tq, tq)
            acc_ref[hs, :] += _dot_nt(jnp.concatenate(blocks, axis=0), vt)
            if s_next is not None and h + 2 < NSA_HPG:
                s_next.append(_dot(qs[h + 2], kt_next))
        return s_next

    last = (i * tq + tq - 1) // tk
    kt0 = kt_ref[:, tile_keys(0)]
    s_first = tuple(_dot(q, kt0) for q in qs)
    s_last = lax.fori_loop(0, last, lambda j, s_cur: tuple(tile(j, s_cur, False)), s_first)
    tile(last, s_last, True)
    for h in range(NSA_HPG):
        acc = acc_ref[pl.ds(h * tq, tq), :]
        o_ref[:, h * HEAD_DIM:(h + 1) * HEAD_DIM] = acc[:, :HEAD_DIM] / jnp.maximum(acc[:, HEAD_DIM:], 1e-30)


def _slc_attn(q, sel, kvt, e, tq, tk):
    B, S, _ = q.shape
    G = NSA_GROUPS
    NB = sel.shape[3]
    GD = NSA_HPG * HEAD_DIM
    rows = NSA_HPG * tq
    return pl.pallas_call(
        functools.partial(_slc_attn_body, tq=tq, tk=tk), grid=(B, G, S // tq),
        in_specs=[pl.BlockSpec((None, tq, GD), lambda b, g, i: (b, i, g)),
                  pl.BlockSpec((None, None, tq, NB), lambda b, g, i: (b, g, i, 0)),
                  pl.BlockSpec((None, HEAD_DIM, S), lambda b, g, i: (b, g, 0)),
                  pl.BlockSpec((None, HEAD_DIM, S), lambda b, g, i: (b, G + g, 0)),
                  pl.BlockSpec(e.shape, lambda b, g, i: (0, 0))],
        out_specs=pl.BlockSpec((None, tq, GD), lambda b, g, i: (b, i, g)),
        out_shape=jax.ShapeDtypeStruct((B, S, G * GD), F32),
        scratch_shapes=[pltpu.VMEM((rows, LANES), F32), pltpu.VMEM((rows, LANES), F32)],
        compiler_params=_cp("parallel", "parallel", "parallel"), name="slc_attention",
    )(q, sel, kvt, kvt, e)


def _win_attn_body(q_ref, kt_ref, vt_ref, o_ref, *, tq, span):
    i = pl.program_id(2)
    start = pl.multiple_of(jnp.maximum(i * tq - WINDOW, 0), LANES)
    ks = pl.ds(start, span)
    t = i * tq + lax.broadcasted_iota(jnp.int32, (tq, span), 0)
    dist = t - (start + lax.broadcasted_iota(jnp.int32, (tq, span), 1))
    bias = jnp.where((dist >= 0) & (dist < WINDOW), 0.0, NEG_INF)
    vt = vt_ref[:, ks]
    _softmax_attend(q_ref, kt_ref[:, ks], bias, lambda p: _dot_nt(p, vt), o_ref, tq, False)


def _win_attn(q, kvt, tq):
    B, S, _ = q.shape
    G = NSA_GROUPS
    GD = NSA_HPG * HEAD_DIM
    span = min(tq + WINDOW, S)
    return pl.pallas_call(
        functools.partial(_win_attn_body, tq=tq, span=span), grid=(B, G, S // tq),
        in_specs=[pl.BlockSpec((None, tq, GD), lambda b, g, i: (b, i, g)),
                  pl.BlockSpec((None, HEAD_DIM, S), lambda b, g, i: (b, 2 * G + g, 0)),
                  pl.BlockSpec((None, HEAD_DIM, S), lambda b, g, i: (b, 3 * G + g, 0))],
        out_specs=pl.BlockSpec((None, tq, GD), lambda b, g, i: (b, i, g)),
        out_shape=jax.ShapeDtypeStruct((B, S, G * GD), F32),
        compiler_params=_cp("parallel", "parallel", "parallel"), name="win_attention",
    )(q, kvt, kvt)


def _nsa_out_body(oc_ref, os_ref, ow_ref, gt_ref, ge_ref, w_ref, x_ref, mod_ref, g_ref, o_ref):
    D = oc_ref.shape[1]
    gx = _dot(gt_ref[...], ge_ref[...], precision=HI)
    o = gx[:, 0:D] * oc_ref[...] + gx[:, D:2 * D] * os_ref[...] + gx[:, 2 * D:3 * D] * ow_ref[...]
    y = _dot(_bf(o), w_ref[...])
    o_ref[...] = x_ref[...] + mod_ref[2:3, :] * _rms(y, g_ref[...])


def _nsa_out(oc, osl, ow, gates, gexp, w, x, mod, g, ts):
    B, S, D = x.shape
    tok = lambda n: pl.BlockSpec((None, ts, n), lambda b, i: (b, i, 0))
    full = lambda a: pl.BlockSpec(a.shape, lambda b, i: (0,) * a.ndim)
    return pl.pallas_call(
        _nsa_out_body, grid=(B, S // ts),
        in_specs=[tok(D), tok(D), tok(D), tok(gates.shape[2]), full(gexp), full(w), tok(D),
                  pl.BlockSpec((None, 6, D), lambda b, i: (b, 0, 0)), full(g)],
        out_specs=tok(D), out_shape=jax.ShapeDtypeStruct((B, S, D), F32),
        compiler_params=_cp("parallel", "parallel"), name="nsa_out",
    )(oc, osl, ow, gates, gexp, w, x, mod, g)


def _importance_matrix(nc, nb):
    n_cmp = nc - CMP_RATIO + 1
    w = np.zeros((nc, nb), np.float32)
    for m in range(SLC_RATIO):
        for n in range(CMP_RATIO):
            off = m - n + CMP_RATIO - 1
            for j in range(nb):
                src = SLC_RATIO * j + off - (CMP_RATIO - 1)
                if 0 <= src < n_cmp:
                    w[src, j] += 1.0
    return jnp.asarray(w, BF16)


def _block_expand(nb, s):
    return jnp.asarray(np.arange(nb)[:, None] == (np.arange(s)[None, :] // SLC_LEN), BF16)


def _gate_expand(n_in):
    hd = NSA_HEADS * HEAD_DIM
    w = np.zeros((n_in, 3 * hd), np.float32)
    for head in range(NSA_HEADS):
        for br in range(3):
            w[head * 3 + br, br * hd + head * HEAD_DIM: br * hd + (head + 1) * HEAD_DIM] = 1.0
    return jnp.asarray(w)


def kernel(x, c, ada_w, ada_b, norm_g, mlp_up, mlp_down, rw_mu, rw_r, rw_k, rw_v, rw_o, rw_w0, rw_wa, rw_wb,
           rw_a0, rw_aa, rw_ab, rw_ga, rw_gb, rw_kk, rw_ka, rw_rk, rw_lng, rw_lnb, kv_norm_g, kv_ada_w,
           kv_ada_b, kv_w, cmp_pos_k, cmp_pos_v, cmp_k_w1, cmp_k_w2, cmp_v_w1, cmp_v_w2, nsa_wqg, nsa_wo):
    B, S, D = x.shape
    depth = ada_w.shape[0]
    n_a = rw_mu.shape[0]
    bf = _bf
    row = lambda a: a.reshape(1, -1)
    ts = min(512, S)

    c8 = jnp.pad(c, ((0, 8 - B), (0, 0)))
    mods = _ada(c8, ada_w, ada_b)[:, :B].reshape(depth, B, 6, D)
    mkv = _ada(c8, kv_ada_w[None], kv_ada_b[None])[0, :B].reshape(B, 2, D)

    GD = NSA_GROUPS * HEAD_DIM
    HD = NSA_HEADS * HEAD_DIM
    shared = None
    for layer in range(depth):
        mod = mods[layer]
        ng = norm_g[layer]
        if layer < n_a:
            i = layer
            r, k, v, lw, kk, a, g = _rw_front(
                x, mod, row(ng[0]), rw_mu[i], bf(rw_r[i]), bf(rw_k[i]), bf(rw_v[i]), bf(rw_wa[i]),
                bf(rw_wb[i]), row(rw_w0[i]), bf(rw_aa[i]), bf(rw_ab[i]), row(rw_a0[i]), bf(rw_ga[i]),
                bf(rw_gb[i]), row(rw_kk[i]), row(rw_ka[i]), min(256, S))
            y = _rw_recurrence(r, k, v, lw, kk, a, g, row(rw_rk[i]), row(rw_lng[i]), row(rw_lnb[i]),
                               min(512, S))
            x = _proj_res(y, bf(rw_o[i]), x, mod, row(ng[1]), ts)
        else:
            j = layer - n_a
            wqg = nsa_wqg[j]
            wq = bf(wqg[:, :HD])
            wg = bf(jnp.pad(wqg[:, HD:], ((0, 0), (0, LANES - 3 * NSA_HEADS))))
            if shared is None:
                wc = bf(kv_w[:, :2 * GD])
                wt = bf(kv_w[:, 2 * GD:].T)
            q, gates, kvc, kvt = _nsa_proj(x, mod, mkv, row(ng[0]), row(kv_norm_g), wq, wg, wc, wt, ts)
            if shared is None:
                fk, fv = _cmp1(kvc, cmp_pos_k, cmp_pos_v, bf(cmp_k_w1), bf(cmp_v_w1), min(1024, S))
                kct, vc = _cmp2(fk, fv, bf(cmp_k_w2.T), bf(cmp_v_w2))
                shared = (kct, vc, kvt)
            kct, vc, kvt_s = shared
            nc, nb = S // CMP_STRIDE, S // SLC_LEN
            tq = min(128, S)
            oc, sel = _cmp_attn(q, kct, vc, _importance_matrix(nc, nb), tq)
            osl = _slc_attn(q, sel, kvt_s, _block_expand(nb, S), min(256, S), min(512, S))
            ow = _win_attn(q, kvt_s, min(256, S))
            x = _nsa_out(oc, osl, ow, gates, _gate_expand(gates.shape[2]), bf(nsa_wo[j]), x, mod,
                         row(ng[1]), ts)
        x = _mlp(x, mod, row(ng[2]), row(ng[3]), bf(mlp_up[layer]), bf(mlp_down[layer]), min(1024, S),
                 1024)
    return x
```

```python
import functools
import math

import numpy as np
import jax
import jax.numpy as jnp
from jax import lax
from jax.experimental import pallas as pl
from jax.experimental.pallas import tpu as pltpu

F32 = jnp.float32
BF16 = jnp.bfloat16
HI = lax.Precision.HIGHEST

NORM_EPS = 1e-6
GN_EPS = 64e-5
RWKV_HEAD = 64
HEAD_DIM = 64
NSA_GROUPS = 4
NSA_HPG = 4
NSA_HEADS = 16
CMP_LEN = 32
CMP_STRIDE = 16
CMP_RATIO = CMP_LEN // CMP_STRIDE
SLC_LEN = 64
SLC_RATIO = SLC_LEN // CMP_STRIDE
TOP_N = 16
WINDOW = 512
FORCED_SCORE = 1e4
NEG_INF = -1e30
LOG2E = math.log2(math.e)

LANES = 128
VMEM_LIMIT = 56 * 1024 * 1024
REC_CHUNK = 64
SOFTMAX_ROWS = 32
CMP_COLS = 256


def _cp(*sem):
    return pltpu.CompilerParams(dimension_semantics=sem, vmem_limit_bytes=VMEM_LIMIT)


def _dot(a, b, **kw):
    return jnp.dot(a, b, preferred_element_type=F32, **kw)


def _dot_nt(a, b, **kw):
    return lax.dot_general(a, b, (((1,), (1,)), ((), ())), preferred_element_type=F32, **kw)


def _dot_tn(a, b, **kw):
    return lax.dot_general(a, b, (((0,), (0,)), ((), ())), preferred_element_type=F32, **kw)


def _bf(t):
    return t.astype(BF16)


def _split_bf16(x, parts):
    out = []
    for _ in range(parts - 1):
        hi = _bf(x)
        out.append(hi)
        x = x - hi.astype(F32)
    out.append(_bf(x))
    return out


def _rms(x, g):
    return x * lax.rsqrt(jnp.mean(x * x, axis=-1, keepdims=True) + NORM_EPS) * g


def _ada_body(c_ref, w_ref, b_ref, o_ref):
    c = c_ref[...]
    o_ref[...] = _dot(c * jax.nn.sigmoid(c), w_ref[...], precision=HI) + b_ref[...]


def _ada(c8, w, b):
    L, D, N = w.shape
    tn = 1024
    return pl.pallas_call(
        _ada_body, grid=(L, N // tn),
        in_specs=[pl.BlockSpec((8, D), lambda l, j: (0, 0)),
                  pl.BlockSpec((None, D, tn), lambda l, j: (l, 0, j)),
                  pl.BlockSpec((None, 1, tn), lambda l, j: (l, 0, j))],
        out_specs=pl.BlockSpec((None, 8, tn), lambda l, j: (l, 0, j)),
        out_shape=jax.ShapeDtypeStruct((L, 8, N), F32),
        compiler_params=_cp("parallel", "parallel"), name="ada",
    )(c8, w, b.reshape(L, 1, N))


def _rw_front_body(x_ref, xp_ref, mod_ref, g_ref, mu_ref, wr, wk, wv, wa, wb, w0, aa, ab, a0, ga, gb,
                   kk_ref, ka_ref, r_o, k_o, v_o, lw_o, kk_o, a_o, g_o):
    i = pl.program_id(1)
    g = g_ref[...]
    sh, sc = mod_ref[0:1, :], mod_ref[1:2, :]
    h = _rms(x_ref[...], g) * (1 + sc) + sh
    hp = _rms(xp_ref[...], g) * (1 + sc) + sh
    hp = jnp.where(i > 0, hp[7:8, :], 0.0)
    row = lax.broadcasted_iota(jnp.int32, h.shape, 0)
    hs = jnp.where(row == 0, hp, pltpu.roll(h, 1, axis=0))
    xx = hs - h
    mu = mu_ref[...]
    xr, xw, xk, xv, xa, xg = [_bf(h + xx * mu[j:j + 1, :]) for j in range(6)]
    r = _dot(xr, wr[...])
    k = _dot(xk, wk[...])
    v = _dot(xv, wv[...])
    wl = w0[...] + _dot(_bf(jnp.tanh(_dot(xw, wa[...]))), wb[...])
    lw = -jax.nn.sigmoid(wl) * math.exp(-0.5)
    a = jax.nn.sigmoid(a0[...] + _dot(_bf(_dot(xa, aa[...])), ab[...]))
    gate = _dot(_bf(jax.nn.sigmoid(_dot(xg, ga[...]))), gb[...])
    r_o[...] = r
    k_o[...] = k * (1 + (a - 1) * ka_ref[...])
    v_o[...] = v
    lw_o[...] = lw
    kk_o[...] = k * kk_ref[...]
    a_o[...] = a
    g_o[...] = gate


def _rw_front(x, mod, g, mu, wr, wk, wv, wa, wb, w0, aa, ab, a0, ga, gb, kk, ka, ts):
    B, S, D = x.shape
    full = lambda a: pl.BlockSpec(a.shape, lambda b, i: (0,) * a.ndim)
    tok = pl.BlockSpec((None, ts, D), lambda b, i: (b, i, 0))
    ws = [wr, wk, wv, wa, wb, w0, aa, ab, a0, ga, gb, kk, ka]
    return pl.pallas_call(
        _rw_front_body, grid=(B, S // ts),
        in_specs=[tok,
                  pl.BlockSpec((None, 8, D), lambda b, i: (b, jnp.maximum(i * (ts // 8) - 1, 0), 0)),
                  pl.BlockSpec((None, 6, D), lambda b, i: (b, 0, 0)),
                  full(g), full(mu)] + [full(w) for w in ws],
        out_specs=[tok] * 7,
        out_shape=[jax.ShapeDtypeStruct((B, S, D), F32)] * 7,
        compiler_params=_cp("parallel", "parallel"), name="rwkv_front",
    )(x, x, mod, g, mu, *ws)


def _rec_body(r_ref, k_ref, v_ref, lw_ref, kk_ref, a_ref, g_ref, rk_ref, lng_ref, lnb_ref, o_ref, st_ref,
              *, n_chunks):
    C = REC_CHUNK
    L = LANES
    C2 = 2 * C

    @pl.when(pl.program_id(2) == 0)
    def _():
        st_ref[...] = jnp.zeros_like(st_ref)

    h0 = lax.broadcasted_iota(jnp.int32, (C, L), 1) < RWKV_HEAD
    rr = lax.broadcasted_iota(jnp.int32, (C2, C2), 0)
    cc = lax.broadcasted_iota(jnp.int32, (C2, C2), 1)
    same = rr // C == cc // C
    incl = same & (rr >= cc)
    strict = same & (rr > cc)
    eye = (rr == cc).astype(F32)
    lvl_masks = [(rr // sz == cc // sz) & (rr // (sz // 2) != cc // (sz // 2))
                 for sz in (2 ** e for e in range(1, int(math.log2(C)) + 1))]
    tri_b = (lax.broadcasted_iota(jnp.int32, (C, C), 0) >= lax.broadcasted_iota(jnp.int32, (C, C), 1)).astype(BF16)
    bd_b = (lax.broadcasted_iota(jnp.int32, (L, L), 0) // RWKV_HEAD
            == lax.broadcasted_iota(jnp.int32, (L, L), 1) // RWKV_HEAD).astype(BF16)
    rk, lng, lnb = rk_ref[...], lng_ref[...], lnb_ref[...]
    chunks = range(n_chunks)

    def head_sum(t):
        return sum(_dot(p, bd_b) for p in _split_bf16(t, 2))

    def stack(t):
        zero = jnp.zeros_like(t)
        return jnp.concatenate([jnp.where(h0, t, zero), jnp.where(h0, zero, t)], axis=0)

    ld = [dict(r=r_ref[pl.ds(c * C, C), :], k=k_ref[pl.ds(c * C, C), :], v=v_ref[pl.ds(c * C, C), :],
               lw=lw_ref[pl.ds(c * C, C), :], kk=kk_ref[pl.ds(c * C, C), :], a=a_ref[pl.ds(c * C, C), :])
          for c in chunks]
    kkn = [d["kk"] / jnp.maximum(jnp.sqrt(head_sum(d["kk"] * d["kk"])), 1e-12) for d in ld]
    cl = [sum(_dot(tri_b, p) for p in _split_bf16(d["lw"], 3)) for d in ld]
    cl_end = [x[C - 1:C, :] for x in cl]
    e_neg = [jnp.exp(-x) for x in cl]
    e_end = [jnp.exp(ce - x) for x, ce in zip(cl, cl_end)]
    rt2 = [_bf(stack(d["r"] * jnp.exp(x))) for d, x in zip(ld, cl)]
    at2 = [_bf(stack(-kn * jnp.exp(x - d["lw"]))) for d, kn, x in zip(ld, kkn, cl)]
    bt2 = [_bf(stack(kn * d["a"] * e)) for d, kn, e in zip(ld, kkn, e_neg)]
    kt2 = [_bf(stack(d["k"] * e)) for d, e in zip(ld, e_neg)]
    bend2 = [_bf(stack(kn * d["a"] * e)) for d, kn, e in zip(ld, kkn, e_end)]
    kend2 = [_bf(stack(d["k"] * e)) for d, e in zip(ld, e_end)]
    v2 = [_bf(stack(d["v"])) for d in ld]
    decay = [jnp.exp(x) for x in cl_end]

    gm = [_dot_nt(jnp.concatenate([a_, r_], axis=0), jnp.concatenate([b_, k_], axis=0))
          for a_, r_, b_, k_ in zip(at2, rt2, bt2, kt2)]
    a_ab = [jnp.where(strict, g_[:C2, :C2], 0.0) for g_ in gm]
    a_ak = [_bf(jnp.where(strict, g_[:C2, C2:], 0.0)) for g_ in gm]
    a_rb = [_bf(jnp.where(incl, g_[C2:, :C2], 0.0)) for g_ in gm]
    a_rk = [_bf(jnp.where(incl, g_[C2:, C2:], 0.0)) for g_ in gm]

    tm = [eye + jnp.where(lvl_masks[0], a_, 0.0) for a_ in a_ab]
    for lm in lvl_masks[1:]:
        off = [_bf(jnp.where(lm, a_, 0.0)) for a_ in a_ab]
        half = [_bf(_dot(_bf(t_), o_)) for t_, o_ in zip(tm, off)]
        tm = [t_ + _dot(h_, _bf(t_)) for t_, h_ in zip(tm, half)]
    tm = [_bf(t_) for t_ in tm]

    w2 = [_bf(_dot(t_, a_)) for t_, a_ in zip(tm, at2)]
    akv = [_bf(_dot(a_, v_)) for a_, v_ in zip(a_ak, v2)]
    uh2 = [_bf(_dot(t_, x_)) for t_, x_ in zip(tm, akv)]
    rw2 = [_bf(r_.astype(F32) + _dot(a_, w_)) for r_, a_, w_ in zip(rt2, a_rb, w2)]
    yh2 = [_dot(a_, u_) + _dot(b_, v_) for a_, u_, b_, v_ in zip(a_rb, uh2, a_rk, v2)]
    ec = [_bf(_dot_tn(w_, b_)) for w_, b_ in zip(w2, bend2)]
    qc = [_dot_tn(jnp.concatenate([u_, v_], axis=0), jnp.concatenate([b_, k_], axis=0))
          for u_, v_, b_, k_ in zip(uh2, v2, bend2, kend2)]

    n = st_ref[...]
    y2 = []
    for c in chunks:
        n_b = _bf(n)
        y2.append(_dot_nt(rw2[c], n_b) + yh2[c])
        n = n * decay[c] + _dot(n_b, ec[c]) + qc[c]
    st_ref[...] = n

    for c in chunks:
        d = ld[c]
        y = y2[c][:C] + y2[c][C:]
        mean = head_sum(y) * (1.0 / RWKV_HEAD)
        yc = y - mean
        var = head_sum(yc * yc) * (1.0 / RWKV_HEAD)
        yn = yc * lax.rsqrt(var + GN_EPS) * lng + lnb
        bonus = head_sum(d["r"] * d["k"] * rk) * d["v"]
        o_ref[pl.ds(c * C, C), :] = (yn + bonus) * g_ref[pl.ds(c * C, C), :]


def _rw_recurrence(r, k, v, lw, kk, a, g, rk, lng, lnb, tt):
    B, S, D = r.shape
    tok = pl.BlockSpec((None, tt, LANES), lambda b, hp, t: (b, t, hp))
    vec = pl.BlockSpec((1, LANES), lambda b, hp, t: (0, hp))
    return pl.pallas_call(
        functools.partial(_rec_body, n_chunks=tt // REC_CHUNK),
        grid=(B, D // LANES, S // tt),
        in_specs=[tok] * 7 + [vec] * 3,
        out_specs=tok,
        out_shape=jax.ShapeDtypeStruct((B, S, D), F32),
        scratch_shapes=[pltpu.VMEM((LANES, LANES), F32)],
        compiler_params=_cp("parallel", "parallel", "arbitrary"), name="rwkv_recurrence",
    )(r, k, v, lw, kk, a, g, rk, lng, lnb)


def _proj_res_body(y_ref, w_ref, x_ref, mod_ref, g_ref, o_ref):
    y = _dot(_bf(y_ref[...]), w_ref[...])
    o_ref[...] = x_ref[...] + mod_ref[2:3, :] * _rms(y, g_ref[...])


def _proj_res(y, w, x, mod, g, ts):
    B, S, D = x.shape
    tok = pl.BlockSpec((None, ts, D), lambda b, i: (b, i, 0))
    return pl.pallas_call(
        _proj_res_body, grid=(B, S // ts),
        in_specs=[tok, pl.BlockSpec(w.shape, lambda b, i: (0, 0)), tok,
                  pl.BlockSpec((None, 6, D), lambda b, i: (b, 0, 0)),
                  pl.BlockSpec(g.shape, lambda b, i: (0, 0))],
        out_specs=tok, out_shape=jax.ShapeDtypeStruct((B, S, D), F32),
        compiler_params=_cp("parallel", "parallel"), name="proj_residual",
    )(y, w, x, mod, g)


def _mlp_body(x_ref, mod_ref, g2_ref, g3_ref, up_ref, dn_ref, o_ref, h_ref, acc_ref):
    f = pl.program_id(2)

    @pl.when(f == 0)
    def _():
        h = _rms(x_ref[...], g2_ref[...]) * (1 + mod_ref[4:5, :]) + mod_ref[3:4, :]
        h_ref[...] = _bf(h)
        acc_ref[...] = jnp.zeros_like(acc_ref)

    z = jnp.maximum(_dot(h_ref[...], up_ref[...]), 0.0)
    acc_ref[...] += _dot(_bf(z * z), dn_ref[...])

    @pl.when(f == pl.num_programs(2) - 1)
    def _():
        o_ref[...] = x_ref[...] + mod_ref[5:6, :] * _rms(acc_ref[...], g3_ref[...])


def _mlp(x, mod, g2, g3, up, dn, ts, tf):
    B, S, D = x.shape
    F = up.shape[1]
    tok = pl.BlockSpec((None, ts, D), lambda b, i, f: (b, i, 0))
    vec = pl.BlockSpec((1, D), lambda b, i, f: (0, 0))
    return pl.pallas_call(
        _mlp_body, grid=(B, S // ts, F // tf),
        in_specs=[tok, pl.BlockSpec((None, 6, D), lambda b, i, f: (b, 0, 0)), vec, vec,
                  pl.BlockSpec((D, tf), lambda b, i, f: (0, f)),
                  pl.BlockSpec((tf, D), lambda b, i, f: (f, 0))],
        out_specs=tok, out_shape=jax.ShapeDtypeStruct((B, S, D), F32),
        scratch_shapes=[pltpu.VMEM((ts, D), BF16), pltpu.VMEM((ts, D), F32)],
        compiler_params=_cp("parallel", "parallel", "arbitrary"), name="mlp",
    )(x, mod, g2, g3, up, dn)


def _nsa_proj_body(x_ref, mod_ref, mkv_ref, g_ref, gkv_ref, wq_ref, wg_ref, wc_ref, wt_ref,
                   q_o, gt_o, c_o, t_o):
    x = x_ref[...]
    xn = x * lax.rsqrt(jnp.mean(x * x, axis=-1, keepdims=True) + NORM_EPS)
    h = _bf((xn * g_ref[...]) * (1 + mod_ref[1:2, :]) + mod_ref[0:1, :])
    hkv = _bf((xn * gkv_ref[...]) * (1 + mkv_ref[1:2, :]) + mkv_ref[0:1, :])
    q_o[...] = _bf(_dot(h, wq_ref[...]) * (HEAD_DIM ** -0.5 * LOG2E))
    gt_o[...] = jax.nn.sigmoid(_dot(h, wg_ref[...]))
    c_o[...] = _dot(hkv, wc_ref[...])
    t_o[...] = _bf(_dot_nt(wt_ref[...], hkv))


def _nsa_proj(x, mod, mkv, g, gkv, wq, wg, wc, wt, ts):
    B, S, D = x.shape
    tok = lambda n: pl.BlockSpec((None, ts, n), lambda b, i: (b, i, 0))
    full = lambda a: pl.BlockSpec(a.shape, lambda b, i: (0,) * a.ndim)
    return pl.pallas_call(
        _nsa_proj_body, grid=(B, S // ts),
        in_specs=[tok(D), pl.BlockSpec((None, 6, D), lambda b, i: (b, 0, 0)),
                  pl.BlockSpec((None, 2, D), lambda b, i: (b, 0, 0)),
                  full(g), full(gkv), full(wq), full(wg), full(wc), full(wt)],
        out_specs=[tok(wq.shape[1]), tok(wg.shape[1]), tok(wc.shape[1]),
                   pl.BlockSpec((None, wt.shape[0], ts), lambda b, i: (b, 0, i))],
        out_shape=[jax.ShapeDtypeStruct((B, S, wq.shape[1]), BF16),
                   jax.ShapeDtypeStruct((B, S, wg.shape[1]), F32),
                   jax.ShapeDtypeStruct((B, S, wc.shape[1]), F32),
                   jax.ShapeDtypeStruct((B, wt.shape[0], S), BF16)],
        compiler_params=_cp("parallel", "parallel"), name="nsa_proj",
    )(x, mod, mkv, g, gkv, wq, wg, wc, wt)


def _cmp1_body(k0_ref, k1_ref, v0_ref, v1_ref, pk_ref, pv_ref, wk_ref, wv_ref, fk_o, fv_o, *, n_chunk):
    hid = wk_ref.shape[1]
    gpr = LANES // HEAD_DIM
    for srcs, pos, w1, out in (((k0_ref, k1_ref), pk_ref, wk_ref, fk_o), ((v0_ref, v1_ref), pv_ref, wv_ref, fv_o)):
        acc = [[jnp.zeros((n_chunk, hid), F32) for _ in range(CMP_RATIO)] for _ in range(NSA_GROUPS)]
        for l in range(CMP_STRIDE):
            for si, src in enumerate(srcs):
                rows = src[pl.ds(l, n_chunk, stride=CMP_STRIDE), :]
                for gi in range(gpr):
                    g = si * gpr + gi
                    t = rows[:, gi * HEAD_DIM:(gi + 1) * HEAD_DIM]
                    for half in range(CMP_RATIO):
                        p = half * CMP_STRIDE + l
                        lhs = _bf(t + pos[p:p + 1, :])
                        acc[g][half] = acc[g][half] + _dot(lhs, w1[p * HEAD_DIM:(p + 1) * HEAD_DIM, :])
        for g in range(NSA_GROUPS):
            for half in range(CMP_RATIO):
                out[g, :, half * hid:(half + 1) * hid] = acc[g][half]


def _cmp1(kvc, pos_k, pos_v, w1k, w1v, ts):
    B, S, _ = kvc.shape
    G = NSA_GROUPS
    hid = w1k.shape[1]
    nch = ts // CMP_STRIDE
    full = lambda a: pl.BlockSpec(a.shape, lambda b, i: (0,) * a.ndim)
    src = lambda j: pl.BlockSpec((None, ts, LANES), lambda b, i: (b, i, j))
    outs = pl.BlockSpec((None, G, nch, CMP_RATIO * hid), lambda b, i: (b, 0, i, 0))
    return pl.pallas_call(
        functools.partial(_cmp1_body, n_chunk=nch), grid=(B, S // ts),
        in_specs=[src(0), src(1), src(2), src(3), full(pos_k), full(pos_v), full(w1k), full(w1v)],
        out_specs=[outs, outs],
        out_shape=[jax.ShapeDtypeStruct((B, G, S // CMP_STRIDE, CMP_RATIO * hid), F32)] * 2,
        compiler_params=_cp("parallel", "parallel"), name="cmp_stage1",
    )(kvc, kvc, kvc, kvc, pos_k, pos_v, w1k, w1v)


def _gelu_tanh(x):
    return 0.5 * x * (1.0 + jnp.tanh(math.sqrt(2.0 / math.pi) * (x + 0.044715 * (x * x * x))))


def _cmp2_body(fk_ref, fv_ref, w2kt_ref, w2v_ref, kt_o, v_o):
    nc = fk_ref.shape[0]
    hid = w2v_ref.shape[0]
    row = lax.broadcasted_iota(jnp.int32, (nc, hid), 0)

    def hidden(f):
        nxt = jnp.where(row == nc - 1, 0.0, pltpu.roll(f[:, hid:2 * hid], nc - 1, axis=0))
        return _bf(_gelu_tanh(f[:, 0:hid] + nxt))

    kt_o[...] = _bf(_dot_nt(w2kt_ref[...], hidden(fk_ref[...])))
    v_o[...] = _bf(_dot(hidden(fv_ref[...]), w2v_ref[...]))


def _cmp2(fk, fv, w2kt, w2v):
    B, G, NC, H2 = fk.shape
    fin = pl.BlockSpec((None, None, NC, H2), lambda b, g: (b, g, 0, 0))
    return pl.pallas_call(
        _cmp2_body, grid=(B, G),
        in_specs=[fin, fin, pl.BlockSpec(w2kt.shape, lambda b, g: (0, 0)),
                  pl.BlockSpec(w2v.shape, lambda b, g: (0, 0))],
        out_specs=[pl.BlockSpec((None, None, HEAD_DIM, NC), lambda b, g: (b, g, 0, 0)),
                   pl.BlockSpec((None, None, NC, HEAD_DIM), lambda b, g: (b, g, 0, 0))],
        out_shape=[jax.ShapeDtypeStruct((B, G, HEAD_DIM, NC), BF16),
                   jax.ShapeDtypeStruct((B, G, NC, HEAD_DIM), BF16)],
        compiler_params=_cp("parallel", "parallel"), name="cmp_stage2",
    )(fk, fv, w2kt, w2v)


def _head_queries(q_ref):
    return [q_ref[:, h * HEAD_DIM:(h + 1) * HEAD_DIM] for h in range(NSA_HPG)]


def _softmax_attend(q_ref, kt, bias, pv, o_ref, tq, want_psum):
    rb = SOFTMAX_ROWS
    n_blk = tq // rb
    qs = _head_queries(q_ref)
    scores = {0: _dot(qs[0], kt)}
    tot = [None] * n_blk
    for h in range(NSA_HPG):
        if h + 1 < NSA_HPG:
            scores[h + 1] = _dot(qs[h + 1], kt)
        s_h = scores.pop(h)
        blocks = []
        for b in range(n_blk):
            s = s_h[b * rb:(b + 1) * rb, :] + bias[b * rb:(b + 1) * rb, :]
            m = jnp.maximum(jnp.max(s, axis=-1, keepdims=True), 0.1 * NEG_INF)
            e = jnp.exp2(s - m)
            p = e * (1.0 / jnp.maximum(jnp.sum(e, axis=-1, keepdims=True), 1e-30))
            blocks.append(_bf(p))
            if want_psum:
                tot[b] = p if tot[b] is None else tot[b] + p
        o_ref[:, h * HEAD_DIM:(h + 1) * HEAD_DIM] = pv(jnp.concatenate(blocks, axis=0))
    return tot


def _cmp_attn_body(q_ref, kt_ref, v_ref, wimp_ref, o_ref, sel_ref, imp_ref, *, tq):
    i = pl.program_id(2)
    nc = kt_ref.shape[1]
    nb = wimp_ref.shape[1]
    step = min(CMP_COLS, nc)
    n_need = (i * tq + tq - CMP_LEN) // CMP_STRIDE + 1
    for ncv in range(step, nc + 1, step):
        @pl.when((n_need > ncv - step) & ((n_need <= ncv) | (ncv == nc)))
        def _():
            t = i * tq + lax.broadcasted_iota(jnp.int32, (tq, ncv), 0)
            cmp_end = lax.broadcasted_iota(jnp.int32, (tq, ncv), 1) * CMP_STRIDE + (CMP_LEN - 1)
            bias = jnp.where(cmp_end <= t, 0.0, NEG_INF)
            v = v_ref[0:ncv, :]
            tot = _softmax_attend(q_ref, kt_ref[:, 0:ncv], bias, lambda p: _dot(p, v), o_ref, tq, True)
            wimp = wimp_ref[0:ncv, :]
            imp_ref[...] = sum(_dot(part, wimp) for part in _split_bf16(jnp.concatenate(tot, axis=0), 3))

    imp = imp_ref[...].T
    blk = lax.broadcasted_iota(jnp.int32, (nb, tq), 0)
    cur = (i * tq + lax.broadcasted_iota(jnp.int32, (nb, tq), 1)) // SLC_LEN
    forced = (blk == 0) | (blk == cur) | (blk == cur - 1)
    preselect = (i * tq) // SLC_LEN >= 2
    imp = jnp.where(forced, jnp.where(preselect, -3e38, FORCED_SCORE), imp)
    imp = jnp.where(blk > cur, -1.0, imp)
    sel0 = jnp.where(forced & preselect, 1.0, 0.0)
    n_picks = jnp.where(preselect, min(TOP_N, nb) - 3, min(TOP_N, nb))
    blk_f = blk.astype(F32)

    def pick(_, carry):
        imp, sel = carry
        m = jnp.max(imp, axis=0, keepdims=True)
        first = jnp.min(jnp.where(imp == m, blk_f, float(nb)), axis=0, keepdims=True)
        hit = blk_f == first
        return jnp.where(hit, -3e38, imp), jnp.where(hit, 1.0, sel)

    _, sel = lax.fori_loop(0, n_picks, pick, (imp, sel0))
    sel_ref[...] = _bf(jnp.where(sel.T > 0.5, 0.0, NEG_INF))


def _cmp_attn(q, kct, vc, wimp, tq):
    B, S, _ = q.shape
    G = NSA_GROUPS
    NC = kct.shape[3]
    NB = wimp.shape[1]
    GD = NSA_HPG * HEAD_DIM
    return pl.pallas_call(
        functools.partial(_cmp_attn_body, tq=tq), grid=(B, G, S // tq),
        in_specs=[pl.BlockSpec((None, tq, GD), lambda b, g, i: (b, i, g)),
                  pl.BlockSpec((None, None, HEAD_DIM, NC), lambda b, g, i: (b, g, 0, 0)),
                  pl.BlockSpec((None, None, NC, HEAD_DIM), lambda b, g, i: (b, g, 0, 0)),
                  pl.BlockSpec(wimp.shape, lambda b, g, i: (0, 0))],
        out_specs=[pl.BlockSpec((None, tq, GD), lambda b, g, i: (b, i, g)),
                   pl.BlockSpec((None, None, tq, NB), lambda b, g, i: (b, g, i, 0))],
        out_shape=[jax.ShapeDtypeStruct((B, S, G * GD), F32),
                   jax.ShapeDtypeStruct((B, G, S, NB), BF16)],
        scratch_shapes=[pltpu.VMEM((tq, NB), F32)],
        compiler_params=_cp("parallel", "parallel", "parallel"), name="cmp_attention_topk",
    )(q, kct, vc, wimp)


def _slc_attn_body(q_ref, selb_ref, kt_ref, vt_ref, e_ref, o_ref, lhs_ref, s0_ref, s1_ref, p0_ref, p1_ref,
                   alpha_ref, m_ref, acc_ref, *, tq, tk, wb):
    i = pl.program_id(2)
    rb = SOFTMAX_ROWS
    rows = NSA_HPG * tq
    tiles_per_window = wb * SLC_LEN // tk
    for h in range(NSA_HPG):
        lhs_ref[pl.ds(h * tq, tq), 0:HEAD_DIM] = q_ref[:, h * HEAD_DIM:(h + 1) * HEAD_DIM]
    lhs_ref[:, HEAD_DIM:LANES] = jnp.zeros((rows, LANES - HEAD_DIM), BF16)
    m_ref[...] = jnp.full(m_ref.shape, NEG_INF, F32)
    acc_ref[...] = jnp.zeros_like(acc_ref)
    s_refs = (s0_ref, s1_ref)
    p_refs = (p0_ref, p1_ref)
    p1_ref[...] = jnp.zeros((rows, tk), BF16)
    ones = jnp.ones((LANES - HEAD_DIM, tk), BF16)
    zeros = jnp.zeros((LANES - HEAD_DIM, tk), BF16)
    tile_keys = lambda j: pl.ds(pl.multiple_of(j * tk, tk), tk)

    def set_window(w):
        win = selb_ref[:, pl.ds(pl.multiple_of(w * wb, wb), wb)]
        for h in range(NSA_HPG):
            lhs_ref[pl.ds(h * tq, tq), LANES:LANES + wb] = win

    def scores(j, s_ref):
        @pl.when(j % tiles_per_window == 0)
        def _():
            set_window(j // tiles_per_window)
        ks = tile_keys(j)
        w0 = pl.multiple_of((j // tiles_per_window) * wb, wb)
        rhs = jnp.concatenate([kt_ref[:, ks], zeros, e_ref[pl.ds(w0, wb), ks]], axis=0)
        s_ref[...] = _dot(lhs_ref[...], rhs)

    def value_product(j, p_ref):
        vt = jnp.concatenate([vt_ref[:, tile_keys(j)], ones], axis=0)
        return _dot_nt(p_ref[...], vt)

    def tile(j, par, last_tile):
        if last_tile:
            t = i * tq + lax.broadcasted_iota(jnp.int32, (tq, tk), 0)
            causal = j * tk + lax.broadcasted_iota(jnp.int32, (tq, tk), 1) <= t
        else:
            scores(j + 1, s_refs[1 - par])
        pv_prev = value_product(jnp.maximum(j - 1, 0), p_refs[1 - par])
        for h in range(NSA_HPG):
            for b in range(tq // rb):
                rs = pl.ds(h * tq + b * rb, rb)
                s = s_refs[par][rs, :]
                if last_tile:
                    s = jnp.where(causal[b * rb:(b + 1) * rb, :], s, NEG_INF)
                m_old = m_ref[rs, :]
                m_new = jnp.maximum(m_old, jnp.max(s, axis=-1, keepdims=True))
                ps = [jnp.exp2(s[:, c * LANES:(c + 1) * LANES] - m_new) for c in range(tk // LANES)]
                m_ref[rs, :] = m_new
                alpha_ref[rs, :] = jnp.exp2(m_old - m_new)
                p_refs[par][rs, :] = _bf(jnp.concatenate(ps, axis=1))
        acc_ref[...] = (acc_ref[...] + pv_prev) * alpha_ref[...]

    last = (i * tq + tq - 1) // tk
    scores(0, s_refs[0])

    def two_tiles(jj, carry):
        tile(2 * jj, 0, False)
        tile(2 * jj + 1, 1, False)
        return carry

    lax.fori_loop(0, last // 2, two_tiles, 0)

    @pl.when(last % 2 == 1)
    def _():
        tile(last - 1, 0, False)

    for par in range(2):
        @pl.when(last % 2 == par)
        def _():
            tile(last, par, True)
            acc = acc_ref[...] + value_product(last, p_refs[par])
            for h in range(NSA_HPG):
                a_h = acc[h * tq:(h + 1) * tq, :]
                o_ref[:, h * HEAD_DIM:(h + 1) * HEAD_DIM] = (
                    a_h[:, :HEAD_DIM] / jnp.maximum(a_h[:, HEAD_DIM:], 1e-30))


def _slc_attn(q, selb, kvt, e, tq, tk, wb):
    B, S, _ = q.shape
    G = NSA_GROUPS
    NB = selb.shape[3]
    GD = NSA_HPG * HEAD_DIM
    rows = NSA_HPG * tq
    return pl.pallas_call(
        functools.partial(_slc_attn_body, tq=tq, tk=tk, wb=wb), grid=(B, G, S // tq),
        in_specs=[pl.BlockSpec((None, tq, GD), lambda b, g, i: (b, i, g)),
                  pl.BlockSpec((None, None, tq, NB), lambda b, g, i: (b, g, i, 0)),
                  pl.BlockSpec((None, HEAD_DIM, S), lambda b, g, i: (b, g, 0)),
                  pl.BlockSpec((None, HEAD_DIM, S), lambda b, g, i: (b, G + g, 0)),
                  pl.BlockSpec(e.shape, lambda b, g, i: (0, 0))],
        out_specs=pl.BlockSpec((None, tq, GD), lambda b, g, i: (b, i, g)),
        out_shape=jax.ShapeDtypeStruct((B, S, G * GD), F32),
        scratch_shapes=[pltpu.VMEM((rows, LANES + wb), BF16), pltpu.VMEM((rows, tk), F32), pltpu.VMEM((rows, tk), F32),
                        pltpu.VMEM((rows, tk), BF16), pltpu.VMEM((rows, tk), BF16),
                        pltpu.VMEM((rows, LANES), F32), pltpu.VMEM((rows, LANES), F32),
                        pltpu.VMEM((rows, LANES), F32)],
        compiler_params=_cp("parallel", "parallel", "parallel"), name="slc_attention",
    )(q, selb, kvt, kvt, e)


def _win_attn_body(q_ref, kt_ref, vt_ref, o_ref, *, tq, span):
    i = pl.program_id(2)
    start = pl.multiple_of(jnp.maximum(i * tq - WINDOW, 0), LANES)
    ks = pl.ds(start, span)
    t = i * tq + lax.broadcasted_iota(jnp.int32, (tq, span), 0)
    dist = t - (start + lax.broadcasted_iota(jnp.int32, (tq, span), 1))
    bias = jnp.where((dist >= 0) & (dist < WINDOW), 0.0, NEG_INF)
    vt = vt_ref[:, ks]
    _softmax_attend(q_ref, kt_ref[:, ks], bias, lambda p: _dot_nt(p, vt), o_ref, tq, False)


def _win_attn(q, kvt, tq):
    B, S, _ = q.shape
    G = NSA_GROUPS
    GD = NSA_HPG * HEAD_DIM
    span = min(tq + WINDOW, S)
    return pl.pallas_call(
        functools.partial(_win_attn_body, tq=tq, span=span), grid=(B, G, S // tq),
        in_specs=[pl.BlockSpec((None, tq, GD), lambda b, g, i: (b, i, g)),
                  pl.BlockSpec((None, HEAD_DIM, S), lambda b, g, i: (b, 2 * G + g, 0)),
                  pl.BlockSpec((None, HEAD_DIM, S), lambda b, g, i: (b, 3 * G + g, 0))],
        out_specs=pl.BlockSpec((None, tq, GD), lambda b, g, i: (b, i, g)),
        out_shape=jax.ShapeDtypeStruct((B, S, G * GD), F32),
        compiler_params=_cp("parallel", "parallel", "parallel"), name="win_attention",
    )(q, kvt, kvt)


def _nsa_out_body(oc_ref, os_ref, ow_ref, gt_ref, ge_ref, w_ref, x_ref, mod_ref, g_ref, o_ref):
    D = oc_ref.shape[1]
    gx = _dot(gt_ref[...], ge_ref[...], precision=HI)
    o = gx[:, 0:D] * oc_ref[...] + gx[:, D:2 * D] * os_ref[...] + gx[:, 2 * D:3 * D] * ow_ref[...]
    y = _dot(_bf(o), w_ref[...])
    o_ref[...] = x_ref[...] + mod_ref[2:3, :] * _rms(y, g_ref[...])


def _nsa_out(oc, osl, ow, gates, gexp, w, x, mod, g, ts):
    B, S, D = x.shape
    tok = lambda n: pl.BlockSpec((None, ts, n), lambda b, i: (b, i, 0))
    full = lambda a: pl.BlockSpec(a.shape, lambda b, i: (0,) * a.ndim)
    return pl.pallas_call(
        _nsa_out_body, grid=(B, S // ts),
        in_specs=[tok(D), tok(D), tok(D), tok(gates.shape[2]), full(gexp), full(w), tok(D),
                  pl.BlockSpec((None, 6, D), lambda b, i: (b, 0, 0)), full(g)],
        out_specs=tok(D), out_shape=jax.ShapeDtypeStruct((B, S, D), F32),
        compiler_params=_cp("parallel", "parallel"), name="nsa_out",
    )(oc, osl, ow, gates, gexp, w, x, mod, g)


def _importance_matrix(nc, nb):
    n_cmp = nc - CMP_RATIO + 1
    w = np.zeros((nc, nb), np.float32)
    for m in range(SLC_RATIO):
        for n in range(CMP_RATIO):
            off = m - n + CMP_RATIO - 1
            for j in range(nb):
                src = SLC_RATIO * j + off - (CMP_RATIO - 1)
                if 0 <= src < n_cmp:
                    w[src, j] += 1.0
    return jnp.asarray(w, BF16)


def _block_expand(nb, s):
    return jnp.asarray(np.arange(nb)[:, None] == (np.arange(s)[None, :] // SLC_LEN), BF16)


def _gate_expand(n_in):
    hd = NSA_HEADS * HEAD_DIM
    w = np.zeros((n_in, 3 * hd), np.float32)
    for head in range(NSA_HEADS):
        for br in range(3):
            w[head * 3 + br, br * hd + head * HEAD_DIM: br * hd + (head + 1) * HEAD_DIM] = 1.0
    return jnp.asarray(w)


def kernel(x, c, ada_w, ada_b, norm_g, mlp_up, mlp_down, rw_mu, rw_r, rw_k, rw_v, rw_o, rw_w0, rw_wa, rw_wb,
           rw_a0, rw_aa, rw_ab, rw_ga, rw_gb, rw_kk, rw_ka, rw_rk, rw_lng, rw_lnb, kv_norm_g, kv_ada_w,
           kv_ada_b, kv_w, cmp_pos_k, cmp_pos_v, cmp_k_w1, cmp_k_w2, cmp_v_w1, cmp_v_w2, nsa_wqg, nsa_wo):
    B, S, D = x.shape
    depth = ada_w.shape[0]
    n_a = rw_mu.shape[0]
    bf = _bf
    row = lambda a: a.reshape(1, -1)
    ts = min(512, S)

    c8 = jnp.pad(c, ((0, 8 - B), (0, 0)))
    mods = _ada(c8, ada_w, ada_b)[:, :B].reshape(depth, B, 6, D)
    mkv = _ada(c8, kv_ada_w[None], kv_ada_b[None])[0, :B].reshape(B, 2, D)

    GD = NSA_GROUPS * HEAD_DIM
    HD = NSA_HEADS * HEAD_DIM
    shared = None
    for layer in range(depth):
        mod = mods[layer]
        ng = norm_g[layer]
        if layer < n_a:
            i = layer
            r, k, v, lw, kk, a, g = _rw_front(
                x, mod, row(ng[0]), rw_mu[i], bf(rw_r[i]), bf(rw_k[i]), bf(rw_v[i]), bf(rw_wa[i]),
                bf(rw_wb[i]), row(rw_w0[i]), bf(rw_aa[i]), bf(rw_ab[i]), row(rw_a0[i]), bf(rw_ga[i]),
                bf(rw_gb[i]), row(rw_kk[i]), row(rw_ka[i]), min(256, S))
            y = _rw_recurrence(r, k, v, lw, kk, a, g, row(rw_rk[i]), row(rw_lng[i]), row(rw_lnb[i]),
                               min(1024, S))
            x = _proj_res(y, bf(rw_o[i]), x, mod, row(ng[1]), ts)
        else:
            j = layer - n_a
            wqg = nsa_wqg[j]
            wq = bf(wqg[:, :HD])
            wg = bf(jnp.pad(wqg[:, HD:], ((0, 0), (0, LANES - 3 * NSA_HEADS))))
            if shared is None:
                wc = bf(kv_w[:, :2 * GD])
                wt = bf(kv_w[:, 2 * GD:].T)
            q, gates, kvc, kvt = _nsa_proj(x, mod, mkv, row(ng[0]), row(kv_norm_g), wq, wg, wc, wt, ts)
            if shared is None:
                fk, fv = _cmp1(kvc, cmp_pos_k, cmp_pos_v, bf(cmp_k_w1), bf(cmp_v_w1), min(1024, S))
                kct, vc = _cmp2(fk, fv, bf(cmp_k_w2.T), bf(cmp_v_w2))
                shared = (kct, vc, kvt)
            kct, vc, kvt_s = shared
            nc, nb = S // CMP_STRIDE, S // SLC_LEN
            tq = min(128, S)
            oc, sel = _cmp_attn(q, kct, vc, _importance_matrix(nc, nb), tq)
            osl = _slc_attn(q, sel, kvt_s, _block_expand(nb, S), min(256, S), min(512, S), min(LANES, nb))
            ow = _win_attn(q, kvt_s, min(256, S))
            x = _nsa_out(oc, osl, ow, gates, _gate_expand(gates.shape[2]), bf(nsa_wo[j]), x, mod,
                         row(ng[1]), ts)
        x = _mlp(x, mod, row(ng[2]), row(ng[3]), bf(mlp_up[layer]), bf(mlp_down[layer]), min(1024, S),
                 1024)
    return x
```

```python
import functools
import math

import numpy as np
import jax
import jax.numpy as jnp
from jax import lax
from jax.experimental import pallas as pl
from jax.experimental.pallas import tpu as pltpu

F32 = jnp.float32
BF16 = jnp.bfloat16
HI = lax.Precision.HIGHEST

NORM_EPS = 1e-6
GN_EPS = 64e-5
RWKV_HEAD = 64
HEAD_DIM = 64
NSA_GROUPS = 4
NSA_HPG = 4
NSA_HEADS = 16
CMP_LEN = 32
CMP_STRIDE = 16
CMP_RATIO = CMP_LEN // CMP_STRIDE
SLC_LEN = 64
SLC_RATIO = SLC_LEN // CMP_STRIDE
TOP_N = 16
WINDOW = 512
FORCED_SCORE = 1e4
NEG_INF = -1e30
LOG2E = math.log2(math.e)

LANES = 128
VMEM_LIMIT = 56 * 1024 * 1024
REC_CHUNK = 64
SOFTMAX_ROWS = 32
CMP_COLS = 256


def _cp(*sem):
    return pltpu.CompilerParams(dimension_semantics=sem, vmem_limit_bytes=VMEM_LIMIT)


def _dot(a, b, **kw):
    return jnp.dot(a, b, preferred_element_type=F32, **kw)


def _dot_nt(a, b, **kw):
    return lax.dot_general(a, b, (((1,), (1,)), ((), ())), preferred_element_type=F32, **kw)


def _dot_tn(a, b, **kw):
    return lax.dot_general(a, b, (((0,), (0,)), ((), ())), preferred_element_type=F32, **kw)


def _bf(t):
    return t.astype(BF16)


def _split_bf16(x, parts):
    out = []
    for _ in range(parts - 1):
        hi = _bf(x)
        out.append(hi)
        x = x - hi.astype(F32)
    out.append(_bf(x))
    return out


def _rms(x, g):
    return x * lax.rsqrt(jnp.mean(x * x, axis=-1, keepdims=True) + NORM_EPS) * g


def _ada_body(c_ref, w_ref, b_ref, o_ref):
    c = c_ref[...]
    o_ref[...] = _dot(c * jax.nn.sigmoid(c), w_ref[...], precision=HI) + b_ref[...]


def _ada(c8, w, b):
    L, D, N = w.shape
    tn = 1024
    return pl.pallas_call(
        _ada_body, grid=(L, N // tn),
        in_specs=[pl.BlockSpec((8, D), lambda l, j: (0, 0)),
                  pl.BlockSpec((None, D, tn), lambda l, j: (l, 0, j)),
                  pl.BlockSpec((None, 1, tn), lambda l, j: (l, 0, j))],
        out_specs=pl.BlockSpec((None, 8, tn), lambda l, j: (l, 0, j)),
        out_shape=jax.ShapeDtypeStruct((L, 8, N), F32),
        compiler_params=_cp("parallel", "parallel"), name="ada",
    )(c8, w, b.reshape(L, 1, N))


def _rw_front_body(x_ref, xp_ref, mod_ref, g_ref, mu_ref, wr, wk, wv, wa, wb, w0, aa, ab, a0, ga, gb,
                   kk_ref, ka_ref, r_o, k_o, v_o, lw_o, kk_o, a_o, g_o):
    i = pl.program_id(1)
    g = g_ref[...]
    sh, sc = mod_ref[0:1, :], mod_ref[1:2, :]
    h = _rms(x_ref[...], g) * (1 + sc) + sh
    hp = _rms(xp_ref[...], g) * (1 + sc) + sh
    hp = jnp.where(i > 0, hp[7:8, :], 0.0)
    row = lax.broadcasted_iota(jnp.int32, h.shape, 0)
    hs = jnp.where(row == 0, hp, pltpu.roll(h, 1, axis=0))
    xx = hs - h
    mu = mu_ref[...]
    xr, xw, xk, xv, xa, xg = [_bf(h + xx * mu[j:j + 1, :]) for j in range(6)]
    r = _dot(xr, wr[...])
    k = _dot(xk, wk[...])
    v = _dot(xv, wv[...])
    wl = w0[...] + _dot(_bf(jnp.tanh(_dot(xw, wa[...]))), wb[...])
    lw = -jax.nn.sigmoid(wl) * math.exp(-0.5)
    a = jax.nn.sigmoid(a0[...] + _dot(_bf(_dot(xa, aa[...])), ab[...]))
    gate = _dot(_bf(jax.nn.sigmoid(_dot(xg, ga[...]))), gb[...])
    r_o[...] = r
    k_o[...] = k * (1 + (a - 1) * ka_ref[...])
    v_o[...] = v
    lw_o[...] = lw
    kk_o[...] = k * kk_ref[...]
    a_o[...] = a
    g_o[...] = gate


def _rw_front(x, mod, g, mu, wr, wk, wv, wa, wb, w0, aa, ab, a0, ga, gb, kk, ka, ts):
    B, S, D = x.shape
    full = lambda a: pl.BlockSpec(a.shape, lambda b, i: (0,) * a.ndim)
    tok = pl.BlockSpec((None, ts, D), lambda b, i: (b, i, 0))
    ws = [wr, wk, wv, wa, wb, w0, aa, ab, a0, ga, gb, kk, ka]
    return pl.pallas_call(
        _rw_front_body, grid=(B, S // ts),
        in_specs=[tok,
                  pl.BlockSpec((None, 8, D), lambda b, i: (b, jnp.maximum(i * (ts // 8) - 1, 0), 0)),
                  pl.BlockSpec((None, 6, D), lambda b, i: (b, 0, 0)),
                  full(g), full(mu)] + [full(w) for w in ws],
        out_specs=[tok] * 7,
        out_shape=[jax.ShapeDtypeStruct((B, S, D), F32)] * 7,
        compiler_params=_cp("parallel", "parallel"), name="rwkv_front",
    )(x, x, mod, g, mu, *ws)


def _rec_body(r_ref, k_ref, v_ref, lw_ref, kk_ref, a_ref, g_ref, rk_ref, lng_ref, lnb_ref, o_ref, st_ref,
              *, n_chunks):
    C = REC_CHUNK
    L = LANES
    C2 = 2 * C

    @pl.when(pl.program_id(2) == 0)
    def _():
        st_ref[...] = jnp.zeros_like(st_ref)

    h0 = lax.broadcasted_iota(jnp.int32, (C, L), 1) < RWKV_HEAD
    rr = lax.broadcasted_iota(jnp.int32, (C2, C2), 0)
    cc = lax.broadcasted_iota(jnp.int32, (C2, C2), 1)
    same = rr // C == cc // C
    incl = same & (rr >= cc)
    strict = same & (rr > cc)
    eye = (rr == cc).astype(F32)
    lvl_masks = [(rr // sz == cc // sz) & (rr // (sz // 2) != cc // (sz // 2))
                 for sz in (2 ** e for e in range(1, int(math.log2(C)) + 1))]
    tri_b = (lax.broadcasted_iota(jnp.int32, (C, C), 0) >= lax.broadcasted_iota(jnp.int32, (C, C), 1)).astype(BF16)
    bd_b = (lax.broadcasted_iota(jnp.int32, (L, L), 0) // RWKV_HEAD
            == lax.broadcasted_iota(jnp.int32, (L, L), 1) // RWKV_HEAD).astype(BF16)
    rk, lng, lnb = rk_ref[...], lng_ref[...], lnb_ref[...]
    chunks = range(n_chunks)

    def head_sum(t):
        return sum(_dot(p, bd_b) for p in _split_bf16(t, 2))

    def stack(t):
        zero = jnp.zeros_like(t)
        return jnp.concatenate([jnp.where(h0, t, zero), jnp.where(h0, zero, t)], axis=0)

    ld = [dict(r=r_ref[pl.ds(c * C, C), :], k=k_ref[pl.ds(c * C, C), :], v=v_ref[pl.ds(c * C, C), :],
               lw=lw_ref[pl.ds(c * C, C), :], kk=kk_ref[pl.ds(c * C, C), :], a=a_ref[pl.ds(c * C, C), :])
          for c in chunks]
    kkn = [d["kk"] / jnp.maximum(jnp.sqrt(head_sum(d["kk"] * d["kk"])), 1e-12) for d in ld]
    cl = [sum(_dot(tri_b, p) for p in _split_bf16(d["lw"], 3)) for d in ld]
    cl_end = [x[C - 1:C, :] for x in cl]
    e_neg = [jnp.exp(-x) for x in cl]
    e_end = [jnp.exp(ce - x) for x, ce in zip(cl, cl_end)]
    rt2 = [_bf(stack(d["r"] * jnp.exp(x))) for d, x in zip(ld, cl)]
    at2 = [_bf(stack(-kn * jnp.exp(x - d["lw"]))) for d, kn, x in zip(ld, kkn, cl)]
    bt2 = [_bf(stack(kn * d["a"] * e)) for d, kn, e in zip(ld, kkn, e_neg)]
    kt2 = [_bf(stack(d["k"] * e)) for d, e in zip(ld, e_neg)]
    bend2 = [_bf(stack(kn * d["a"] * e)) for d, kn, e in zip(ld, kkn, e_end)]
    kend2 = [_bf(stack(d["k"] * e)) for d, e in zip(ld, e_end)]
    v2 = [_bf(stack(d["v"])) for d in ld]
    decay = [jnp.exp(x) for x in cl_end]

    gm = [_dot_nt(jnp.concatenate([a_, r_], axis=0), jnp.concatenate([b_, k_], axis=0))
          for a_, r_, b_, k_ in zip(at2, rt2, bt2, kt2)]
    a_ab = [jnp.where(strict, g_[:C2, :C2], 0.0) for g_ in gm]
    a_ak = [_bf(jnp.where(strict, g_[:C2, C2:], 0.0)) for g_ in gm]
    a_rb = [_bf(jnp.where(incl, g_[C2:, :C2], 0.0)) for g_ in gm]
    a_rk = [_bf(jnp.where(incl, g_[C2:, C2:], 0.0)) for g_ in gm]

    tm = [eye + jnp.where(lvl_masks[0], a_, 0.0) for a_ in a_ab]
    for lm in lvl_masks[1:]:
        off = [_bf(jnp.where(lm, a_, 0.0)) for a_ in a_ab]
        half = [_bf(_dot(_bf(t_), o_)) for t_, o_ in zip(tm, off)]
        tm = [t_ + _dot(h_, _bf(t_)) for t_, h_ in zip(tm, half)]
    tm = [_bf(t_) for t_ in tm]

    w2 = [_bf(_dot(t_, a_)) for t_, a_ in zip(tm, at2)]
    akv = [_bf(_dot(a_, v_)) for a_, v_ in zip(a_ak, v2)]
    uh2 = [_bf(_dot(t_, x_)) for t_, x_ in zip(tm, akv)]
    rw2 = [_bf(r_.astype(F32) + _dot(a_, w_)) for r_, a_, w_ in zip(rt2, a_rb, w2)]
    yh2 = [_dot(a_, u_) + _dot(b_, v_) for a_, u_, b_, v_ in zip(a_rb, uh2, a_rk, v2)]
    ec = [_bf(_dot_tn(w_, b_)) for w_, b_ in zip(w2, bend2)]
    qc = [_dot_tn(jnp.concatenate([u_, v_], axis=0), jnp.concatenate([b_, k_], axis=0))
          for u_, v_, b_, k_ in zip(uh2, v2, bend2, kend2)]

    n = st_ref[...]
    y2 = []
    for c in chunks:
        n_b = _bf(n)
        y2.append(_dot_nt(rw2[c], n_b) + yh2[c])
        n = n * decay[c] + _dot(n_b, ec[c]) + qc[c]
    st_ref[...] = n

    for c in chunks:
        d = ld[c]
        y = y2[c][:C] + y2[c][C:]
        mean = head_sum(y) * (1.0 / RWKV_HEAD)
        yc = y - mean
        var = head_sum(yc * yc) * (1.0 / RWKV_HEAD)
        yn = yc * lax.rsqrt(var + GN_EPS) * lng + lnb
        bonus = head_sum(d["r"] * d["k"] * rk) * d["v"]
        o_ref[pl.ds(c * C, C), :] = (yn + bonus) * g_ref[pl.ds(c * C, C), :]


def _rw_recurrence(r, k, v, lw, kk, a, g, rk, lng, lnb, tt):
    B, S, D = r.shape
    tok = pl.BlockSpec((None, tt, LANES), lambda b, hp, t: (b, t, hp))
    vec = pl.BlockSpec((1, LANES), lambda b, hp, t: (0, hp))
    return pl.pallas_call(
        functools.partial(_rec_body, n_chunks=tt // REC_CHUNK),
        grid=(B, D // LANES, S // tt),
        in_specs=[tok] * 7 + [vec] * 3,
        out_specs=tok,
        out_shape=jax.ShapeDtypeStruct((B, S, D), F32),
        scratch_shapes=[pltpu.VMEM((LANES, LANES), F32)],
        compiler_params=_cp("parallel", "parallel", "arbitrary"), name="rwkv_recurrence",
    )(r, k, v, lw, kk, a, g, rk, lng, lnb)


def _proj_res_body(y_ref, w_ref, x_ref, mod_ref, g_ref, o_ref):
    y = _dot(_bf(y_ref[...]), w_ref[...])
    o_ref[...] = x_ref[...] + mod_ref[2:3, :] * _rms(y, g_ref[...])


def _proj_res(y, w, x, mod, g, ts):
    B, S, D = x.shape
    tok = pl.BlockSpec((None, ts, D), lambda b, i: (b, i, 0))
    return pl.pallas_call(
        _proj_res_body, grid=(B, S // ts),
        in_specs=[tok, pl.BlockSpec(w.shape, lambda b, i: (0, 0)), tok,
                  pl.BlockSpec((None, 6, D), lambda b, i: (b, 0, 0)),
                  pl.BlockSpec(g.shape, lambda b, i: (0, 0))],
        out_specs=tok, out_shape=jax.ShapeDtypeStruct((B, S, D), F32),
        compiler_params=_cp("parallel", "parallel"), name="proj_residual",
    )(y, w, x, mod, g)


def _mlp_body(x_ref, mod_ref, g2_ref, g3_ref, up_ref, dn_ref, o_ref, h_ref, acc_ref):
    f = pl.program_id(2)

    @pl.when(f == 0)
    def _():
        h = _rms(x_ref[...], g2_ref[...]) * (1 + mod_ref[4:5, :]) + mod_ref[3:4, :]
        h_ref[...] = _bf(h)
        acc_ref[...] = jnp.zeros_like(acc_ref)

    z = jnp.maximum(_dot(h_ref[...], up_ref[...]), 0.0)
    acc_ref[...] += _dot(_bf(z * z), dn_ref[...])

    @pl.when(f == pl.num_programs(2) - 1)
    def _():
        o_ref[...] = x_ref[...] + mod_ref[5:6, :] * _rms(acc_ref[...], g3_ref[...])


def _mlp(x, mod, g2, g3, up, dn, ts, tf):
    B, S, D = x.shape
    F = up.shape[1]
    tok = pl.BlockSpec((None, ts, D), lambda b, i, f: (b, i, 0))
    vec = pl.BlockSpec((1, D), lambda b, i, f: (0, 0))
    return pl.pallas_call(
        _mlp_body, grid=(B, S // ts, F // tf),
        in_specs=[tok, pl.BlockSpec((None, 6, D), lambda b, i, f: (b, 0, 0)), vec, vec,
                  pl.BlockSpec((D, tf), lambda b, i, f: (0, f)),
                  pl.BlockSpec((tf, D), lambda b, i, f: (f, 0))],
        out_specs=tok, out_shape=jax.ShapeDtypeStruct((B, S, D), F32),
        scratch_shapes=[pltpu.VMEM((ts, D), BF16), pltpu.VMEM((ts, D), F32)],
        compiler_params=_cp("parallel", "parallel", "arbitrary"), name="mlp",
    )(x, mod, g2, g3, up, dn)


def _nsa_proj_body(x_ref, mod_ref, mkv_ref, g_ref, gkv_ref, wq_ref, wg_ref, wc_ref, wt_ref,
                   q_o, gt_o, c_o, t_o):
    x = x_ref[...]
    xn = x * lax.rsqrt(jnp.mean(x * x, axis=-1, keepdims=True) + NORM_EPS)
    h = _bf((xn * g_ref[...]) * (1 + mod_ref[1:2, :]) + mod_ref[0:1, :])
    hkv = _bf((xn * gkv_ref[...]) * (1 + mkv_ref[1:2, :]) + mkv_ref[0:1, :])
    q_o[...] = _bf(_dot(h, wq_ref[...]) * (HEAD_DIM ** -0.5 * LOG2E))
    gt_o[...] = jax.nn.sigmoid(_dot(h, wg_ref[...]))
    c_o[...] = _dot(hkv, wc_ref[...])
    t_o[...] = _bf(_dot_nt(wt_ref[...], hkv))


def _nsa_proj(x, mod, mkv, g, gkv, wq, wg, wc, wt, ts):
    B, S, D = x.shape
    tok = lambda n: pl.BlockSpec((None, ts, n), lambda b, i: (b, i, 0))
    full = lambda a: pl.BlockSpec(a.shape, lambda b, i: (0,) * a.ndim)
    return pl.pallas_call(
        _nsa_proj_body, grid=(B, S // ts),
        in_specs=[tok(D), pl.BlockSpec((None, 6, D), lambda b, i: (b, 0, 0)),
                  pl.BlockSpec((None, 2, D), lambda b, i: (b, 0, 0)),
                  full(g), full(gkv), full(wq), full(wg), full(wc), full(wt)],
        out_specs=[tok(wq.shape[1]), tok(wg.shape[1]), tok(wc.shape[1]),
                   pl.BlockSpec((None, wt.shape[0], ts), lambda b, i: (b, 0, i))],
        out_shape=[jax.ShapeDtypeStruct((B, S, wq.shape[1]), BF16),
                   jax.ShapeDtypeStruct((B, S, wg.shape[1]), F32),
                   jax.ShapeDtypeStruct((B, S, wc.shape[1]), F32),
                   jax.ShapeDtypeStruct((B, wt.shape[0], S), BF16)],
        compiler_params=_cp("parallel", "parallel"), name="nsa_proj",
    )(x, mod, mkv, g, gkv, wq, wg, wc, wt)


def _cmp1_body(k0_ref, k1_ref, v0_ref, v1_ref, pk_ref, pv_ref, wk_ref, wv_ref, fk_o, fv_o, *, n_chunk):
    hid = wk_ref.shape[1]
    gpr = LANES // HEAD_DIM
    for srcs, pos, w1, out in (((k0_ref, k1_ref), pk_ref, wk_ref, fk_o), ((v0_ref, v1_ref), pv_ref, wv_ref, fv_o)):
        acc = [[jnp.zeros((n_chunk, hid), F32) for _ in range(CMP_RATIO)] for _ in range(NSA_GROUPS)]
        for l in range(CMP_STRIDE):
            for si, src in enumerate(srcs):
                rows = src[pl.ds(l, n_chunk, stride=CMP_STRIDE), :]
                for gi in range(gpr):
                    g = si * gpr + gi
                    t = rows[:, gi * HEAD_DIM:(gi + 1) * HEAD_DIM]
                    for half in range(CMP_RATIO):
                        p = half * CMP_STRIDE + l
                        lhs = _bf(t + pos[p:p + 1, :])
                        acc[g][half] = acc[g][half] + _dot(lhs, w1[p * HEAD_DIM:(p + 1) * HEAD_DIM, :])
        for g in range(NSA_GROUPS):
            for half in range(CMP_RATIO):
                out[g, :, half * hid:(half + 1) * hid] = acc[g][half]


def _cmp1(kvc, pos_k, pos_v, w1k, w1v, ts):
    B, S, _ = kvc.shape
    G = NSA_GROUPS
    hid = w1k.shape[1]
    nch = ts // CMP_STRIDE
    full = lambda a: pl.BlockSpec(a.shape, lambda b, i: (0,) * a.ndim)
    src = lambda j: pl.BlockSpec((None, ts, LANES), lambda b, i: (b, i, j))
    outs = pl.BlockSpec((None, G, nch, CMP_RATIO * hid), lambda b, i: (b, 0, i, 0))
    return pl.pallas_call(
        functools.partial(_cmp1_body, n_chunk=nch), grid=(B, S // ts),
        in_specs=[src(0), src(1), src(2), src(3), full(pos_k), full(pos_v), full(w1k), full(w1v)],
        out_specs=[outs, outs],
        out_shape=[jax.ShapeDtypeStruct((B, G, S // CMP_STRIDE, CMP_RATIO * hid), F32)] * 2,
        compiler_params=_cp("parallel", "parallel"), name="cmp_stage1",
    )(kvc, kvc, kvc, kvc, pos_k, pos_v, w1k, w1v)


def _gelu_tanh(x):
    return 0.5 * x * (1.0 + jnp.tanh(math.sqrt(2.0 / math.pi) * (x + 0.044715 * (x * x * x))))


def _cmp2_body(fk_ref, fv_ref, w2kt_ref, w2v_ref, kt_o, v_o):
    nc = fk_ref.shape[0]
    hid = w2v_ref.shape[0]
    row = lax.broadcasted_iota(jnp.int32, (nc, hid), 0)

    def hidden(f):
        nxt = jnp.where(row == nc - 1, 0.0, pltpu.roll(f[:, hid:2 * hid], nc - 1, axis=0))
        return _bf(_gelu_tanh(f[:, 0:hid] + nxt))

    kt_o[...] = _bf(_dot_nt(w2kt_ref[...], hidden(fk_ref[...])))
    v_o[...] = _bf(_dot(hidden(fv_ref[...]), w2v_ref[...]))


def _cmp2(fk, fv, w2kt, w2v):
    B, G, NC, H2 = fk.shape
    fin = pl.BlockSpec((None, None, NC, H2), lambda b, g: (b, g, 0, 0))
    return pl.pallas_call(
        _cmp2_body, grid=(B, G),
        in_specs=[fin, fin, pl.BlockSpec(w2kt.shape, lambda b, g: (0, 0)),
                  pl.BlockSpec(w2v.shape, lambda b, g: (0, 0))],
        out_specs=[pl.BlockSpec((None, None, HEAD_DIM, NC), lambda b, g: (b, g, 0, 0)),
                   pl.BlockSpec((None, None, NC, HEAD_DIM), lambda b, g: (b, g, 0, 0))],
        out_shape=[jax.ShapeDtypeStruct((B, G, HEAD_DIM, NC), BF16),
                   jax.ShapeDtypeStruct((B, G, NC, HEAD_DIM), BF16)],
        compiler_params=_cp("parallel", "parallel"), name="cmp_stage2",
    )(fk, fv, w2kt, w2v)


def _head_queries(q_ref):
    return [q_ref[:, h * HEAD_DIM:(h + 1) * HEAD_DIM] for h in range(NSA_HPG)]


def _softmax_attend(q_ref, kt, bias, pv, o_ref, tq, want_psum):
    rb = SOFTMAX_ROWS
    n_blk = tq // rb
    qs = _head_queries(q_ref)
    scores = {0: _dot(qs[0], kt)}
    tot = [None] * n_blk
    for h in range(NSA_HPG):
        if h + 1 < NSA_HPG:
            scores[h + 1] = _dot(qs[h + 1], kt)
        s_h = scores.pop(h)
        blocks = []
        for b in range(n_blk):
            s = s_h[b * rb:(b + 1) * rb, :] + bias[b * rb:(b + 1) * rb, :]
            m = jnp.maximum(jnp.max(s, axis=-1, keepdims=True), 0.1 * NEG_INF)
            e = jnp.exp2(s - m)
            p = e * (1.0 / jnp.maximum(jnp.sum(e, axis=-1, keepdims=True), 1e-30))
            blocks.append(_bf(p))
            if want_psum:
                tot[b] = p if tot[b] is None else tot[b] + p
        o_ref[:, h * HEAD_DIM:(h + 1) * HEAD_DIM] = pv(jnp.concatenate(blocks, axis=0))
    return tot


def _select_blocks(imp, t0):
    nb, tq = imp.shape
    blk = lax.broadcasted_iota(jnp.int32, (nb, tq), 0)
    cur = (t0 + lax.broadcasted_iota(jnp.int32, (nb, tq), 1)) // SLC_LEN
    forced = (blk == 0) | (blk == cur) | (blk == cur - 1)
    imp = jnp.where(forced, FORCED_SCORE, imp)
    imp = jnp.where(blk > cur, -1.0, imp)
    bits = lax.bitcast_convert_type(imp, jnp.int32)
    k_sel = float(min(TOP_N, nb))
    count = lambda mask: jnp.sum(jnp.where(mask, 1.0, 0.0), axis=0, keepdims=True)

    def refine(it, thr):
        cand = thr | jnp.left_shift(jnp.int32(1), 30 - it)
        return jnp.where(count(bits >= cand) >= k_sel, cand, thr)

    thr = lax.fori_loop(0, 31, refine, jnp.zeros((1, tq), jnp.int32))
    above = bits > thr
    tied = bits == thr
    tri = (lax.broadcasted_iota(jnp.int32, (nb, nb), 0) >= lax.broadcasted_iota(jnp.int32, (nb, nb), 1))
    rank = _dot(tri.astype(BF16), jnp.where(tied, 1.0, 0.0).astype(BF16))
    return above | (tied & (rank <= k_sel - count(above)))


def _cmp_attn_body(q_ref, kt_ref, v_ref, wimp_ref, o_ref, sel_ref, imp_ref, *, tq):
    i = pl.program_id(2)
    nc = kt_ref.shape[1]
    nb = wimp_ref.shape[1]
    step = min(CMP_COLS, nc)
    n_need = (i * tq + tq - CMP_LEN) // CMP_STRIDE + 1
    for ncv in range(step, nc + 1, step):
        @pl.when((n_need > ncv - step) & ((n_need <= ncv) | (ncv == nc)))
        def _():
            t = i * tq + lax.broadcasted_iota(jnp.int32, (tq, ncv), 0)
            cmp_end = lax.broadcasted_iota(jnp.int32, (tq, ncv), 1) * CMP_STRIDE + (CMP_LEN - 1)
            bias = jnp.where(cmp_end <= t, 0.0, NEG_INF)
            v = v_ref[0:ncv, :]
            tot = _softmax_attend(q_ref, kt_ref[:, 0:ncv], bias, lambda p: _dot(p, v), o_ref, tq, True)
            wimp = wimp_ref[0:ncv, :]
            imp_ref[...] = sum(_dot(part, wimp) for part in _split_bf16(jnp.concatenate(tot, axis=0), 3))

    sel = _select_blocks(imp_ref[...].T, i * tq)
    sel_ref[...] = _bf(jnp.where(sel, 0.0, NEG_INF).T)


def _cmp_attn(q, kct, vc, wimp, tq):
    B, S, _ = q.shape
    G = NSA_GROUPS
    NC = kct.shape[3]
    NB = wimp.shape[1]
    GD = NSA_HPG * HEAD_DIM
    return pl.pallas_call(
        functools.partial(_cmp_attn_body, tq=tq), grid=(B, G, S // tq),
        in_specs=[pl.BlockSpec((None, tq, GD), lambda b, g, i: (b, i, g)),
                  pl.BlockSpec((None, None, HEAD_DIM, NC), lambda b, g, i: (b, g, 0, 0)),
                  pl.BlockSpec((None, None, NC, HEAD_DIM), lambda b, g, i: (b, g, 0, 0)),
                  pl.BlockSpec(wimp.shape, lambda b, g, i: (0, 0))],
        out_specs=[pl.BlockSpec((None, tq, GD), lambda b, g, i: (b, i, g)),
                   pl.BlockSpec((None, None, tq, NB), lambda b, g, i: (b, g, i, 0))],
        out_shape=[jax.ShapeDtypeStruct((B, S, G * GD), F32),
                   jax.ShapeDtypeStruct((B, G, S, NB), BF16)],
        scratch_shapes=[pltpu.VMEM((tq, NB), F32)],
        compiler_params=_cp("parallel", "parallel", "parallel"), name="cmp_attention_topk",
    )(q, kct, vc, wimp)


def _slc_attn_body(q_ref, selb_ref, kt_ref, vt_ref, e_ref, o_ref, lhs_ref, s0_ref, s1_ref, p0_ref, p1_ref,
                   alpha_ref, m_ref, acc_ref, *, tq, tk, wb):
    i = pl.program_id(2)
    rb = SOFTMAX_ROWS
    rows = NSA_HPG * tq
    tiles_per_window = wb * SLC_LEN // tk
    for h in range(NSA_HPG):
        lhs_ref[pl.ds(h * tq, tq), 0:HEAD_DIM] = q_ref[:, h * HEAD_DIM:(h + 1) * HEAD_DIM]
    lhs_ref[:, HEAD_DIM:LANES] = jnp.zeros((rows, LANES - HEAD_DIM), BF16)
    m_ref[...] = jnp.full(m_ref.shape, NEG_INF, F32)
    acc_ref[...] = jnp.zeros_like(acc_ref)
    s_refs = (s0_ref, s1_ref)
    p_refs = (p0_ref, p1_ref)
    p1_ref[...] = jnp.zeros((rows, tk), BF16)
    ones = jnp.ones((LANES - HEAD_DIM, tk), BF16)
    zeros = jnp.zeros((LANES - HEAD_DIM, tk), BF16)
    tile_keys = lambda j: pl.ds(pl.multiple_of(j * tk, tk), tk)

    def set_window(w):
        win = selb_ref[:, pl.ds(pl.multiple_of(w * wb, wb), wb)]
        for h in range(NSA_HPG):
            lhs_ref[pl.ds(h * tq, tq), LANES:LANES + wb] = win

    def scores(j, s_ref):
        @pl.when(j % tiles_per_window == 0)
        def _():
            set_window(j // tiles_per_window)
        ks = tile_keys(j)
        w0 = pl.multiple_of((j // tiles_per_window) * wb, wb)
        rhs = jnp.concatenate([kt_ref[:, ks], zeros, e_ref[pl.ds(w0, wb), ks]], axis=0)
        s_ref[...] = _dot(lhs_ref[...], rhs)

    def value_product(j, p_ref):
        vt = jnp.concatenate([vt_ref[:, tile_keys(j)], ones], axis=0)
        return _dot_nt(p_ref[...], vt)

    def tile(j, par, last_tile):
        if last_tile:
            t = i * tq + lax.broadcasted_iota(jnp.int32, (tq, tk), 0)
            causal = j * tk + lax.broadcasted_iota(jnp.int32, (tq, tk), 1) <= t
        else:
            scores(j + 1, s_refs[1 - par])
        pv_prev = value_product(jnp.maximum(j - 1, 0), p_refs[1 - par])
        for h in range(NSA_HPG):
            for b in range(tq // rb):
                rs = pl.ds(h * tq + b * rb, rb)
                s = s_refs[par][rs, :]
                if last_tile:
                    s = jnp.where(causal[b * rb:(b + 1) * rb, :], s, NEG_INF)
                m_old = m_ref[rs, :]
                m_new = jnp.maximum(m_old, jnp.max(s, axis=-1, keepdims=True))
                ps = [jnp.exp2(s[:, c * LANES:(c + 1) * LANES] - m_new) for c in range(tk // LANES)]
                m_ref[rs, :] = m_new
                alpha_ref[rs, :] = jnp.exp2(m_old - m_new)
                p_refs[par][rs, :] = _bf(jnp.concatenate(ps, axis=1))
        acc_ref[...] = (acc_ref[...] + pv_prev) * alpha_ref[...]

    last = (i * tq + tq - 1) // tk
    scores(0, s_refs[0])

    def two_tiles(jj, carry):
        tile(2 * jj, 0, False)
        tile(2 * jj + 1, 1, False)
        return carry

    lax.fori_loop(0, last // 2, two_tiles, 0)

    @pl.when(last % 2 == 1)
    def _():
        tile(last - 1, 0, False)

    for par in range(2):
        @pl.when(last % 2 == par)
        def _():
            tile(last, par, True)
            acc = acc_ref[...] + value_product(last, p_refs[par])
            for h in range(NSA_HPG):
                a_h = acc[h * tq:(h + 1) * tq, :]
                o_ref[:, h * HEAD_DIM:(h + 1) * HEAD_DIM] = (
                    a_h[:, :HEAD_DIM] / jnp.maximum(a_h[:, HEAD_DIM:], 1e-30))


def _slc_attn(q, selb, kvt, e, tq, tk, wb):
    B, S, _ = q.shape
    G = NSA_GROUPS
    NB = selb.shape[3]
    GD = NSA_HPG * HEAD_DIM
    rows = NSA_HPG * tq
    return pl.pallas_call(
        functools.partial(_slc_attn_body, tq=tq, tk=tk, wb=wb), grid=(B, G, S // tq),
        in_specs=[pl.BlockSpec((None, tq, GD), lambda b, g, i: (b, i, g)),
                  pl.BlockSpec((None, None, tq, NB), lambda b, g, i: (b, g, i, 0)),
                  pl.BlockSpec((None, HEAD_DIM, S), lambda b, g, i: (b, g, 0)),
                  pl.BlockSpec((None, HEAD_DIM, S), lambda b, g, i: (b, G + g, 0)),
                  pl.BlockSpec(e.shape, lambda b, g, i: (0, 0))],
        out_specs=pl.BlockSpec((None, tq, GD), lambda b, g, i: (b, i, g)),
        out_shape=jax.ShapeDtypeStruct((B, S, G * GD), F32),
        scratch_shapes=[pltpu.VMEM((rows, LANES + wb), BF16), pltpu.VMEM((rows, tk), F32), pltpu.VMEM((rows, tk), F32),
                        pltpu.VMEM((rows, tk), BF16), pltpu.VMEM((rows, tk), BF16),
                        pltpu.VMEM((rows, LANES), F32), pltpu.VMEM((rows, LANES), F32),
                        pltpu.VMEM((rows, LANES), F32)],
        compiler_params=_cp("parallel", "parallel", "parallel"), name="slc_attention",
    )(q, selb, kvt, kvt, e)


def _win_attn_body(q_ref, kt_ref, vt_ref, o_ref, *, tq, span):
    i = pl.program_id(2)
    start = pl.multiple_of(jnp.maximum(i * tq - WINDOW, 0), LANES)
    ks = pl.ds(start, span)
    t = i * tq + lax.broadcasted_iota(jnp.int32, (tq, span), 0)
    dist = t - (start + lax.broadcasted_iota(jnp.int32, (tq, span), 1))
    bias = jnp.where((dist >= 0) & (dist < WINDOW), 0.0, NEG_INF)
    vt = vt_ref[:, ks]
    _softmax_attend(q_ref, kt_ref[:, ks], bias, lambda p: _dot_nt(p, vt), o_ref, tq, False)


def _win_attn(q, kvt, tq):
    B, S, _ = q.shape
    G = NSA_GROUPS
    GD = NSA_HPG * HEAD_DIM
    span = min(tq + WINDOW, S)
    return pl.pallas_call(
        functools.partial(_win_attn_body, tq=tq, span=span), grid=(B, G, S // tq),
        in_specs=[pl.BlockSpec((None, tq, GD), lambda b, g, i: (b, i, g)),
                  pl.BlockSpec((None, HEAD_DIM, S), lambda b, g, i: (b, 2 * G + g, 0)),
                  pl.BlockSpec((None, HEAD_DIM, S), lambda b, g, i: (b, 3 * G + g, 0))],
        out_specs=pl.BlockSpec((None, tq, GD), lambda b, g, i: (b, i, g)),
        out_shape=jax.ShapeDtypeStruct((B, S, G * GD), F32),
        compiler_params=_cp("parallel", "parallel", "parallel"), name="win_attention",
    )(q, kvt, kvt)


def _nsa_out_body(oc_ref, os_ref, ow_ref, gt_ref, ge_ref, w_ref, x_ref, mod_ref, g_ref, o_ref):
    D = oc_ref.shape[1]
    ge = ge_ref[...]
    gx = sum(_dot(part, ge) for part in _split_bf16(gt_ref[...], 2))
    o = gx[:, 0:D] * oc_ref[...] + gx[:, D:2 * D] * os_ref[...] + gx[:, 2 * D:3 * D] * ow_ref[...]
    y = _dot(_bf(o), w_ref[...])
    o_ref[...] = x_ref[...] + mod_ref[2:3, :] * _rms(y, g_ref[...])


def _nsa_out(oc, osl, ow, gates, gexp, w, x, mod, g, ts):
    B, S, D = x.shape
    tok = lambda n: pl.BlockSpec((None, ts, n), lambda b, i: (b, i, 0))
    full = lambda a: pl.BlockSpec(a.shape, lambda b, i: (0,) * a.ndim)
    return pl.pallas_call(
        _nsa_out_body, grid=(B, S // ts),
        in_specs=[tok(D), tok(D), tok(D), tok(gates.shape[2]), full(gexp), full(w), tok(D),
                  pl.BlockSpec((None, 6, D), lambda b, i: (b, 0, 0)), full(g)],
        out_specs=tok(D), out_shape=jax.ShapeDtypeStruct((B, S, D), F32),
        compiler_params=_cp("parallel", "parallel"), name="nsa_out",
    )(oc, osl, ow, gates, gexp, w, x, mod, g)


def _importance_matrix(nc, nb):
    n_cmp = nc - CMP_RATIO + 1
    w = np.zeros((nc, nb), np.float32)
    for m in range(SLC_RATIO):
        for n in range(CMP_RATIO):
            off = m - n + CMP_RATIO - 1
            for j in range(nb):
                src = SLC_RATIO * j + off - (CMP_RATIO - 1)
                if 0 <= src < n_cmp:
                    w[src, j] += 1.0
    return jnp.asarray(w, BF16)


def _block_expand(nb, s):
    return jnp.asarray(np.arange(nb)[:, None] == (np.arange(s)[None, :] // SLC_LEN), BF16)


def _gate_expand(n_in):
    hd = NSA_HEADS * HEAD_DIM
    w = np.zeros((n_in, 3 * hd), np.float32)
    for head in range(NSA_HEADS):
        for br in range(3):
            w[head * 3 + br, br * hd + head * HEAD_DIM: br * hd + (head + 1) * HEAD_DIM] = 1.0
    return jnp.asarray(w, BF16)


def kernel(x, c, ada_w, ada_b, norm_g, mlp_up, mlp_down, rw_mu, rw_r, rw_k, rw_v, rw_o, rw_w0, rw_wa, rw_wb,
           rw_a0, rw_aa, rw_ab, rw_ga, rw_gb, rw_kk, rw_ka, rw_rk, rw_lng, rw_lnb, kv_norm_g, kv_ada_w,
           kv_ada_b, kv_w, cmp_pos_k, cmp_pos_v, cmp_k_w1, cmp_k_w2, cmp_v_w1, cmp_v_w2, nsa_wqg, nsa_wo):
    B, S, D = x.shape
    depth = ada_w.shape[0]
    n_a = rw_mu.shape[0]
    bf = _bf
    row = lambda a: a.reshape(1, -1)
    ts = min(512, S)

    c8 = jnp.pad(c, ((0, 8 - B), (0, 0)))
    mods = _ada(c8, ada_w, ada_b)[:, :B].reshape(depth, B, 6, D)
    mkv = _ada(c8, kv_ada_w[None], kv_ada_b[None])[0, :B].reshape(B, 2, D)

    GD = NSA_GROUPS * HEAD_DIM
    HD = NSA_HEADS * HEAD_DIM
    shared = None
    for layer in range(depth):
        mod = mods[layer]
        ng = norm_g[layer]
        if layer < n_a:
            i = layer
            r, k, v, lw, kk, a, g = _rw_front(
                x, mod, row(ng[0]), rw_mu[i], bf(rw_r[i]), bf(rw_k[i]), bf(rw_v[i]), bf(rw_wa[i]),
                bf(rw_wb[i]), row(rw_w0[i]), bf(rw_aa[i]), bf(rw_ab[i]), row(rw_a0[i]), bf(rw_ga[i]),
                bf(rw_gb[i]), row(rw_kk[i]), row(rw_ka[i]), min(256, S))
            y = _rw_recurrence(r, k, v, lw, kk, a, g, row(rw_rk[i]), row(rw_lng[i]), row(rw_lnb[i]),
                               min(1024, S))
            x = _proj_res(y, bf(rw_o[i]), x, mod, row(ng[1]), ts)
        else:
            j = layer - n_a
            wqg = nsa_wqg[j]
            wq = bf(wqg[:, :HD])
            wg = bf(jnp.pad(wqg[:, HD:], ((0, 0), (0, LANES - 3 * NSA_HEADS))))
            if shared is None:
                wc = bf(kv_w[:, :2 * GD])
                wt = bf(kv_w[:, 2 * GD:].T)
            q, gates, kvc, kvt = _nsa_proj(x, mod, mkv, row(ng[0]), row(kv_norm_g), wq, wg, wc, wt, ts)
            if shared is None:
                fk, fv = _cmp1(kvc, cmp_pos_k, cmp_pos_v, bf(cmp_k_w1), bf(cmp_v_w1), min(1024, S))
                kct, vc = _cmp2(fk, fv, bf(cmp_k_w2.T), bf(cmp_v_w2))
                shared = (kct, vc, kvt)
            kct, vc, kvt_s = shared
            nc, nb = S // CMP_STRIDE, S // SLC_LEN
            tq = min(256, S)
            oc, sel = _cmp_attn(q, kct, vc, _importance_matrix(nc, nb), tq)
            osl = _slc_attn(q, sel, kvt_s, _block_expand(nb, S), min(512, S), min(512, S), min(LANES, nb))
            ow = _win_attn(q, kvt_s, min(256, S))
            x = _nsa_out(oc, osl, ow, gates, _gate_expand(gates.shape[2]), bf(nsa_wo[j]), x, mod,
                         row(ng[1]), ts)
        x = _mlp(x, mod, row(ng[2]), row(ng[3]), bf(mlp_up[layer]), bf(mlp_down[layer]), min(1024, S),
                 1024)
    return x
```

```python
import functools
import math

import numpy as np
import jax
import jax.numpy as jnp
from jax import lax
from jax.experimental import pallas as pl
from jax.experimental.pallas import tpu as pltpu

F32 = jnp.float32
BF16 = jnp.bfloat16
HI = lax.Precision.HIGHEST

NORM_EPS = 1e-6
GN_EPS = 64e-5
RWKV_HEAD = 64
HEAD_DIM = 64
NSA_GROUPS = 4
NSA_HPG = 4
NSA_HEADS = 16
CMP_LEN = 32
CMP_STRIDE = 16
CMP_RATIO = CMP_LEN // CMP_STRIDE
SLC_LEN = 64
SLC_RATIO = SLC_LEN // CMP_STRIDE
TOP_N = 16
WINDOW = 512
FORCED_SCORE = 1e4
NEG_INF = -1e30
LOG2E = math.log2(math.e)

LANES = 128
VMEM_LIMIT = 56 * 1024 * 1024
REC_CHUNK = 64
SOFTMAX_ROWS = 32
CMP_COLS = 256


def _cp(*sem):
    return pltpu.CompilerParams(dimension_semantics=sem, vmem_limit_bytes=VMEM_LIMIT)


def _dot(a, b, **kw):
    return jnp.dot(a, b, preferred_element_type=F32, **kw)


def _dot_nt(a, b, **kw):
    return lax.dot_general(a, b, (((1,), (1,)), ((), ())), preferred_element_type=F32, **kw)


def _dot_tn(a, b, **kw):
    return lax.dot_general(a, b, (((0,), (0,)), ((), ())), preferred_element_type=F32, **kw)


def _bf(t):
    return t.astype(BF16)


def _split_bf16(x, parts):
    out = []
    for _ in range(parts - 1):
        hi = _bf(x)
        out.append(hi)
        x = x - hi.astype(F32)
    out.append(_bf(x))
    return out


def _rms(x, g):
    return x * lax.rsqrt(jnp.mean(x * x, axis=-1, keepdims=True) + NORM_EPS) * g


def _ada_body(c_ref, w_ref, b_ref, o_ref):
    c = c_ref[...]
    o_ref[...] = _dot(c * jax.nn.sigmoid(c), w_ref[...], precision=HI) + b_ref[...]


def _ada(c8, w, b):
    L, D, N = w.shape
    tn = 1024
    return pl.pallas_call(
        _ada_body, grid=(L, N // tn),
        in_specs=[pl.BlockSpec((8, D), lambda l, j: (0, 0)),
                  pl.BlockSpec((None, D, tn), lambda l, j: (l, 0, j)),
                  pl.BlockSpec((None, 1, tn), lambda l, j: (l, 0, j))],
        out_specs=pl.BlockSpec((None, 8, tn), lambda l, j: (l, 0, j)),
        out_shape=jax.ShapeDtypeStruct((L, 8, N), F32),
        compiler_params=_cp("parallel", "parallel"), name="ada",
    )(c8, w, b.reshape(L, 1, N))


def _rw_front_body(x_ref, xp_ref, mod_ref, g_ref, mu_ref, wr, wk, wv, wa, wb, w0, aa, ab, a0, ga, gb,
                   kk_ref, ka_ref, r_o, k_o, v_o, lw_o, kk_o, a_o, g_o):
    i = pl.program_id(1)
    g = g_ref[...]
    sh, sc = mod_ref[0:1, :], mod_ref[1:2, :]
    h = _rms(x_ref[...], g) * (1 + sc) + sh
    hp = _rms(xp_ref[...], g) * (1 + sc) + sh
    hp = jnp.where(i > 0, hp[7:8, :], 0.0)
    row = lax.broadcasted_iota(jnp.int32, h.shape, 0)
    hs = jnp.where(row == 0, hp, pltpu.roll(h, 1, axis=0))
    xx = hs - h
    mu = mu_ref[...]
    xr, xw, xk, xv, xa, xg = [_bf(h + xx * mu[j:j + 1, :]) for j in range(6)]
    r = _dot(xr, wr[...])
    k = _dot(xk, wk[...])
    v = _dot(xv, wv[...])
    wl = w0[...] + _dot(_bf(jnp.tanh(_dot(xw, wa[...]))), wb[...])
    lw = -jax.nn.sigmoid(wl) * math.exp(-0.5)
    a = jax.nn.sigmoid(a0[...] + _dot(_bf(_dot(xa, aa[...])), ab[...]))
    gate = _dot(_bf(jax.nn.sigmoid(_dot(xg, ga[...]))), gb[...])
    r_o[...] = r
    k_o[...] = k * (1 + (a - 1) * ka_ref[...])
    v_o[...] = v
    lw_o[...] = lw
    kk_o[...] = k * kk_ref[...]
    a_o[...] = a
    g_o[...] = gate


def _rw_front(x, mod, g, mu, wr, wk, wv, wa, wb, w0, aa, ab, a0, ga, gb, kk, ka, ts):
    B, S, D = x.shape
    full = lambda a: pl.BlockSpec(a.shape, lambda b, i: (0,) * a.ndim)
    tok = pl.BlockSpec((None, ts, D), lambda b, i: (b, i, 0))
    ws = [wr, wk, wv, wa, wb, w0, aa, ab, a0, ga, gb, kk, ka]
    return pl.pallas_call(
        _rw_front_body, grid=(B, S // ts),
        in_specs=[tok,
                  pl.BlockSpec((None, 8, D), lambda b, i: (b, jnp.maximum(i * (ts // 8) - 1, 0), 0)),
                  pl.BlockSpec((None, 6, D), lambda b, i: (b, 0, 0)),
                  full(g), full(mu)] + [full(w) for w in ws],
        out_specs=[tok] * 7,
        out_shape=[jax.ShapeDtypeStruct((B, S, D), F32)] * 7,
        compiler_params=_cp("parallel", "parallel"), name="rwkv_front",
    )(x, x, mod, g, mu, *ws)


def _rec_body(r_ref, k_ref, v_ref, lw_ref, kk_ref, a_ref, g_ref, rk_ref, lng_ref, lnb_ref, o_ref, st_ref,
              *, n_chunks, n_pairs):
    C = REC_CHUNK
    L = LANES
    C2 = 2 * C

    @pl.when(pl.program_id(2) == 0)
    def _():
        st_ref[...] = jnp.zeros_like(st_ref)

    h0 = lax.broadcasted_iota(jnp.int32, (C, L), 1) < RWKV_HEAD
    rr = lax.broadcasted_iota(jnp.int32, (C2, C2), 0)
    cc = lax.broadcasted_iota(jnp.int32, (C2, C2), 1)
    same = rr // C == cc // C
    incl = same & (rr >= cc)
    strict = same & (rr > cc)
    eye = (rr == cc).astype(F32)
    lvl_masks = [(rr // sz == cc // sz) & (rr // (sz // 2) != cc // (sz // 2))
                 for sz in (2 ** e for e in range(1, int(math.log2(C)) + 1))]
    tri_b = (lax.broadcasted_iota(jnp.int32, (C, C), 0) >= lax.broadcasted_iota(jnp.int32, (C, C), 1)).astype(BF16)
    bd_b = (lax.broadcasted_iota(jnp.int32, (L, L), 0) // RWKV_HEAD
            == lax.broadcasted_iota(jnp.int32, (L, L), 1) // RWKV_HEAD).astype(BF16)
    units = [(pr, c) for pr in range(n_pairs) for c in range(n_chunks)]

    def head_sum(t):
        return sum(_dot(p, bd_b) for p in _split_bf16(t, 2))

    def stack(t):
        zero = jnp.zeros_like(t)
        return jnp.concatenate([jnp.where(h0, t, zero), jnp.where(h0, zero, t)], axis=0)

    sl = lambda ref, pr, c: ref[pl.ds(c * C, C), pr * L:(pr + 1) * L]
    ld = [dict(r=sl(r_ref, pr, c), k=sl(k_ref, pr, c), v=sl(v_ref, pr, c), lw=sl(lw_ref, pr, c),
               kk=sl(kk_ref, pr, c), a=sl(a_ref, pr, c)) for pr, c in units]
    kkn = [d["kk"] / jnp.maximum(jnp.sqrt(head_sum(d["kk"] * d["kk"])), 1e-12) for d in ld]
    cl = [sum(_dot(tri_b, p) for p in _split_bf16(d["lw"], 3)) for d in ld]
    cl_end = [x[C - 1:C, :] for x in cl]
    e_neg = [jnp.exp(-x) for x in cl]
    e_end = [jnp.exp(ce - x) for x, ce in zip(cl, cl_end)]
    rt2 = [_bf(stack(d["r"] * jnp.exp(x))) for d, x in zip(ld, cl)]
    at2 = [_bf(stack(-kn * jnp.exp(x - d["lw"]))) for d, kn, x in zip(ld, kkn, cl)]
    bt2 = [_bf(stack(kn * d["a"] * e)) for d, kn, e in zip(ld, kkn, e_neg)]
    kt2 = [_bf(stack(d["k"] * e)) for d, e in zip(ld, e_neg)]
    bend2 = [_bf(stack(kn * d["a"] * e)) for d, kn, e in zip(ld, kkn, e_end)]
    kend2 = [_bf(stack(d["k"] * e)) for d, e in zip(ld, e_end)]
    v2 = [_bf(stack(d["v"])) for d in ld]
    decay = [jnp.exp(x) for x in cl_end]

    gm = [_dot_nt(jnp.concatenate([a_, r_], axis=0), jnp.concatenate([b_, k_], axis=0))
          for a_, r_, b_, k_ in zip(at2, rt2, bt2, kt2)]
    a_ab = [jnp.where(strict, g_[:C2, :C2], 0.0) for g_ in gm]
    a_ak = [_bf(jnp.where(strict, g_[:C2, C2:], 0.0)) for g_ in gm]
    a_rb = [_bf(jnp.where(incl, g_[C2:, :C2], 0.0)) for g_ in gm]
    a_rk = [_bf(jnp.where(incl, g_[C2:, C2:], 0.0)) for g_ in gm]

    tm = [eye + jnp.where(lvl_masks[0], a_, 0.0) for a_ in a_ab]
    for lm in lvl_masks[1:]:
        off = [_bf(jnp.where(lm, a_, 0.0)) for a_ in a_ab]
        half = [_bf(_dot(_bf(t_), o_)) for t_, o_ in zip(tm, off)]
        tm = [t_ + _dot(h_, _bf(t_)) for t_, h_ in zip(tm, half)]
    tm = [_bf(t_) for t_ in tm]

    w2 = [_bf(_dot(t_, a_)) for t_, a_ in zip(tm, at2)]
    akv = [_bf(_dot(a_, v_)) for a_, v_ in zip(a_ak, v2)]
    uh2 = [_bf(_dot(t_, x_)) for t_, x_ in zip(tm, akv)]
    rw2 = [_bf(r_.astype(F32) + _dot(a_, w_)) for r_, a_, w_ in zip(rt2, a_rb, w2)]
    yh2 = [_dot(a_, u_) + _dot(b_, v_) for a_, u_, b_, v_ in zip(a_rb, uh2, a_rk, v2)]
    ec = [_bf(_dot_tn(w_, b_)) for w_, b_ in zip(w2, bend2)]
    qc = [_dot_tn(jnp.concatenate([u_, v_], axis=0), jnp.concatenate([b_, k_], axis=0))
          for u_, v_, b_, k_ in zip(uh2, v2, bend2, kend2)]

    states = [st_ref[pr] for pr in range(n_pairs)]
    y2 = [None] * len(units)
    for c in range(n_chunks):
        for pr in range(n_pairs):
            u = pr * n_chunks + c
            n = states[pr]
            n_b = _bf(n)
            y2[u] = _dot_nt(rw2[u], n_b) + yh2[u]
            states[pr] = n * decay[u] + _dot(n_b, ec[u]) + qc[u]
    for pr in range(n_pairs):
        st_ref[pr] = states[pr]

    for u, (pr, c) in enumerate(units):
        d = ld[u]
        lanes = slice(pr * L, (pr + 1) * L)
        rk, lng, lnb = rk_ref[:, lanes], lng_ref[:, lanes], lnb_ref[:, lanes]
        y = y2[u][:C] + y2[u][C:]
        mean = head_sum(y) * (1.0 / RWKV_HEAD)
        yc = y - mean
        var = head_sum(yc * yc) * (1.0 / RWKV_HEAD)
        yn = yc * lax.rsqrt(var + GN_EPS) * lng + lnb
        bonus = head_sum(d["r"] * d["k"] * rk) * d["v"]
        o_ref[pl.ds(c * C, C), lanes] = (yn + bonus) * sl(g_ref, pr, c)


def _rw_recurrence(r, k, v, lw, kk, a, g, rk, lng, lnb, tt, n_pairs):
    B, S, D = r.shape
    width = n_pairs * LANES
    tok = pl.BlockSpec((None, tt, width), lambda b, hp, t: (b, t, hp))
    vec = pl.BlockSpec((1, width), lambda b, hp, t: (0, hp))
    return pl.pallas_call(
        functools.partial(_rec_body, n_chunks=tt // REC_CHUNK, n_pairs=n_pairs),
        grid=(B, D // width, S // tt),
        in_specs=[tok] * 7 + [vec] * 3,
        out_specs=tok,
        out_shape=jax.ShapeDtypeStruct((B, S, D), F32),
        scratch_shapes=[pltpu.VMEM((n_pairs, LANES, LANES), F32)],
        compiler_params=_cp("parallel", "parallel", "arbitrary"), name="rwkv_recurrence",
    )(r, k, v, lw, kk, a, g, rk, lng, lnb)


def _proj_res_body(y_ref, w_ref, x_ref, mod_ref, g_ref, o_ref):
    y = _dot(_bf(y_ref[...]), w_ref[...])
    o_ref[...] = x_ref[...] + mod_ref[2:3, :] * _rms(y, g_ref[...])


def _proj_res(y, w, x, mod, g, ts):
    B, S, D = x.shape
    tok = pl.BlockSpec((None, ts, D), lambda b, i: (b, i, 0))
    return pl.pallas_call(
        _proj_res_body, grid=(B, S // ts),
        in_specs=[tok, pl.BlockSpec(w.shape, lambda b, i: (0, 0)), tok,
                  pl.BlockSpec((None, 6, D), lambda b, i: (b, 0, 0)),
                  pl.BlockSpec(g.shape, lambda b, i: (0, 0))],
        out_specs=tok, out_shape=jax.ShapeDtypeStruct((B, S, D), F32),
        compiler_params=_cp("parallel", "parallel"), name="proj_residual",
    )(y, w, x, mod, g)


def _mlp_body(x_ref, mod_ref, g2_ref, g3_ref, up_ref, dn_ref, o_ref, h_ref, acc_ref):
    f = pl.program_id(2)

    @pl.when(f == 0)
    def _():
        h = _rms(x_ref[...], g2_ref[...]) * (1 + mod_ref[4:5, :]) + mod_ref[3:4, :]
        h_ref[...] = _bf(h)
        acc_ref[...] = jnp.zeros_like(acc_ref)

    z = jnp.maximum(_dot(h_ref[...], up_ref[...]), 0.0)
    acc_ref[...] += _dot(_bf(z * z), dn_ref[...])

    @pl.when(f == pl.num_programs(2) - 1)
    def _():
        o_ref[...] = x_ref[...] + mod_ref[5:6, :] * _rms(acc_ref[...], g3_ref[...])


def _mlp(x, mod, g2, g3, up, dn, ts, tf):
    B, S, D = x.shape
    F = up.shape[1]
    tok = pl.BlockSpec((None, ts, D), lambda b, i, f: (b, i, 0))
    vec = pl.BlockSpec((1, D), lambda b, i, f: (0, 0))
    return pl.pallas_call(
        _mlp_body, grid=(B, S // ts, F // tf),
        in_specs=[tok, pl.BlockSpec((None, 6, D), lambda b, i, f: (b, 0, 0)), vec, vec,
                  pl.BlockSpec((D, tf), lambda b, i, f: (0, f)),
                  pl.BlockSpec((tf, D), lambda b, i, f: (f, 0))],
        out_specs=tok, out_shape=jax.ShapeDtypeStruct((B, S, D), F32),
        scratch_shapes=[pltpu.VMEM((ts, D), BF16), pltpu.VMEM((ts, D), F32)],
        compiler_params=_cp("parallel", "parallel", "arbitrary"), name="mlp",
    )(x, mod, g2, g3, up, dn)


def _nsa_proj_body(x_ref, mod_ref, mkv_ref, g_ref, gkv_ref, wq_ref, wg_ref, wc_ref, wt_ref,
                   q_o, gt_o, c_o, t_o):
    x = x_ref[...]
    xn = x * lax.rsqrt(jnp.mean(x * x, axis=-1, keepdims=True) + NORM_EPS)
    h = _bf((xn * g_ref[...]) * (1 + mod_ref[1:2, :]) + mod_ref[0:1, :])
    hkv = _bf((xn * gkv_ref[...]) * (1 + mkv_ref[1:2, :]) + mkv_ref[0:1, :])
    q_o[...] = _bf(_dot(h, wq_ref[...]) * (HEAD_DIM ** -0.5 * LOG2E))
    gt_o[...] = jax.nn.sigmoid(_dot(h, wg_ref[...]))
    c_o[...] = _dot(hkv, wc_ref[...])
    t_o[...] = _bf(_dot_nt(wt_ref[...], hkv))


def _nsa_proj(x, mod, mkv, g, gkv, wq, wg, wc, wt, ts):
    B, S, D = x.shape
    tok = lambda n: pl.BlockSpec((None, ts, n), lambda b, i: (b, i, 0))
    full = lambda a: pl.BlockSpec(a.shape, lambda b, i: (0,) * a.ndim)
    return pl.pallas_call(
        _nsa_proj_body, grid=(B, S // ts),
        in_specs=[tok(D), pl.BlockSpec((None, 6, D), lambda b, i: (b, 0, 0)),
                  pl.BlockSpec((None, 2, D), lambda b, i: (b, 0, 0)),
                  full(g), full(gkv), full(wq), full(wg), full(wc), full(wt)],
        out_specs=[tok(wq.shape[1]), tok(wg.shape[1]), tok(wc.shape[1]),
                   pl.BlockSpec((None, wt.shape[0], ts), lambda b, i: (b, 0, i))],
        out_shape=[jax.ShapeDtypeStruct((B, S, wq.shape[1]), BF16),
                   jax.ShapeDtypeStruct((B, S, wg.shape[1]), F32),
                   jax.ShapeDtypeStruct((B, S, wc.shape[1]), F32),
                   jax.ShapeDtypeStruct((B, wt.shape[0], S), BF16)],
        compiler_params=_cp("parallel", "parallel"), name="nsa_proj",
    )(x, mod, mkv, g, gkv, wq, wg, wc, wt)


def _cmp1_body(k0_ref, k1_ref, v0_ref, v1_ref, pk_ref, pv_ref, wk_ref, wv_ref, fk_o, fv_o, *, n_chunk):
    hid = wk_ref.shape[1]
    gpr = LANES // HEAD_DIM
    for srcs, pos, w1, out in (((k0_ref, k1_ref), pk_ref, wk_ref, fk_o), ((v0_ref, v1_ref), pv_ref, wv_ref, fv_o)):
        acc = [[jnp.zeros((n_chunk, hid), F32) for _ in range(CMP_RATIO)] for _ in range(NSA_GROUPS)]
        for l in range(CMP_STRIDE):
            for si, src in enumerate(srcs):
                rows = src[pl.ds(l, n_chunk, stride=CMP_STRIDE), :]
                for gi in range(gpr):
                    g = si * gpr + gi
                    t = rows[:, gi * HEAD_DIM:(gi + 1) * HEAD_DIM]
                    for half in range(CMP_RATIO):
                        p = half * CMP_STRIDE + l
                        lhs = _bf(t + pos[p:p + 1, :])
                        acc[g][half] = acc[g][half] + _dot(lhs, w1[p * HEAD_DIM:(p + 1) * HEAD_DIM, :])
        for g in range(NSA_GROUPS):
            for half in range(CMP_RATIO):
                out[g, :, half * hid:(half + 1) * hid] = acc[g][half]


def _cmp1(kvc, pos_k, pos_v, w1k, w1v, ts):
    B, S, _ = kvc.shape
    G = NSA_GROUPS
    hid = w1k.shape[1]
    nch = ts // CMP_STRIDE
    full = lambda a: pl.BlockSpec(a.shape, lambda b, i: (0,) * a.ndim)
    src = lambda j: pl.BlockSpec((None, ts, LANES), lambda b, i: (b, i, j))
    outs = pl.BlockSpec((None, G, nch, CMP_RATIO * hid), lambda b, i: (b, 0, i, 0))
    return pl.pallas_call(
        functools.partial(_cmp1_body, n_chunk=nch), grid=(B, S // ts),
        in_specs=[src(0), src(1), src(2), src(3), full(pos_k), full(pos_v), full(w1k), full(w1v)],
        out_specs=[outs, outs],
        out_shape=[jax.ShapeDtypeStruct((B, G, S // CMP_STRIDE, CMP_RATIO * hid), F32)] * 2,
        compiler_params=_cp("parallel", "parallel"), name="cmp_stage1",
    )(kvc, kvc, kvc, kvc, pos_k, pos_v, w1k, w1v)


def _gelu_tanh(x):
    return 0.5 * x * (1.0 + jnp.tanh(math.sqrt(2.0 / math.pi) * (x + 0.044715 * (x * x * x))))


def _cmp2_body(fk_ref, fv_ref, w2kt_ref, w2v_ref, kt_o, v_o):
    nc = fk_ref.shape[0]
    hid = w2v_ref.shape[0]
    row = lax.broadcasted_iota(jnp.int32, (nc, hid), 0)

    def hidden(f):
        nxt = jnp.where(row == nc - 1, 0.0, pltpu.roll(f[:, hid:2 * hid], nc - 1, axis=0))
        return _bf(_gelu_tanh(f[:, 0:hid] + nxt))

    kt_o[...] = _bf(_dot_nt(w2kt_ref[...], hidden(fk_ref[...])))
    v_o[...] = _bf(_dot(hidden(fv_ref[...]), w2v_ref[...]))


def _cmp2(fk, fv, w2kt, w2v):
    B, G, NC, H2 = fk.shape
    fin = pl.BlockSpec((None, None, NC, H2), lambda b, g: (b, g, 0, 0))
    return pl.pallas_call(
        _cmp2_body, grid=(B, G),
        in_specs=[fin, fin, pl.BlockSpec(w2kt.shape, lambda b, g: (0, 0)),
                  pl.BlockSpec(w2v.shape, lambda b, g: (0, 0))],
        out_specs=[pl.BlockSpec((None, None, HEAD_DIM, NC), lambda b, g: (b, g, 0, 0)),
                   pl.BlockSpec((None, None, NC, HEAD_DIM), lambda b, g: (b, g, 0, 0))],
        out_shape=[jax.ShapeDtypeStruct((B, G, HEAD_DIM, NC), BF16),
                   jax.ShapeDtypeStruct((B, G, NC, HEAD_DIM), BF16)],
        compiler_params=_cp("parallel", "parallel"), name="cmp_stage2",
    )(fk, fv, w2kt, w2v)


def _head_queries(q_ref):
    return [q_ref[:, h * HEAD_DIM:(h + 1) * HEAD_DIM] for h in range(NSA_HPG)]


def _softmax_attend(q_ref, kt, bias, pv, o_ref, tq, want_psum):
    rb = SOFTMAX_ROWS
    n_blk = tq // rb
    qs = _head_queries(q_ref)
    scores = {0: _dot(qs[0], kt)}
    tot = [None] * n_blk
    for h in range(NSA_HPG):
        if h + 1 < NSA_HPG:
            scores[h + 1] = _dot(qs[h + 1], kt)
        s_h = scores.pop(h)
        blocks = []
        for b in range(n_blk):
            s = s_h[b * rb:(b + 1) * rb, :] + bias[b * rb:(b + 1) * rb, :]
            m = jnp.maximum(jnp.max(s, axis=-1, keepdims=True), 0.1 * NEG_INF)
            e = jnp.exp2(s - m)
            p = e * (1.0 / jnp.maximum(jnp.sum(e, axis=-1, keepdims=True), 1e-30))
            blocks.append(_bf(p))
            if want_psum:
                tot[b] = p if tot[b] is None else tot[b] + p
        o_ref[:, h * HEAD_DIM:(h + 1) * HEAD_DIM] = pv(jnp.concatenate(blocks, axis=0))
    return tot


def _select_blocks(imp, t0):
    nb, tq = imp.shape
    blk = lax.broadcasted_iota(jnp.int32, (nb, tq), 0)
    cur = (t0 + lax.broadcasted_iota(jnp.int32, (nb, tq), 1)) // SLC_LEN
    forced = (blk == 0) | (blk == cur) | (blk == cur - 1)
    imp = jnp.where(forced, FORCED_SCORE, imp)
    imp = jnp.where(blk > cur, -1.0, imp)
    bits = lax.bitcast_convert_type(imp, jnp.int32)
    k_sel = float(min(TOP_N, nb))
    count = lambda mask: jnp.sum(jnp.where(mask, 1.0, 0.0), axis=0, keepdims=True)

    def refine(it, thr):
        cand = thr | jnp.left_shift(jnp.int32(1), 30 - it)
        return jnp.where(count(bits >= cand) >= k_sel, cand, thr)

    thr = lax.fori_loop(0, 31, refine, jnp.zeros((1, tq), jnp.int32))
    above = bits > thr
    tied = bits == thr
    tri = (lax.broadcasted_iota(jnp.int32, (nb, nb), 0) >= lax.broadcasted_iota(jnp.int32, (nb, nb), 1))
    rank = _dot(tri.astype(BF16), jnp.where(tied, 1.0, 0.0).astype(BF16))
    return above | (tied & (rank <= k_sel - count(above)))


def _cmp_attn_body(q_ref, kt_ref, v_ref, wimp_ref, o_ref, sel_ref, imp_ref, *, tq):
    i = pl.program_id(2)
    nc = kt_ref.shape[1]
    nb = wimp_ref.shape[1]
    step = min(CMP_COLS, nc)
    n_need = (i * tq + tq - CMP_LEN) // CMP_STRIDE + 1
    for ncv in range(step, nc + 1, step):
        @pl.when((n_need > ncv - step) & ((n_need <= ncv) | (ncv == nc)))
        def _():
            t = i * tq + lax.broadcasted_iota(jnp.int32, (tq, ncv), 0)
            cmp_end = lax.broadcasted_iota(jnp.int32, (tq, ncv), 1) * CMP_STRIDE + (CMP_LEN - 1)
            bias = jnp.where(cmp_end <= t, 0.0, NEG_INF)
            v = v_ref[0:ncv, :]
            tot = _softmax_attend(q_ref, kt_ref[:, 0:ncv], bias, lambda p: _dot(p, v), o_ref, tq, True)
            wimp = wimp_ref[0:ncv, :]
            imp_ref[...] = sum(_dot(part, wimp) for part in _split_bf16(jnp.concatenate(tot, axis=0), 3))

    sel = _select_blocks(imp_ref[...].T, i * tq)
    sel_ref[...] = _bf(jnp.where(sel, 0.0, NEG_INF).T)


def _cmp_attn(q, kct, vc, wimp, tq):
    B, S, _ = q.shape
    G = NSA_GROUPS
    NC = kct.shape[3]
    NB = wimp.shape[1]
    GD = NSA_HPG * HEAD_DIM
    return pl.pallas_call(
        functools.partial(_cmp_attn_body, tq=tq), grid=(B, G, S // tq),
        in_specs=[pl.BlockSpec((None, tq, GD), lambda b, g, i: (b, i, g)),
                  pl.BlockSpec((None, None, HEAD_DIM, NC), lambda b, g, i: (b, g, 0, 0)),
                  pl.BlockSpec((None, None, NC, HEAD_DIM), lambda b, g, i: (b, g, 0, 0)),
                  pl.BlockSpec(wimp.shape, lambda b, g, i: (0, 0))],
        out_specs=[pl.BlockSpec((None, tq, GD), lambda b, g, i: (b, i, g)),
                   pl.BlockSpec((None, None, tq, NB), lambda b, g, i: (b, g, i, 0))],
        out_shape=[jax.ShapeDtypeStruct((B, S, G * GD), F32),
                   jax.ShapeDtypeStruct((B, G, S, NB), BF16)],
        scratch_shapes=[pltpu.VMEM((tq, NB), F32)],
        compiler_params=_cp("parallel", "parallel", "parallel"), name="cmp_attention_topk",
    )(q, kct, vc, wimp)


def _slc_attn_body(q_ref, selb_ref, kt_ref, vt_ref, e_ref, o_ref, lhs_ref, s0_ref, s1_ref, p0_ref, p1_ref,
                   alpha_ref, m_ref, acc_ref, *, tq, tk, wb):
    i = pl.program_id(2)
    rb = SOFTMAX_ROWS
    rows = NSA_HPG * tq
    tiles_per_window = wb * SLC_LEN // tk
    for h in range(NSA_HPG):
        lhs_ref[pl.ds(h * tq, tq), 0:HEAD_DIM] = q_ref[:, h * HEAD_DIM:(h + 1) * HEAD_DIM]
    lhs_ref[:, HEAD_DIM:LANES] = jnp.zeros((rows, LANES - HEAD_DIM), BF16)
    m_ref[...] = jnp.full(m_ref.shape, NEG_INF, F32)
    acc_ref[...] = jnp.zeros_like(acc_ref)
    s_refs = (s0_ref, s1_ref)
    p_refs = (p0_ref, p1_ref)
    p1_ref[...] = jnp.zeros((rows, tk), BF16)
    ones = jnp.ones((LANES - HEAD_DIM, tk), BF16)
    zeros = jnp.zeros((LANES - HEAD_DIM, tk), BF16)
    tile_keys = lambda j: pl.ds(pl.multiple_of(j * tk, tk), tk)

    def set_window(w):
        win = selb_ref[:, pl.ds(pl.multiple_of(w * wb, wb), wb)]
        for h in range(NSA_HPG):
            lhs_ref[pl.ds(h * tq, tq), LANES:LANES + wb] = win

    def scores(j, s_ref):
        @pl.when(j % tiles_per_window == 0)
        def _():
            set_window(j // tiles_per_window)
        ks = tile_keys(j)
        w0 = pl.multiple_of((j // tiles_per_window) * wb, wb)
        rhs = jnp.concatenate([kt_ref[:, ks], zeros, e_ref[pl.ds(w0, wb), ks]], axis=0)
        s_ref[...] = _dot(lhs_ref[...], rhs)

    def value_product(j, p_ref):
        vt = jnp.concatenate([vt_ref[:, tile_keys(j)], ones], axis=0)
        return _dot_nt(p_ref[...], vt)

    def tile(j, par, last_tile):
        if last_tile:
            t = i * tq + lax.broadcasted_iota(jnp.int32, (tq, tk), 0)
            causal = j * tk + lax.broadcasted_iota(jnp.int32, (tq, tk), 1) <= t
        else:
            scores(j + 1, s_refs[1 - par])
        pv_prev = value_product(jnp.maximum(j - 1, 0), p_refs[1 - par])
        for h in range(NSA_HPG):
            for b in range(tq // rb):
                rs = pl.ds(h * tq + b * rb, rb)
                s = s_refs[par][rs, :]
                if last_tile:
                    s = jnp.where(causal[b * rb:(b + 1) * rb, :], s, NEG_INF)
                m_old = m_ref[rs, :]
                m_new = jnp.maximum(m_old, jnp.max(s, axis=-1, keepdims=True))
                ps = [jnp.exp2(_bf(s[:, c * LANES:(c + 1) * LANES] - m_new)) for c in range(tk // LANES)]
                m_ref[rs, :] = m_new
                alpha_ref[rs, :] = jnp.exp2(m_old - m_new)
                p_refs[par][rs, :] = jnp.concatenate(ps, axis=1)
        acc_ref[...] = (acc_ref[...] + pv_prev) * alpha_ref[...]

    last = (i * tq + tq - 1) // tk
    scores(0, s_refs[0])

    def two_tiles(jj, carry):
        tile(2 * jj, 0, False)
        tile(2 * jj + 1, 1, False)
        return carry

    lax.fori_loop(0, last // 2, two_tiles, 0)

    @pl.when(last % 2 == 1)
    def _():
        tile(last - 1, 0, False)

    for par in range(2):
        @pl.when(last % 2 == par)
        def _():
            tile(last, par, True)
            acc = acc_ref[...] + value_product(last, p_refs[par])
            for h in range(NSA_HPG):
                a_h = acc[h * tq:(h + 1) * tq, :]
                o_ref[:, h * HEAD_DIM:(h + 1) * HEAD_DIM] = (
                    a_h[:, :HEAD_DIM] / jnp.maximum(a_h[:, HEAD_DIM:], 1e-30))


def _slc_attn(q, selb, kvt, e, tq, tk, wb):
    B, S, _ = q.shape
    G = NSA_GROUPS
    NB = selb.shape[3]
    GD = NSA_HPG * HEAD_DIM
    rows = NSA_HPG * tq
    return pl.pallas_call(
        functools.partial(_slc_attn_body, tq=tq, tk=tk, wb=wb), grid=(B, G, S // tq),
        in_specs=[pl.BlockSpec((None, tq, GD), lambda b, g, i: (b, i, g)),
                  pl.BlockSpec((None, None, tq, NB), lambda b, g, i: (b, g, i, 0)),
                  pl.BlockSpec((None, HEAD_DIM, S), lambda b, g, i: (b, g, 0)),
                  pl.BlockSpec((None, HEAD_DIM, S), lambda b, g, i: (b, G + g, 0)),
                  pl.BlockSpec(e.shape, lambda b, g, i: (0, 0))],
        out_specs=pl.BlockSpec((None, tq, GD), lambda b, g, i: (b, i, g)),
        out_shape=jax.ShapeDtypeStruct((B, S, G * GD), F32),
        scratch_shapes=[pltpu.VMEM((rows, LANES + wb), BF16), pltpu.VMEM((rows, tk), F32), pltpu.VMEM((rows, tk), F32),
                        pltpu.VMEM((rows, tk), BF16), pltpu.VMEM((rows, tk), BF16),
                        pltpu.VMEM((rows, LANES), F32), pltpu.VMEM((rows, LANES), F32),
                        pltpu.VMEM((rows, LANES), F32)],
        compiler_params=_cp("parallel", "parallel", "parallel"), name="slc_attention",
    )(q, selb, kvt, kvt, e)


def _win_attn_body(q_ref, kt_ref, vt_ref, o_ref, *, tq, span):
    i = pl.program_id(2)
    start = pl.multiple_of(jnp.maximum(i * tq - WINDOW, 0), LANES)
    ks = pl.ds(start, span)
    t = i * tq + lax.broadcasted_iota(jnp.int32, (tq, span), 0)
    dist = t - (start + lax.broadcasted_iota(jnp.int32, (tq, span), 1))
    bias = jnp.where((dist >= 0) & (dist < WINDOW), 0.0, NEG_INF)
    vt = vt_ref[:, ks]
    _softmax_attend(q_ref, kt_ref[:, ks], bias, lambda p: _dot_nt(p, vt), o_ref, tq, False)


def _win_attn(q, kvt, tq):
    B, S, _ = q.shape
    G = NSA_GROUPS
    GD = NSA_HPG * HEAD_DIM
    span = min(tq + WINDOW, S)
    return pl.pallas_call(
        functools.partial(_win_attn_body, tq=tq, span=span), grid=(B, G, S // tq),
        in_specs=[pl.BlockSpec((None, tq, GD), lambda b, g, i: (b, i, g)),
                  pl.BlockSpec((None, HEAD_DIM, S), lambda b, g, i: (b, 2 * G + g, 0)),
                  pl.BlockSpec((None, HEAD_DIM, S), lambda b, g, i: (b, 3 * G + g, 0))],
        out_specs=pl.BlockSpec((None, tq, GD), lambda b, g, i: (b, i, g)),
        out_shape=jax.ShapeDtypeStruct((B, S, G * GD), F32),
        compiler_params=_cp("parallel", "parallel", "parallel"), name="win_attention",
    )(q, kvt, kvt)


def _nsa_out_body(oc_ref, os_ref, ow_ref, gt_ref, ge_ref, w_ref, x_ref, mod_ref, g_ref, o_ref):
    D = oc_ref.shape[1]
    ge = ge_ref[...]
    gx = sum(_dot(part, ge) for part in _split_bf16(gt_ref[...], 2))
    o = gx[:, 0:D] * oc_ref[...] + gx[:, D:2 * D] * os_ref[...] + gx[:, 2 * D:3 * D] * ow_ref[...]
    y = _dot(_bf(o), w_ref[...])
    o_ref[...] = x_ref[...] + mod_ref[2:3, :] * _rms(y, g_ref[...])


def _nsa_out(oc, osl, ow, gates, gexp, w, x, mod, g, ts):
    B, S, D = x.shape
    tok = lambda n: pl.BlockSpec((None, ts, n), lambda b, i: (b, i, 0))
    full = lambda a: pl.BlockSpec(a.shape, lambda b, i: (0,) * a.ndim)
    return pl.pallas_call(
        _nsa_out_body, grid=(B, S // ts),
        in_specs=[tok(D), tok(D), tok(D), tok(gates.shape[2]), full(gexp), full(w), tok(D),
                  pl.BlockSpec((None, 6, D), lambda b, i: (b, 0, 0)), full(g)],
        out_specs=tok(D), out_shape=jax.ShapeDtypeStruct((B, S, D), F32),
        compiler_params=_cp("parallel", "parallel"), name="nsa_out",
    )(oc, osl, ow, gates, gexp, w, x, mod, g)


def _importance_matrix(nc, nb):
    n_cmp = nc - CMP_RATIO + 1
    w = np.zeros((nc, nb), np.float32)
    for m in range(SLC_RATIO):
        for n in range(CMP_RATIO):
            off = m - n + CMP_RATIO - 1
            for j in range(nb):
                src = SLC_RATIO * j + off - (CMP_RATIO - 1)
                if 0 <= src < n_cmp:
                    w[src, j] += 1.0
    return jnp.asarray(w, BF16)


def _block_expand(nb, s):
    return jnp.asarray(np.arange(nb)[:, None] == (np.arange(s)[None, :] // SLC_LEN), BF16)


def _gate_expand(n_in):
    hd = NSA_HEADS * HEAD_DIM
    w = np.zeros((n_in, 3 * hd), np.float32)
    for head in range(NSA_HEADS):
        for br in range(3):
            w[head * 3 + br, br * hd + head * HEAD_DIM: br * hd + (head + 1) * HEAD_DIM] = 1.0
    return jnp.asarray(w, BF16)


def kernel(x, c, ada_w, ada_b, norm_g, mlp_up, mlp_down, rw_mu, rw_r, rw_k, rw_v, rw_o, rw_w0, rw_wa, rw_wb,
           rw_a0, rw_aa, rw_ab, rw_ga, rw_gb, rw_kk, rw_ka, rw_rk, rw_lng, rw_lnb, kv_norm_g, kv_ada_w,
           kv_ada_b, kv_w, cmp_pos_k, cmp_pos_v, cmp_k_w1, cmp_k_w2, cmp_v_w1, cmp_v_w2, nsa_wqg, nsa_wo):
    B, S, D = x.shape
    depth = ada_w.shape[0]
    n_a = rw_mu.shape[0]
    bf = _bf
    row = lambda a: a.reshape(1, -1)
    ts = min(512, S)

    c8 = jnp.pad(c, ((0, 8 - B), (0, 0)))
    mods = _ada(c8, ada_w, ada_b)[:, :B].reshape(depth, B, 6, D)
    mkv = _ada(c8, kv_ada_w[None], kv_ada_b[None])[0, :B].reshape(B, 2, D)

    GD = NSA_GROUPS * HEAD_DIM
    HD = NSA_HEADS * HEAD_DIM
    shared = None
    for layer in range(depth):
        mod = mods[layer]
        ng = norm_g[layer]
        if layer < n_a:
            i = layer
            r, k, v, lw, kk, a, g = _rw_front(
                x, mod, row(ng[0]), rw_mu[i], bf(rw_r[i]), bf(rw_k[i]), bf(rw_v[i]), bf(rw_wa[i]),
                bf(rw_wb[i]), row(rw_w0[i]), bf(rw_aa[i]), bf(rw_ab[i]), row(rw_a0[i]), bf(rw_ga[i]),
                bf(rw_gb[i]), row(rw_kk[i]), row(rw_ka[i]), min(256, S))
            y = _rw_recurrence(r, k, v, lw, kk, a, g, row(rw_rk[i]), row(rw_lng[i]), row(rw_lnb[i]),
                               min(512, S), 2)
            x = _proj_res(y, bf(rw_o[i]), x, mod, row(ng[1]), ts)
        else:
            j = layer - n_a
            wqg = nsa_wqg[j]
            wq = bf(wqg[:, :HD])
            wg = bf(jnp.pad(wqg[:, HD:], ((0, 0), (0, LANES - 3 * NSA_HEADS))))
            if shared is None:
                wc = bf(kv_w[:, :2 * GD])
                wt = bf(kv_w[:, 2 * GD:].T)
            q, gates, kvc, kvt = _nsa_proj(x, mod, mkv, row(ng[0]), row(kv_norm_g), wq, wg, wc, wt, ts)
            if shared is None:
                fk, fv = _cmp1(kvc, cmp_pos_k, cmp_pos_v, bf(cmp_k_w1), bf(cmp_v_w1), min(1024, S))
                kct, vc = _cmp2(fk, fv, bf(cmp_k_w2.T), bf(cmp_v_w2))
                shared = (kct, vc, kvt)
            kct, vc, kvt_s = shared
            nc, nb = S // CMP_STRIDE, S // SLC_LEN
            tq = min(256, S)
            oc, sel = _cmp_attn(q, kct, vc, _importance_matrix(nc, nb), tq)
            osl = _slc_attn(q, sel, kvt_s, _block_expand(nb, S), min(512, S), min(512, S), min(LANES, nb))
            ow = _win_attn(q, kvt_s, min(256, S))
            x = _nsa_out(oc, osl, ow, gates, _gate_expand(gates.shape[2]), bf(nsa_wo[j]), x, mod,
                         row(ng[1]), ts)
        x = _mlp(x, mod, row(ng[2]), row(ng[3]), bf(mlp_up[layer]), bf(mlp_down[layer]), min(1024, S),
                 1024)
    return x
```

```python
import functools
import math

import numpy as np
import jax
import jax.numpy as jnp
from jax import lax
from jax.experimental import pallas as pl
from jax.experimental.pallas import tpu as pltpu

F32 = jnp.float32
BF16 = jnp.bfloat16
HI = lax.Precision.HIGHEST

NORM_EPS = 1e-6
GN_EPS = 64e-5
RWKV_HEAD = 64
HEAD_DIM = 64
NSA_GROUPS = 4
NSA_HPG = 4
NSA_HEADS = 16
CMP_LEN = 32
CMP_STRIDE = 16
CMP_RATIO = CMP_LEN // CMP_STRIDE
SLC_LEN = 64
SLC_RATIO = SLC_LEN // CMP_STRIDE
TOP_N = 16
WINDOW = 512
FORCED_SCORE = 1e4
NEG_INF = -1e30
LOG2E = math.log2(math.e)

LANES = 128
VMEM_LIMIT = 56 * 1024 * 1024
REC_CHUNK = 64
SOFTMAX_ROWS = 32
CMP_COLS = 256
SEL_BLOCKS = 64


def _cp(*sem):
    return pltpu.CompilerParams(dimension_semantics=sem, vmem_limit_bytes=VMEM_LIMIT)


def _dot(a, b, **kw):
    return jnp.dot(a, b, preferred_element_type=F32, **kw)


def _dot_nt(a, b, **kw):
    return lax.dot_general(a, b, (((1,), (1,)), ((), ())), preferred_element_type=F32, **kw)


def _dot_tn(a, b, **kw):
    return lax.dot_general(a, b, (((0,), (0,)), ((), ())), preferred_element_type=F32, **kw)


def _bf(t):
    return t.astype(BF16)


def _split_bf16(x, parts):
    out = []
    for _ in range(parts - 1):
        hi = _bf(x)
        out.append(hi)
        x = x - hi.astype(F32)
    out.append(_bf(x))
    return out


def _rms(x, g):
    return x * lax.rsqrt(jnp.mean(x * x, axis=-1, keepdims=True) + NORM_EPS) * g


def _ada_body(c_ref, w_ref, b_ref, o_ref):
    c = c_ref[...]
    o_ref[...] = _dot(c * jax.nn.sigmoid(c), w_ref[...], precision=HI) + b_ref[...]


def _ada(c8, w, b):
    L, D, N = w.shape
    tn = 1024
    return pl.pallas_call(
        _ada_body, grid=(L, N // tn),
        in_specs=[pl.BlockSpec((8, D), lambda l, j: (0, 0)),
                  pl.BlockSpec((None, D, tn), lambda l, j: (l, 0, j)),
                  pl.BlockSpec((None, 1, tn), lambda l, j: (l, 0, j))],
        out_specs=pl.BlockSpec((None, 8, tn), lambda l, j: (l, 0, j)),
        out_shape=jax.ShapeDtypeStruct((L, 8, N), F32),
        compiler_params=_cp("parallel", "parallel"), name="ada",
    )(c8, w, b.reshape(L, 1, N))


def _rw_front_body(x_ref, xp_ref, mod_ref, g_ref, mu_ref, wr, wk, wv, wa, wb, w0, aa, ab, a0, ga, gb,
                   kk_ref, ka_ref, r_o, k_o, v_o, lw_o, kk_o, a_o, g_o):
    i = pl.program_id(1)
    g = g_ref[...]
    sh, sc = mod_ref[0:1, :], mod_ref[1:2, :]
    h = _rms(x_ref[...], g) * (1 + sc) + sh
    hp = _rms(xp_ref[...], g) * (1 + sc) + sh
    hp = jnp.where(i > 0, hp[7:8, :], 0.0)
    row = lax.broadcasted_iota(jnp.int32, h.shape, 0)
    hs = jnp.where(row == 0, hp, pltpu.roll(h, 1, axis=0))
    xx = hs - h
    mu = mu_ref[...]
    xr, xw, xk, xv, xa, xg = [_bf(h + xx * mu[j:j + 1, :]) for j in range(6)]
    r = _dot(xr, wr[...])
    k = _dot(xk, wk[...])
    v = _dot(xv, wv[...])
    wl = w0[...] + _dot(_bf(jnp.tanh(_dot(xw, wa[...]))), wb[...])
    lw = -jax.nn.sigmoid(wl) * math.exp(-0.5)
    a = jax.nn.sigmoid(a0[...] + _dot(_bf(_dot(xa, aa[...])), ab[...]))
    gate = _dot(_bf(jax.nn.sigmoid(_dot(xg, ga[...]))), gb[...])
    r_o[...] = r
    k_o[...] = k * (1 + (a - 1) * ka_ref[...])
    v_o[...] = v
    lw_o[...] = lw
    kk_o[...] = k * kk_ref[...]
    a_o[...] = a
    g_o[...] = gate


def _rw_front(x, mod, g, mu, wr, wk, wv, wa, wb, w0, aa, ab, a0, ga, gb, kk, ka, ts):
    B, S, D = x.shape
    full = lambda a: pl.BlockSpec(a.shape, lambda b, i: (0,) * a.ndim)
    tok = pl.BlockSpec((None, ts, D), lambda b, i: (b, i, 0))
    ws = [wr, wk, wv, wa, wb, w0, aa, ab, a0, ga, gb, kk, ka]
    return pl.pallas_call(
        _rw_front_body, grid=(B, S // ts),
        in_specs=[tok,
                  pl.BlockSpec((None, 8, D), lambda b, i: (b, jnp.maximum(i * (ts // 8) - 1, 0), 0)),
                  pl.BlockSpec((None, 6, D), lambda b, i: (b, 0, 0)),
                  full(g), full(mu)] + [full(w) for w in ws],
        out_specs=[tok] * 7,
        out_shape=[jax.ShapeDtypeStruct((B, S, D), F32)] * 7,
        compiler_params=_cp("parallel", "parallel"), name="rwkv_front",
    )(x, x, mod, g, mu, *ws)


def _rec_body(r_ref, k_ref, v_ref, lw_ref, kk_ref, a_ref, g_ref, rk_ref, lng_ref, lnb_ref, o_ref, st_ref,
              *, n_chunks, n_pairs):
    C = REC_CHUNK
    L = LANES
    C2 = 2 * C

    @pl.when(pl.program_id(2) == 0)
    def _():
        st_ref[...] = jnp.zeros_like(st_ref)

    h0 = lax.broadcasted_iota(jnp.int32, (C, L), 1) < RWKV_HEAD
    rr = lax.broadcasted_iota(jnp.int32, (C2, C2), 0)
    cc = lax.broadcasted_iota(jnp.int32, (C2, C2), 1)
    same = rr // C == cc // C
    incl = same & (rr >= cc)
    strict = same & (rr > cc)
    eye = (rr == cc).astype(F32)
    lvl_masks = [(rr // sz == cc // sz) & (rr // (sz // 2) != cc // (sz // 2))
                 for sz in (2 ** e for e in range(1, int(math.log2(C)) + 1))]
    tri_b = (lax.broadcasted_iota(jnp.int32, (C, C), 0) >= lax.broadcasted_iota(jnp.int32, (C, C), 1)).astype(BF16)
    bd_b = (lax.broadcasted_iota(jnp.int32, (L, L), 0) // RWKV_HEAD
            == lax.broadcasted_iota(jnp.int32, (L, L), 1) // RWKV_HEAD).astype(BF16)
    units = [(pr, c) for pr in range(n_pairs) for c in range(n_chunks)]

    def head_sum(t):
        return sum(_dot(p, bd_b) for p in _split_bf16(t, 2))

    def stack(t):
        zero = jnp.zeros_like(t)
        return jnp.concatenate([jnp.where(h0, t, zero), jnp.where(h0, zero, t)], axis=0)

    sl = lambda ref, pr, c: ref[pl.ds(c * C, C), pr * L:(pr + 1) * L]
    ld = [dict(r=sl(r_ref, pr, c), k=sl(k_ref, pr, c), v=sl(v_ref, pr, c), lw=sl(lw_ref, pr, c),
               kk=sl(kk_ref, pr, c), a=sl(a_ref, pr, c)) for pr, c in units]
    kkn = [d["kk"] / jnp.maximum(jnp.sqrt(head_sum(d["kk"] * d["kk"])), 1e-12) for d in ld]
    cl = [sum(_dot(tri_b, p) for p in _split_bf16(d["lw"], 3)) for d in ld]
    cl_end = [x[C - 1:C, :] for x in cl]
    e_neg = [jnp.exp(-x) for x in cl]
    e_end = [jnp.exp(ce - x) for x, ce in zip(cl, cl_end)]
    rt2 = [_bf(stack(d["r"] * jnp.exp(x))) for d, x in zip(ld, cl)]
    at2 = [_bf(stack(-kn * jnp.exp(x - d["lw"]))) for d, kn, x in zip(ld, kkn, cl)]
    bt2 = [_bf(stack(kn * d["a"] * e)) for d, kn, e in zip(ld, kkn, e_neg)]
    kt2 = [_bf(stack(d["k"] * e)) for d, e in zip(ld, e_neg)]
    bend2 = [_bf(stack(kn * d["a"] * e)) for d, kn, e in zip(ld, kkn, e_end)]
    kend2 = [_bf(stack(d["k"] * e)) for d, e in zip(ld, e_end)]
    v2 = [_bf(stack(d["v"])) for d in ld]
    decay = [jnp.exp(x) for x in cl_end]

    gm = [_dot_nt(jnp.concatenate([a_, r_], axis=0), jnp.concatenate([b_, k_], axis=0))
          for a_, r_, b_, k_ in zip(at2, rt2, bt2, kt2)]
    a_ab = [jnp.where(strict, g_[:C2, :C2], 0.0) for g_ in gm]
    a_ak = [_bf(jnp.where(strict, g_[:C2, C2:], 0.0)) for g_ in gm]
    a_rb = [_bf(jnp.where(incl, g_[C2:, :C2], 0.0)) for g_ in gm]
    a_rk = [_bf(jnp.where(incl, g_[C2:, C2:], 0.0)) for g_ in gm]

    tm = [eye + jnp.where(lvl_masks[0], a_, 0.0) for a_ in a_ab]
    for lm in lvl_masks[1:]:
        off = [_bf(jnp.where(lm, a_, 0.0)) for a_ in a_ab]
        half = [_bf(_dot(_bf(t_), o_)) for t_, o_ in zip(tm, off)]
        tm = [t_ + _dot(h_, _bf(t_)) for t_, h_ in zip(tm, half)]
    tm = [_bf(t_) for t_ in tm]

    w2 = [_bf(_dot(t_, a_)) for t_, a_ in zip(tm, at2)]
    akv = [_bf(_dot(a_, v_)) for a_, v_ in zip(a_ak, v2)]
    uh2 = [_bf(_dot(t_, x_)) for t_, x_ in zip(tm, akv)]
    rw2 = [_bf(r_.astype(F32) + _dot(a_, w_)) for r_, a_, w_ in zip(rt2, a_rb, w2)]
    yh2 = [_dot(a_, u_) + _dot(b_, v_) for a_, u_, b_, v_ in zip(a_rb, uh2, a_rk, v2)]
    ec = [_bf(_dot_tn(w_, b_)) for w_, b_ in zip(w2, bend2)]
    qc = [_dot_tn(jnp.concatenate([u_, v_], axis=0), jnp.concatenate([b_, k_], axis=0))
          for u_, v_, b_, k_ in zip(uh2, v2, bend2, kend2)]

    states = [st_ref[pr] for pr in range(n_pairs)]
    y2 = [None] * len(units)
    for c in range(n_chunks):
        for pr in range(n_pairs):
            u = pr * n_chunks + c
            n = states[pr]
            n_b = _bf(n)
            y2[u] = _dot_nt(rw2[u], n_b) + yh2[u]
            states[pr] = n * decay[u] + _dot(n_b, ec[u]) + qc[u]
    for pr in range(n_pairs):
        st_ref[pr] = states[pr]

    for u, (pr, c) in enumerate(units):
        d = ld[u]
        lanes = slice(pr * L, (pr + 1) * L)
        rk, lng, lnb = rk_ref[:, lanes], lng_ref[:, lanes], lnb_ref[:, lanes]
        y = y2[u][:C] + y2[u][C:]
        mean = head_sum(y) * (1.0 / RWKV_HEAD)
        yc = y - mean
        var = head_sum(yc * yc) * (1.0 / RWKV_HEAD)
        yn = yc * lax.rsqrt(var + GN_EPS) * lng + lnb
        bonus = head_sum(d["r"] * d["k"] * rk) * d["v"]
        o_ref[pl.ds(c * C, C), lanes] = (yn + bonus) * sl(g_ref, pr, c)


def _rw_recurrence(r, k, v, lw, kk, a, g, rk, lng, lnb, tt, n_pairs):
    B, S, D = r.shape
    width = n_pairs * LANES
    tok = pl.BlockSpec((None, tt, width), lambda b, hp, t: (b, t, hp))
    vec = pl.BlockSpec((1, width), lambda b, hp, t: (0, hp))
    return pl.pallas_call(
        functools.partial(_rec_body, n_chunks=tt // REC_CHUNK, n_pairs=n_pairs),
        grid=(B, D // width, S // tt),
        in_specs=[tok] * 7 + [vec] * 3,
        out_specs=tok,
        out_shape=jax.ShapeDtypeStruct((B, S, D), F32),
        scratch_shapes=[pltpu.VMEM((n_pairs, LANES, LANES), F32)],
        compiler_params=_cp("parallel", "parallel", "arbitrary"), name="rwkv_recurrence",
    )(r, k, v, lw, kk, a, g, rk, lng, lnb)


def _proj_res_body(y_ref, w_ref, x_ref, mod_ref, g_ref, o_ref):
    y = _dot(_bf(y_ref[...]), w_ref[...])
    o_ref[...] = x_ref[...] + mod_ref[2:3, :] * _rms(y, g_ref[...])


def _proj_res(y, w, x, mod, g, ts):
    B, S, D = x.shape
    tok = pl.BlockSpec((None, ts, D), lambda b, i: (b, i, 0))
    return pl.pallas_call(
        _proj_res_body, grid=(B, S // ts),
        in_specs=[tok, pl.BlockSpec(w.shape, lambda b, i: (0, 0)), tok,
                  pl.BlockSpec((None, 6, D), lambda b, i: (b, 0, 0)),
                  pl.BlockSpec(g.shape, lambda b, i: (0, 0))],
        out_specs=tok, out_shape=jax.ShapeDtypeStruct((B, S, D), F32),
        compiler_params=_cp("parallel", "parallel"), name="proj_residual",
    )(y, w, x, mod, g)


def _mlp_body(x_ref, mod_ref, g2_ref, g3_ref, up_ref, dn_ref, o_ref, h_ref, acc_ref):
    f = pl.program_id(2)

    @pl.when(f == 0)
    def _():
        h = _rms(x_ref[...], g2_ref[...]) * (1 + mod_ref[4:5, :]) + mod_ref[3:4, :]
        h_ref[...] = _bf(h)
        acc_ref[...] = jnp.zeros_like(acc_ref)

    z = jnp.maximum(_dot(h_ref[...], up_ref[...]), 0.0)
    acc_ref[...] += _dot(_bf(z * z), dn_ref[...])

    @pl.when(f == pl.num_programs(2) - 1)
    def _():
        o_ref[...] = x_ref[...] + mod_ref[5:6, :] * _rms(acc_ref[...], g3_ref[...])


def _mlp(x, mod, g2, g3, up, dn, ts, tf):
    B, S, D = x.shape
    F = up.shape[1]
    tok = pl.BlockSpec((None, ts, D), lambda b, i, f: (b, i, 0))
    vec = pl.BlockSpec((1, D), lambda b, i, f: (0, 0))
    return pl.pallas_call(
        _mlp_body, grid=(B, S // ts, F // tf),
        in_specs=[tok, pl.BlockSpec((None, 6, D), lambda b, i, f: (b, 0, 0)), vec, vec,
                  pl.BlockSpec((D, tf), lambda b, i, f: (0, f)),
                  pl.BlockSpec((tf, D), lambda b, i, f: (f, 0))],
        out_specs=tok, out_shape=jax.ShapeDtypeStruct((B, S, D), F32),
        scratch_shapes=[pltpu.VMEM((ts, D), BF16), pltpu.VMEM((ts, D), F32)],
        compiler_params=_cp("parallel", "parallel", "arbitrary"), name="mlp",
    )(x, mod, g2, g3, up, dn)


def _nsa_proj_body(x_ref, mod_ref, mkv_ref, g_ref, gkv_ref, wq_ref, wg_ref, wc_ref, wt_ref,
                   q_o, gt_o, c_o, t_o):
    x = x_ref[...]
    xn = x * lax.rsqrt(jnp.mean(x * x, axis=-1, keepdims=True) + NORM_EPS)
    h = _bf((xn * g_ref[...]) * (1 + mod_ref[1:2, :]) + mod_ref[0:1, :])
    hkv = _bf((xn * gkv_ref[...]) * (1 + mkv_ref[1:2, :]) + mkv_ref[0:1, :])
    q_o[...] = _bf(_dot(h, wq_ref[...]) * (HEAD_DIM ** -0.5 * LOG2E))
    gt_o[...] = jax.nn.sigmoid(_dot(h, wg_ref[...]))
    c_o[...] = _dot(hkv, wc_ref[...])
    t_o[...] = _bf(_dot_nt(wt_ref[...], hkv))


def _nsa_proj(x, mod, mkv, g, gkv, wq, wg, wc, wt, ts):
    B, S, D = x.shape
    tok = lambda n: pl.BlockSpec((None, ts, n), lambda b, i: (b, i, 0))
    full = lambda a: pl.BlockSpec(a.shape, lambda b, i: (0,) * a.ndim)
    return pl.pallas_call(
        _nsa_proj_body, grid=(B, S // ts),
        in_specs=[tok(D), pl.BlockSpec((None, 6, D), lambda b, i: (b, 0, 0)),
                  pl.BlockSpec((None, 2, D), lambda b, i: (b, 0, 0)),
                  full(g), full(gkv), full(wq), full(wg), full(wc), full(wt)],
        out_specs=[tok(wq.shape[1]), tok(wg.shape[1]), tok(wc.shape[1]),
                   pl.BlockSpec((None, wt.shape[0], ts), lambda b, i: (b, 0, i))],
        out_shape=[jax.ShapeDtypeStruct((B, S, wq.shape[1]), BF16),
                   jax.ShapeDtypeStruct((B, S, wg.shape[1]), F32),
                   jax.ShapeDtypeStruct((B, S, wc.shape[1]), F32),
                   jax.ShapeDtypeStruct((B, wt.shape[0], S), BF16)],
        compiler_params=_cp("parallel", "parallel"), name="nsa_proj",
    )(x, mod, mkv, g, gkv, wq, wg, wc, wt)


def _cmp1_body(k0_ref, k1_ref, v0_ref, v1_ref, pk_ref, pv_ref, wk_ref, wv_ref, fk_o, fv_o, *, n_chunk):
    hid = wk_ref.shape[1]
    gpr = LANES // HEAD_DIM
    for srcs, pos, w1, out in (((k0_ref, k1_ref), pk_ref, wk_ref, fk_o), ((v0_ref, v1_ref), pv_ref, wv_ref, fv_o)):
        acc = [[jnp.zeros((n_chunk, hid), F32) for _ in range(CMP_RATIO)] for _ in range(NSA_GROUPS)]
        for l in range(CMP_STRIDE):
            for si, src in enumerate(srcs):
                rows = src[pl.ds(l, n_chunk, stride=CMP_STRIDE), :]
                for gi in range(gpr):
                    g = si * gpr + gi
                    t = rows[:, gi * HEAD_DIM:(gi + 1) * HEAD_DIM]
                    for half in range(CMP_RATIO):
                        p = half * CMP_STRIDE + l
                        lhs = _bf(t + pos[p:p + 1, :])
                        acc[g][half] = acc[g][half] + _dot(lhs, w1[p * HEAD_DIM:(p + 1) * HEAD_DIM, :])
        for g in range(NSA_GROUPS):
            for half in range(CMP_RATIO):
                out[g, :, half * hid:(half + 1) * hid] = acc[g][half]


def _cmp1(kvc, pos_k, pos_v, w1k, w1v, ts):
    B, S, _ = kvc.shape
    G = NSA_GROUPS
    hid = w1k.shape[1]
    nch = ts // CMP_STRIDE
    full = lambda a: pl.BlockSpec(a.shape, lambda b, i: (0,) * a.ndim)
    src = lambda j: pl.BlockSpec((None, ts, LANES), lambda b, i: (b, i, j))
    outs = pl.BlockSpec((None, G, nch, CMP_RATIO * hid), lambda b, i: (b, 0, i, 0))
    return pl.pallas_call(
        functools.partial(_cmp1_body, n_chunk=nch), grid=(B, S // ts),
        in_specs=[src(0), src(1), src(2), src(3), full(pos_k), full(pos_v), full(w1k), full(w1v)],
        out_specs=[outs, outs],
        out_shape=[jax.ShapeDtypeStruct((B, G, S // CMP_STRIDE, CMP_RATIO * hid), F32)] * 2,
        compiler_params=_cp("parallel", "parallel"), name="cmp_stage1",
    )(kvc, kvc, kvc, kvc, pos_k, pos_v, w1k, w1v)


def _gelu_tanh(x):
    return 0.5 * x * (1.0 + jnp.tanh(math.sqrt(2.0 / math.pi) * (x + 0.044715 * (x * x * x))))


def _cmp2_body(fk_ref, fv_ref, w2kt_ref, w2v_ref, kt_o, v_o):
    nc = fk_ref.shape[0]
    hid = w2v_ref.shape[0]
    row = lax.broadcasted_iota(jnp.int32, (nc, hid), 0)

    def hidden(f):
        nxt = jnp.where(row == nc - 1, 0.0, pltpu.roll(f[:, hid:2 * hid], nc - 1, axis=0))
        return _bf(_gelu_tanh(f[:, 0:hid] + nxt))

    kt_o[...] = _bf(_dot_nt(w2kt_ref[...], hidden(fk_ref[...])))
    v_o[...] = _bf(_dot(hidden(fv_ref[...]), w2v_ref[...]))


def _cmp2(fk, fv, w2kt, w2v):
    B, G, NC, H2 = fk.shape
    fin = pl.BlockSpec((None, None, NC, H2), lambda b, g: (b, g, 0, 0))
    return pl.pallas_call(
        _cmp2_body, grid=(B, G),
        in_specs=[fin, fin, pl.BlockSpec(w2kt.shape, lambda b, g: (0, 0)),
                  pl.BlockSpec(w2v.shape, lambda b, g: (0, 0))],
        out_specs=[pl.BlockSpec((None, None, HEAD_DIM, NC), lambda b, g: (b, g, 0, 0)),
                   pl.BlockSpec((None, None, NC, HEAD_DIM), lambda b, g: (b, g, 0, 0))],
        out_shape=[jax.ShapeDtypeStruct((B, G, HEAD_DIM, NC), BF16),
                   jax.ShapeDtypeStruct((B, G, NC, HEAD_DIM), BF16)],
        compiler_params=_cp("parallel", "parallel"), name="cmp_stage2",
    )(fk, fv, w2kt, w2v)


def _head_queries(q_ref):
    return [q_ref[:, h * HEAD_DIM:(h + 1) * HEAD_DIM] for h in range(NSA_HPG)]


def _softmax_attend(q_ref, kt, bias, pv, o_ref, tq, want_psum):
    rb = SOFTMAX_ROWS
    n_blk = tq // rb
    qs = _head_queries(q_ref)
    scores = {0: _dot(qs[0], kt)}
    tot = [None] * n_blk
    for h in range(NSA_HPG):
        if h + 1 < NSA_HPG:
            scores[h + 1] = _dot(qs[h + 1], kt)
        s_h = scores.pop(h)
        blocks = []
        for b in range(n_blk):
            s = s_h[b * rb:(b + 1) * rb, :] + bias[b * rb:(b + 1) * rb, :]
            m = jnp.maximum(jnp.max(s, axis=-1, keepdims=True), 0.1 * NEG_INF)
            e = jnp.exp2(s - m)
            p = e * (1.0 / jnp.maximum(jnp.sum(e, axis=-1, keepdims=True), 1e-30))
            blocks.append(_bf(p))
            if want_psum:
                tot[b] = p if tot[b] is None else tot[b] + p
        o_ref[:, h * HEAD_DIM:(h + 1) * HEAD_DIM] = pv(jnp.concatenate(blocks, axis=0))
    return tot


def _select_blocks(imp, t0):
    nb, tq = imp.shape
    blk = lax.broadcasted_iota(jnp.int32, (nb, tq), 0)
    cur = (t0 + lax.broadcasted_iota(jnp.int32, (nb, tq), 1)) // SLC_LEN
    forced = (blk == 0) | (blk == cur) | (blk == cur - 1)
    imp = jnp.where(forced, FORCED_SCORE, imp)
    imp = jnp.where(blk > cur, -1.0, imp)
    bits = lax.bitcast_convert_type(imp, jnp.int32)
    k_sel = float(min(TOP_N, nb))
    count = lambda mask: jnp.sum(jnp.where(mask, 1.0, 0.0), axis=0, keepdims=True)

    def refine(it, thr):
        cand = thr | jnp.left_shift(jnp.int32(1), 30 - it)
        return jnp.where(count(bits >= cand) >= k_sel, cand, thr)

    thr = lax.fori_loop(0, 31, refine, jnp.zeros((1, tq), jnp.int32))
    above = bits > thr
    tied = bits == thr
    tri = (lax.broadcasted_iota(jnp.int32, (nb, nb), 0) >= lax.broadcasted_iota(jnp.int32, (nb, nb), 1))
    rank = _dot(tri.astype(BF16), jnp.where(tied, 1.0, 0.0).astype(BF16))
    return above | (tied & (rank <= k_sel - count(above)))


def _cmp_attn_body(q_ref, kt_ref, v_ref, wimp_ref, o_ref, sel_ref, imp_ref, *, tq):
    i = pl.program_id(2)
    nc = kt_ref.shape[1]
    nb = wimp_ref.shape[1]
    step = min(CMP_COLS, nc)
    n_need = (i * tq + tq - CMP_LEN) // CMP_STRIDE + 1
    for ncv in range(step, nc + 1, step):
        @pl.when((n_need > ncv - step) & ((n_need <= ncv) | (ncv == nc)))
        def _():
            t = i * tq + lax.broadcasted_iota(jnp.int32, (tq, ncv), 0)
            cmp_end = lax.broadcasted_iota(jnp.int32, (tq, ncv), 1) * CMP_STRIDE + (CMP_LEN - 1)
            bias = jnp.where(cmp_end <= t, 0.0, NEG_INF)
            v = v_ref[0:ncv, :]
            tot = _softmax_attend(q_ref, kt_ref[:, 0:ncv], bias, lambda p: _dot(p, v), o_ref, tq, True)
            wimp = wimp_ref[0:ncv, :]
            imp_ref[...] = sum(_dot(part, wimp) for part in _split_bf16(jnp.concatenate(tot, axis=0), 3))

    bstep = min(SEL_BLOCKS, nb)
    b_need = (i * tq + tq - 1) // SLC_LEN + 1
    for nbv in range(bstep, nb + 1, bstep):
        @pl.when((b_need > nbv - bstep) & ((b_need <= nbv) | (nbv == nb)))
        def _():
            sel = _select_blocks(imp_ref[:, 0:nbv].T, i * tq)
            sel_ref[:, 0:nbv] = _bf(jnp.where(sel, 0.0, NEG_INF).T)
            if nbv < nb:
                sel_ref[:, nbv:nb] = jnp.full((tq, nb - nbv), NEG_INF, BF16)


def _cmp_attn(q, kct, vc, wimp, tq):
    B, S, _ = q.shape
    G = NSA_GROUPS
    NC = kct.shape[3]
    NB = wimp.shape[1]
    GD = NSA_HPG * HEAD_DIM
    return pl.pallas_call(
        functools.partial(_cmp_attn_body, tq=tq), grid=(B, G, S // tq),
        in_specs=[pl.BlockSpec((None, tq, GD), lambda b, g, i: (b, i, g)),
                  pl.BlockSpec((None, None, HEAD_DIM, NC), lambda b, g, i: (b, g, 0, 0)),
                  pl.BlockSpec((None, None, NC, HEAD_DIM), lambda b, g, i: (b, g, 0, 0)),
                  pl.BlockSpec(wimp.shape, lambda b, g, i: (0, 0))],
        out_specs=[pl.BlockSpec((None, tq, GD), lambda b, g, i: (b, i, g)),
                   pl.BlockSpec((None, None, tq, NB), lambda b, g, i: (b, g, i, 0))],
        out_shape=[jax.ShapeDtypeStruct((B, S, G * GD), F32),
                   jax.ShapeDtypeStruct((B, G, S, NB), BF16)],
        scratch_shapes=[pltpu.VMEM((tq, NB), F32)],
        compiler_params=_cp("parallel", "parallel", "parallel"), name="cmp_attention_topk",
    )(q, kct, vc, wimp)


def _slc_attn_body(q_ref, selb_ref, kt_ref, vt_ref, e_ref, o_ref, lhs_ref, s0_ref, s1_ref, p0_ref, p1_ref,
                   alpha_ref, m_ref, acc_ref, *, tq, tk, wb):
    i = pl.program_id(2)
    rb = SOFTMAX_ROWS
    rows = NSA_HPG * tq
    tiles_per_window = wb * SLC_LEN // tk
    for h in range(NSA_HPG):
        lhs_ref[pl.ds(h * tq, tq), 0:HEAD_DIM] = q_ref[:, h * HEAD_DIM:(h + 1) * HEAD_DIM]
    lhs_ref[:, HEAD_DIM:LANES] = jnp.zeros((rows, LANES - HEAD_DIM), BF16)
    m_ref[...] = jnp.full(m_ref.shape, NEG_INF, F32)
    acc_ref[...] = jnp.zeros_like(acc_ref)
    s_refs = (s0_ref, s1_ref)
    p_refs = (p0_ref, p1_ref)
    p1_ref[...] = jnp.zeros((rows, tk), BF16)
    ones = jnp.ones((LANES - HEAD_DIM, tk), BF16)
    zeros = jnp.zeros((LANES - HEAD_DIM, tk), BF16)
    tile_keys = lambda j: pl.ds(pl.multiple_of(j * tk, tk), tk)

    def set_window(w):
        win = selb_ref[:, pl.ds(pl.multiple_of(w * wb, wb), wb)]
        for h in range(NSA_HPG):
            lhs_ref[pl.ds(h * tq, tq), LANES:LANES + wb] = win

    def scores(j, s_ref):
        @pl.when(j % tiles_per_window == 0)
        def _():
            set_window(j // tiles_per_window)
        ks = tile_keys(j)
        w0 = pl.multiple_of((j // tiles_per_window) * wb, wb)
        rhs = jnp.concatenate([kt_ref[:, ks], zeros, e_ref[pl.ds(w0, wb), ks]], axis=0)
        s_ref[...] = _dot(lhs_ref[...], rhs)

    def value_product(j, p_ref):
        vt = jnp.concatenate([vt_ref[:, tile_keys(j)], ones], axis=0)
        return _dot_nt(p_ref[...], vt)

    def tile(j, par, last_tile):
        if last_tile:
            t = i * tq + lax.broadcasted_iota(jnp.int32, (tq, tk), 0)
            causal = j * tk + lax.broadcasted_iota(jnp.int32, (tq, tk), 1) <= t
        else:
            scores(j + 1, s_refs[1 - par])
        pv_prev = value_product(jnp.maximum(j - 1, 0), p_refs[1 - par])
        for h in range(NSA_HPG):
            for b in range(tq // rb):
                rs = pl.ds(h * tq + b * rb, rb)
                s = s_refs[par][rs, :]
                if last_tile:
                    s = jnp.where(causal[b * rb:(b + 1) * rb, :], s, NEG_INF)
                m_old = m_ref[rs, :]
                m_new = jnp.maximum(m_old, jnp.max(s, axis=-1, keepdims=True))
                ps = [jnp.exp2(_bf(s[:, c * LANES:(c + 1) * LANES] - m_new)) for c in range(tk // LANES)]
                m_ref[rs, :] = m_new
                alpha_ref[rs, :] = jnp.exp2(m_old - m_new)
                p_refs[par][rs, :] = jnp.concatenate(ps, axis=1)
        acc_ref[...] = (acc_ref[...] + pv_prev) * alpha_ref[...]

    last = (i * tq + tq - 1) // tk
    scores(0, s_refs[0])

    def two_tiles(jj, carry):
        tile(2 * jj, 0, False)
        tile(2 * jj + 1, 1, False)
        return carry

    lax.fori_loop(0, last // 2, two_tiles, 0)

    @pl.when(last % 2 == 1)
    def _():
        tile(last - 1, 0, False)

    for par in range(2):
        @pl.when(last % 2 == par)
        def _():
            tile(last, par, True)
            acc = acc_ref[...] + value_product(last, p_refs[par])
            for h in range(NSA_HPG):
                a_h = acc[h * tq:(h + 1) * tq, :]
                o_ref[:, h * HEAD_DIM:(h + 1) * HEAD_DIM] = (
                    a_h[:, :HEAD_DIM] / jnp.maximum(a_h[:, HEAD_DIM:], 1e-30))


def _slc_attn(q, selb, kvt, e, tq, tk, wb):
    B, S, _ = q.shape
    G = NSA_GROUPS
    NB = selb.shape[3]
    GD = NSA_HPG * HEAD_DIM
    rows = NSA_HPG * tq
    return pl.pallas_call(
        functools.partial(_slc_attn_body, tq=tq, tk=tk, wb=wb), grid=(B, G, S // tq),
        in_specs=[pl.BlockSpec((None, tq, GD), lambda b, g, i: (b, i, g)),
                  pl.BlockSpec((None, None, tq, NB), lambda b, g, i: (b, g, i, 0)),
                  pl.BlockSpec((None, HEAD_DIM, S), lambda b, g, i: (b, g, 0)),
                  pl.BlockSpec((None, HEAD_DIM, S), lambda b, g, i: (b, G + g, 0)),
                  pl.BlockSpec(e.shape, lambda b, g, i: (0, 0))],
        out_specs=pl.BlockSpec((None, tq, GD), lambda b, g, i: (b, i, g)),
        out_shape=jax.ShapeDtypeStruct((B, S, G * GD), F32),
        scratch_shapes=[pltpu.VMEM((rows, LANES + wb), BF16), pltpu.VMEM((rows, tk), F32), pltpu.VMEM((rows, tk), F32),
                        pltpu.VMEM((rows, tk), BF16), pltpu.VMEM((rows, tk), BF16),
                        pltpu.VMEM((rows, LANES), F32), pltpu.VMEM((rows, LANES), F32),
                        pltpu.VMEM((rows, LANES), F32)],
        compiler_params=_cp("parallel", "parallel", "parallel"), name="slc_attention",
    )(q, selb, kvt, kvt, e)


def _win_attn_body(q_ref, kt_ref, vt_ref, o_ref, *, tq, span, n_sub):
    rb = SOFTMAX_ROWS
    ones = jnp.ones((LANES - HEAD_DIM, span), BF16)
    for u in range(n_sub):
        t0 = (pl.program_id(2) * n_sub + u) * tq
        start = pl.multiple_of(jnp.maximum(t0 - WINDOW, 0), LANES)
        ks = pl.ds(start, span)
        t = t0 + lax.broadcasted_iota(jnp.int32, (tq, span), 0)
        dist = t - (start + lax.broadcasted_iota(jnp.int32, (tq, span), 1))
        bias = jnp.where((dist >= 0) & (dist < WINDOW), 0.0, NEG_INF)
        kt = kt_ref[:, ks]
        vt = jnp.concatenate([vt_ref[:, ks], ones], axis=0)
        qs = _head_queries(q_ref.at[pl.ds(u * tq, tq), :])
        scores = {0: _dot(qs[0], kt)}
        for h in range(NSA_HPG):
            if h + 1 < NSA_HPG:
                scores[h + 1] = _dot(qs[h + 1], kt)
            s_h = scores.pop(h)
            blocks = []
            for b in range(tq // rb):
                s = s_h[b * rb:(b + 1) * rb, :] + bias[b * rb:(b + 1) * rb, :]
                blocks.append(jnp.exp2(_bf(s - jnp.max(s, axis=-1, keepdims=True))))
            acc = _dot_nt(jnp.concatenate(blocks, axis=0), vt)
            o_ref[pl.ds(u * tq, tq), h * HEAD_DIM:(h + 1) * HEAD_DIM] = acc[:, :HEAD_DIM] / acc[:, HEAD_DIM:]


def _win_attn(q, kvt, tq, n_sub):
    B, S, _ = q.shape
    G = NSA_GROUPS
    GD = NSA_HPG * HEAD_DIM
    span = min(tq + WINDOW, S)
    blk = tq * n_sub
    return pl.pallas_call(
        functools.partial(_win_attn_body, tq=tq, span=span, n_sub=n_sub), grid=(B, G, S // blk),
        in_specs=[pl.BlockSpec((None, blk, GD), lambda b, g, i: (b, i, g)),
                  pl.BlockSpec((None, HEAD_DIM, S), lambda b, g, i: (b, 2 * G + g, 0)),
                  pl.BlockSpec((None, HEAD_DIM, S), lambda b, g, i: (b, 3 * G + g, 0))],
        out_specs=pl.BlockSpec((None, blk, GD), lambda b, g, i: (b, i, g)),
        out_shape=jax.ShapeDtypeStruct((B, S, G * GD), F32),
        compiler_params=_cp("parallel", "parallel", "parallel"), name="win_attention",
    )(q, kvt, kvt)


def _nsa_out_body(oc_ref, os_ref, ow_ref, gt_ref, ge_ref, w_ref, x_ref, mod_ref, g_ref, o_ref):
    D = oc_ref.shape[1]
    ge = ge_ref[...]
    gx = sum(_dot(part, ge) for part in _split_bf16(gt_ref[...], 2))
    o = gx[:, 0:D] * oc_ref[...] + gx[:, D:2 * D] * os_ref[...] + gx[:, 2 * D:3 * D] * ow_ref[...]
    y = _dot(_bf(o), w_ref[...])
    o_ref[...] = x_ref[...] + mod_ref[2:3, :] * _rms(y, g_ref[...])


def _nsa_out(oc, osl, ow, gates, gexp, w, x, mod, g, ts):
    B, S, D = x.shape
    tok = lambda n: pl.BlockSpec((None, ts, n), lambda b, i: (b, i, 0))
    full = lambda a: pl.BlockSpec(a.shape, lambda b, i: (0,) * a.ndim)
    return pl.pallas_call(
        _nsa_out_body, grid=(B, S // ts),
        in_specs=[tok(D), tok(D), tok(D), tok(gates.shape[2]), full(gexp), full(w), tok(D),
                  pl.BlockSpec((None, 6, D), lambda b, i: (b, 0, 0)), full(g)],
        out_specs=tok(D), out_shape=jax.ShapeDtypeStruct((B, S, D), F32),
        compiler_params=_cp("parallel", "parallel"), name="nsa_out",
    )(oc, osl, ow, gates, gexp, w, x, mod, g)


def _importance_matrix(nc, nb):
    n_cmp = nc - CMP_RATIO + 1
    w = np.zeros((nc, nb), np.float32)
    for m in range(SLC_RATIO):
        for n in range(CMP_RATIO):
            off = m - n + CMP_RATIO - 1
            for j in range(nb):
                src = SLC_RATIO * j + off - (CMP_RATIO - 1)
                if 0 <= src < n_cmp:
                    w[src, j] += 1.0
    return jnp.asarray(w, BF16)


def _block_expand(nb, s):
    return jnp.asarray(np.arange(nb)[:, None] == (np.arange(s)[None, :] // SLC_LEN), BF16)


def _gate_expand(n_in):
    hd = NSA_HEADS * HEAD_DIM
    w = np.zeros((n_in, 3 * hd), np.float32)
    for head in range(NSA_HEADS):
        for br in range(3):
            w[head * 3 + br, br * hd + head * HEAD_DIM: br * hd + (head + 1) * HEAD_DIM] = 1.0
    return jnp.asarray(w, BF16)


def kernel(x, c, ada_w, ada_b, norm_g, mlp_up, mlp_down, rw_mu, rw_r, rw_k, rw_v, rw_o, rw_w0, rw_wa, rw_wb,
           rw_a0, rw_aa, rw_ab, rw_ga, rw_gb, rw_kk, rw_ka, rw_rk, rw_lng, rw_lnb, kv_norm_g, kv_ada_w,
           kv_ada_b, kv_w, cmp_pos_k, cmp_pos_v, cmp_k_w1, cmp_k_w2, cmp_v_w1, cmp_v_w2, nsa_wqg, nsa_wo):
    B, S, D = x.shape
    depth = ada_w.shape[0]
    n_a = rw_mu.shape[0]
    bf = _bf
    row = lambda a: a.reshape(1, -1)
    ts = min(512, S)

    c8 = jnp.pad(c, ((0, 8 - B), (0, 0)))
    mods = _ada(c8, ada_w, ada_b)[:, :B].reshape(depth, B, 6, D)
    mkv = _ada(c8, kv_ada_w[None], kv_ada_b[None])[0, :B].reshape(B, 2, D)

    GD = NSA_GROUPS * HEAD_DIM
    HD = NSA_HEADS * HEAD_DIM
    shared = None
    for layer in range(depth):
        mod = mods[layer]
        ng = norm_g[layer]
        if layer < n_a:
            i = layer
            r, k, v, lw, kk, a, g = _rw_front(
                x, mod, row(ng[0]), rw_mu[i], bf(rw_r[i]), bf(rw_k[i]), bf(rw_v[i]), bf(rw_wa[i]),
                bf(rw_wb[i]), row(rw_w0[i]), bf(rw_aa[i]), bf(rw_ab[i]), row(rw_a0[i]), bf(rw_ga[i]),
                bf(rw_gb[i]), row(rw_kk[i]), row(rw_ka[i]), min(256, S))
            y = _rw_recurrence(r, k, v, lw, kk, a, g, row(rw_rk[i]), row(rw_lng[i]), row(rw_lnb[i]),
                               min(512, S), 2)
            x = _proj_res(y, bf(rw_o[i]), x, mod, row(ng[1]), ts)
        else:
            j = layer - n_a
            wqg = nsa_wqg[j]
            wq = bf(wqg[:, :HD])
            wg = bf(jnp.pad(wqg[:, HD:], ((0, 0), (0, LANES - 3 * NSA_HEADS))))
            if shared is None:
                wc = bf(kv_w[:, :2 * GD])
                wt = bf(kv_w[:, 2 * GD:].T)
            q, gates, kvc, kvt = _nsa_proj(x, mod, mkv, row(ng[0]), row(kv_norm_g), wq, wg, wc, wt, ts)
            if shared is None:
                fk, fv = _cmp1(kvc, cmp_pos_k, cmp_pos_v, bf(cmp_k_w1), bf(cmp_v_w1), min(1024, S))
                kct, vc = _cmp2(fk, fv, bf(cmp_k_w2.T), bf(cmp_v_w2))
                shared = (kct, vc, kvt)
            kct, vc, kvt_s = shared
            nc, nb = S // CMP_STRIDE, S // SLC_LEN
            tq = min(512, S)
            oc, sel = _cmp_attn(q, kct, vc, _importance_matrix(nc, nb), tq)
            osl = _slc_attn(q, sel, kvt_s, _block_expand(nb, S), min(512, S), min(512, S), min(LANES, nb))
            ow = _win_attn(q, kvt_s, min(256, S), 2)
            x = _nsa_out(oc, osl, ow, gates, _gate_expand(gates.shape[2]), bf(nsa_wo[j]), x, mod,
                         row(ng[1]), ts)
        x = _mlp(x, mod, row(ng[2]), row(ng[3]), bf(mlp_up[layer]), bf(mlp_down[layer]), min(1024, S),
                 1024)
    return x
```

```python
import functools
import math

import numpy as np
import jax
import jax.numpy as jnp
from jax import lax
from jax.experimental import pallas as pl
from jax.experimental.pallas import tpu as pltpu

F32 = jnp.float32
BF16 = jnp.bfloat16
HI = lax.Precision.HIGHEST

NORM_EPS = 1e-6
GN_EPS = 64e-5
RWKV_HEAD = 64
HEAD_DIM = 64
NSA_GROUPS = 4
NSA_HPG = 4
NSA_HEADS = 16
CMP_LEN = 32
CMP_STRIDE = 16
CMP_RATIO = CMP_LEN // CMP_STRIDE
SLC_LEN = 64
SLC_RATIO = SLC_LEN // CMP_STRIDE
TOP_N = 16
WINDOW = 512
FORCED_SCORE = 1e4
NEG_INF = -1e30
LOG2E = math.log2(math.e)

LANES = 128
VMEM_LIMIT = 56 * 1024 * 1024
REC_CHUNK = 64
SOFTMAX_ROWS = 32
CMP_COLS = 256
SEL_BLOCKS = 64


def _cp(*sem):
    return pltpu.CompilerParams(dimension_semantics=sem, vmem_limit_bytes=VMEM_LIMIT)


def _dot(a, b, **kw):
    return jnp.dot(a, b, preferred_element_type=F32, **kw)


def _dot_nt(a, b, **kw):
    return lax.dot_general(a, b, (((1,), (1,)), ((), ())), preferred_element_type=F32, **kw)


def _dot_tn(a, b, **kw):
    return lax.dot_general(a, b, (((0,), (0,)), ((), ())), preferred_element_type=F32, **kw)


def _bf(t):
    return t.astype(BF16)


def _split_bf16(x, parts):
    out = []
    for _ in range(parts - 1):
        hi = _bf(x)
        out.append(hi)
        x = x - hi.astype(F32)
    out.append(_bf(x))
    return out


def _rms(x, g):
    return x * lax.rsqrt(jnp.mean(x * x, axis=-1, keepdims=True) + NORM_EPS) * g


def _ada_body(c_ref, w_ref, b_ref, o_ref):
    c = c_ref[...]
    o_ref[...] = _dot(c * jax.nn.sigmoid(c), w_ref[...], precision=HI) + b_ref[...]


def _ada(c8, w, b):
    L, D, N = w.shape
    tn = 1024
    return pl.pallas_call(
        _ada_body, grid=(L, N // tn),
        in_specs=[pl.BlockSpec((8, D), lambda l, j: (0, 0)),
                  pl.BlockSpec((None, D, tn), lambda l, j: (l, 0, j)),
                  pl.BlockSpec((None, 1, tn), lambda l, j: (l, 0, j))],
        out_specs=pl.BlockSpec((None, 8, tn), lambda l, j: (l, 0, j)),
        out_shape=jax.ShapeDtypeStruct((L, 8, N), F32),
        compiler_params=_cp("parallel", "parallel"), name="ada",
    )(c8, w, b.reshape(L, 1, N))


def _rw_front_body(x_ref, xp_ref, mod_ref, g_ref, mu_ref, wr, wk, wv, wa, wb, w0, aa, ab, a0, ga, gb,
                   kk_ref, ka_ref, r_o, k_o, v_o, lw_o, kk_o, a_o, g_o):
    i = pl.program_id(1)
    g = g_ref[...]
    sh, sc = mod_ref[0:1, :], mod_ref[1:2, :]
    h = _rms(x_ref[...], g) * (1 + sc) + sh
    hp = _rms(xp_ref[...], g) * (1 + sc) + sh
    hp = jnp.where(i > 0, hp[7:8, :], 0.0)
    row = lax.broadcasted_iota(jnp.int32, h.shape, 0)
    hs = jnp.where(row == 0, hp, pltpu.roll(h, 1, axis=0))
    xx = hs - h
    mu = mu_ref[...]
    xr, xw, xk, xv, xa, xg = [_bf(h + xx * mu[j:j + 1, :]) for j in range(6)]
    r = _dot(xr, wr[...])
    k = _dot(xk, wk[...])
    v = _dot(xv, wv[...])
    wl = w0[...] + _dot(_bf(jnp.tanh(_dot(xw, wa[...]))), wb[...])
    lw = -jax.nn.sigmoid(wl) * math.exp(-0.5)
    a = jax.nn.sigmoid(a0[...] + _dot(_bf(_dot(xa, aa[...])), ab[...]))
    gate = _dot(_bf(jax.nn.sigmoid(_dot(xg, ga[...]))), gb[...])
    r_o[...] = _bf(r)
    k_o[...] = _bf(k * (1 + (a - 1) * ka_ref[...]))
    v_o[...] = _bf(v)
    lw_o[...] = lw
    kk_o[...] = _bf(k * kk_ref[...])
    a_o[...] = _bf(a)
    g_o[...] = _bf(gate)


def _rw_front(x, mod, g, mu, wr, wk, wv, wa, wb, w0, aa, ab, a0, ga, gb, kk, ka, ts):
    B, S, D = x.shape
    full = lambda a: pl.BlockSpec(a.shape, lambda b, i: (0,) * a.ndim)
    tok = pl.BlockSpec((None, ts, D), lambda b, i: (b, i, 0))
    ws = [wr, wk, wv, wa, wb, w0, aa, ab, a0, ga, gb, kk, ka]
    return pl.pallas_call(
        _rw_front_body, grid=(B, S // ts),
        in_specs=[tok,
                  pl.BlockSpec((None, 8, D), lambda b, i: (b, jnp.maximum(i * (ts // 8) - 1, 0), 0)),
                  pl.BlockSpec((None, 6, D), lambda b, i: (b, 0, 0)),
                  full(g), full(mu)] + [full(w) for w in ws],
        out_specs=[tok] * 7,
        out_shape=[jax.ShapeDtypeStruct((B, S, D), F32 if j == 3 else BF16) for j in range(7)],
        compiler_params=_cp("parallel", "parallel"), name="rwkv_front",
    )(x, x, mod, g, mu, *ws)


def _rec_body(r_ref, k_ref, v_ref, lw_ref, kk_ref, a_ref, g_ref, rk_ref, lng_ref, lnb_ref, o_ref, st_ref,
              *, n_chunks, n_pairs):
    C = REC_CHUNK
    L = LANES
    C2 = 2 * C

    @pl.when(pl.program_id(2) == 0)
    def _():
        st_ref[...] = jnp.zeros_like(st_ref)

    h0 = lax.broadcasted_iota(jnp.int32, (C, L), 1) < RWKV_HEAD
    rr = lax.broadcasted_iota(jnp.int32, (C2, C2), 0)
    cc = lax.broadcasted_iota(jnp.int32, (C2, C2), 1)
    same = rr // C == cc // C
    incl = same & (rr >= cc)
    strict = same & (rr > cc)
    eye = (rr == cc).astype(F32)
    lvl_masks = [(rr // sz == cc // sz) & (rr // (sz // 2) != cc // (sz // 2))
                 for sz in (2 ** e for e in range(1, int(math.log2(C)) + 1))]
    tri_b = (lax.broadcasted_iota(jnp.int32, (C, C), 0) >= lax.broadcasted_iota(jnp.int32, (C, C), 1)).astype(BF16)
    bd_b = (lax.broadcasted_iota(jnp.int32, (L, L), 0) // RWKV_HEAD
            == lax.broadcasted_iota(jnp.int32, (L, L), 1) // RWKV_HEAD).astype(BF16)
    units = [(pr, c) for pr in range(n_pairs) for c in range(n_chunks)]

    def head_sum(t):
        return sum(_dot(p, bd_b) for p in _split_bf16(t, 2))

    def stack(t):
        zero = jnp.zeros_like(t)
        return jnp.concatenate([jnp.where(h0, t, zero), jnp.where(h0, zero, t)], axis=0)

    sl = lambda ref, pr, c: ref[pl.ds(c * C, C), pr * L:(pr + 1) * L].astype(F32)
    ld = [dict(r=sl(r_ref, pr, c), k=sl(k_ref, pr, c), v=sl(v_ref, pr, c), lw=sl(lw_ref, pr, c),
               kk=sl(kk_ref, pr, c), a=sl(a_ref, pr, c)) for pr, c in units]
    kkn = [d["kk"] / jnp.maximum(jnp.sqrt(head_sum(d["kk"] * d["kk"])), 1e-12) for d in ld]
    cl = [sum(_dot(tri_b, p) for p in _split_bf16(d["lw"], 3)) for d in ld]
    cl_end = [x[C - 1:C, :] for x in cl]
    e_neg = [jnp.exp(-x) for x in cl]
    e_end = [jnp.exp(ce - x) for x, ce in zip(cl, cl_end)]
    rt2 = [_bf(stack(d["r"] * jnp.exp(x))) for d, x in zip(ld, cl)]
    at2 = [_bf(stack(-kn * jnp.exp(x - d["lw"]))) for d, kn, x in zip(ld, kkn, cl)]
    bt2 = [_bf(stack(kn * d["a"] * e)) for d, kn, e in zip(ld, kkn, e_neg)]
    kt2 = [_bf(stack(d["k"] * e)) for d, e in zip(ld, e_neg)]
    bend2 = [_bf(stack(kn * d["a"] * e)) for d, kn, e in zip(ld, kkn, e_end)]
    kend2 = [_bf(stack(d["k"] * e)) for d, e in zip(ld, e_end)]
    v2 = [_bf(stack(d["v"])) for d in ld]
    decay = [jnp.exp(x) for x in cl_end]

    gm = [_dot_nt(jnp.concatenate([a_, r_], axis=0), jnp.concatenate([b_, k_], axis=0))
          for a_, r_, b_, k_ in zip(at2, rt2, bt2, kt2)]
    a_ab = [jnp.where(strict, g_[:C2, :C2], 0.0) for g_ in gm]
    a_ak = [_bf(jnp.where(strict, g_[:C2, C2:], 0.0)) for g_ in gm]
    a_rb = [_bf(jnp.where(incl, g_[C2:, :C2], 0.0)) for g_ in gm]
    a_rk = [_bf(jnp.where(incl, g_[C2:, C2:], 0.0)) for g_ in gm]

    tm = [eye + jnp.where(lvl_masks[0], a_, 0.0) for a_ in a_ab]
    for lm in lvl_masks[1:]:
        off = [_bf(jnp.where(lm, a_, 0.0)) for a_ in a_ab]
        half = [_bf(_dot(_bf(t_), o_)) for t_, o_ in zip(tm, off)]
        tm = [t_ + _dot(h_, _bf(t_)) for t_, h_ in zip(tm, half)]
    tm = [_bf(t_) for t_ in tm]

    w2 = [_bf(_dot(t_, a_)) for t_, a_ in zip(tm, at2)]
    akv = [_bf(_dot(a_, v_)) for a_, v_ in zip(a_ak, v2)]
    uh2 = [_bf(_dot(t_, x_)) for t_, x_ in zip(tm, akv)]
    rw2 = [_bf(r_.astype(F32) + _dot(a_, w_)) for r_, a_, w_ in zip(rt2, a_rb, w2)]
    yh2 = [_dot(a_, u_) + _dot(b_, v_) for a_, u_, b_, v_ in zip(a_rb, uh2, a_rk, v2)]
    ec = [_bf(_dot_tn(w_, b_)) for w_, b_ in zip(w2, bend2)]
    qc = [_dot_tn(jnp.concatenate([u_, v_], axis=0), jnp.concatenate([b_, k_], axis=0))
          for u_, v_, b_, k_ in zip(uh2, v2, bend2, kend2)]

    states = [st_ref[pr] for pr in range(n_pairs)]
    y2 = [None] * len(units)
    for c in range(n_chunks):
        for pr in range(n_pairs):
            u = pr * n_chunks + c
            n = states[pr]
            n_b = _bf(n)
            y2[u] = _dot_nt(rw2[u], n_b) + yh2[u]
            states[pr] = n * decay[u] + _dot(n_b, ec[u]) + qc[u]
    for pr in range(n_pairs):
        st_ref[pr] = states[pr]

    for u, (pr, c) in enumerate(units):
        d = ld[u]
        lanes = slice(pr * L, (pr + 1) * L)
        rk, lng, lnb = rk_ref[:, lanes], lng_ref[:, lanes], lnb_ref[:, lanes]
        y = y2[u][:C] + y2[u][C:]
        mean = head_sum(y) * (1.0 / RWKV_HEAD)
        yc = y - mean
        var = head_sum(yc * yc) * (1.0 / RWKV_HEAD)
        yn = yc * lax.rsqrt(var + GN_EPS) * lng + lnb
        bonus = head_sum(d["r"] * d["k"] * rk) * d["v"]
        o_ref[pl.ds(c * C, C), lanes] = _bf((yn + bonus) * sl(g_ref, pr, c))


def _rw_recurrence(r, k, v, lw, kk, a, g, rk, lng, lnb, tt, n_pairs):
    B, S, D = r.shape
    width = n_pairs * LANES
    tok = pl.BlockSpec((None, tt, width), lambda b, hp, t: (b, t, hp))
    vec = pl.BlockSpec((1, width), lambda b, hp, t: (0, hp))
    return pl.pallas_call(
        functools.partial(_rec_body, n_chunks=tt // REC_CHUNK, n_pairs=n_pairs),
        grid=(B, D // width, S // tt),
        in_specs=[tok] * 7 + [vec] * 3,
        out_specs=tok,
        out_shape=jax.ShapeDtypeStruct((B, S, D), BF16),
        scratch_shapes=[pltpu.VMEM((n_pairs, LANES, LANES), F32)],
        compiler_params=_cp("parallel", "parallel", "arbitrary"), name="rwkv_recurrence",
    )(r, k, v, lw, kk, a, g, rk, lng, lnb)


def _proj_res_body(y_ref, w_ref, x_ref, mod_ref, g_ref, o_ref):
    y = _dot(y_ref[...], w_ref[...])
    o_ref[...] = x_ref[...] + mod_ref[2:3, :] * _rms(y, g_ref[...])


def _proj_res(y, w, x, mod, g, ts):
    B, S, D = x.shape
    tok = pl.BlockSpec((None, ts, D), lambda b, i: (b, i, 0))
    return pl.pallas_call(
        _proj_res_body, grid=(B, S // ts),
        in_specs=[tok, pl.BlockSpec(w.shape, lambda b, i: (0, 0)), tok,
                  pl.BlockSpec((None, 6, D), lambda b, i: (b, 0, 0)),
                  pl.BlockSpec(g.shape, lambda b, i: (0, 0))],
        out_specs=tok, out_shape=jax.ShapeDtypeStruct((B, S, D), F32),
        compiler_params=_cp("parallel", "parallel"), name="proj_residual",
    )(y, w, x, mod, g)


def _mlp_body(x_ref, mod_ref, g2_ref, g3_ref, up_ref, dn_ref, o_ref, h_ref, acc_ref):
    f = pl.program_id(2)

    @pl.when(f == 0)
    def _():
        h = _rms(x_ref[...], g2_ref[...]) * (1 + mod_ref[4:5, :]) + mod_ref[3:4, :]
        h_ref[...] = _bf(h)
        acc_ref[...] = jnp.zeros_like(acc_ref)

    z = jnp.maximum(_dot(h_ref[...], up_ref[...]), 0.0)
    acc_ref[...] += _dot(_bf(z * z), dn_ref[...])

    @pl.when(f == pl.num_programs(2) - 1)
    def _():
        o_ref[...] = x_ref[...] + mod_ref[5:6, :] * _rms(acc_ref[...], g3_ref[...])


def _mlp(x, mod, g2, g3, up, dn, ts, tf):
    B, S, D = x.shape
    F = up.shape[1]
    tok = pl.BlockSpec((None, ts, D), lambda b, i, f: (b, i, 0))
    vec = pl.BlockSpec((1, D), lambda b, i, f: (0, 0))
    return pl.pallas_call(
        _mlp_body, grid=(B, S // ts, F // tf),
        in_specs=[tok, pl.BlockSpec((None, 6, D), lambda b, i, f: (b, 0, 0)), vec, vec,
                  pl.BlockSpec((D, tf), lambda b, i, f: (0, f)),
                  pl.BlockSpec((tf, D), lambda b, i, f: (f, 0))],
        out_specs=tok, out_shape=jax.ShapeDtypeStruct((B, S, D), F32),
        scratch_shapes=[pltpu.VMEM((ts, D), BF16), pltpu.VMEM((ts, D), F32)],
        compiler_params=_cp("parallel", "parallel", "arbitrary"), name="mlp",
    )(x, mod, g2, g3, up, dn)


def _nsa_proj_body(x_ref, mod_ref, mkv_ref, g_ref, gkv_ref, wq_ref, wg_ref, wc_ref, wt_ref,
                   q_o, gt_o, c_o, t_o):
    x = x_ref[...]
    xn = x * lax.rsqrt(jnp.mean(x * x, axis=-1, keepdims=True) + NORM_EPS)
    h = _bf((xn * g_ref[...]) * (1 + mod_ref[1:2, :]) + mod_ref[0:1, :])
    hkv = _bf((xn * gkv_ref[...]) * (1 + mkv_ref[1:2, :]) + mkv_ref[0:1, :])
    q_o[...] = _bf(_dot(h, wq_ref[...]) * (HEAD_DIM ** -0.5 * LOG2E))
    gt_o[...] = jax.nn.sigmoid(_dot(h, wg_ref[...]))
    c_o[...] = _dot(hkv, wc_ref[...])
    t_o[...] = _bf(_dot_nt(wt_ref[...], hkv))


def _nsa_proj(x, mod, mkv, g, gkv, wq, wg, wc, wt, ts):
    B, S, D = x.shape
    tok = lambda n: pl.BlockSpec((None, ts, n), lambda b, i: (b, i, 0))
    full = lambda a: pl.BlockSpec(a.shape, lambda b, i: (0,) * a.ndim)
    return pl.pallas_call(
        _nsa_proj_body, grid=(B, S // ts),
        in_specs=[tok(D), pl.BlockSpec((None, 6, D), lambda b, i: (b, 0, 0)),
                  pl.BlockSpec((None, 2, D), lambda b, i: (b, 0, 0)),
                  full(g), full(gkv), full(wq), full(wg), full(wc), full(wt)],
        out_specs=[tok(wq.shape[1]), tok(wg.shape[1]), tok(wc.shape[1]),
                   pl.BlockSpec((None, wt.shape[0], ts), lambda b, i: (b, 0, i))],
        out_shape=[jax.ShapeDtypeStruct((B, S, wq.shape[1]), BF16),
                   jax.ShapeDtypeStruct((B, S, wg.shape[1]), F32),
                   jax.ShapeDtypeStruct((B, S, wc.shape[1]), F32),
                   jax.ShapeDtypeStruct((B, wt.shape[0], S), BF16)],
        compiler_params=_cp("parallel", "parallel"), name="nsa_proj",
    )(x, mod, mkv, g, gkv, wq, wg, wc, wt)


def _cmp1_body(k0_ref, k1_ref, v0_ref, v1_ref, pk_ref, pv_ref, wk_ref, wv_ref, fk_o, fv_o, *, n_chunk):
    hid = wk_ref.shape[1]
    gpr = LANES // HEAD_DIM
    for srcs, pos, w1, out in (((k0_ref, k1_ref), pk_ref, wk_ref, fk_o), ((v0_ref, v1_ref), pv_ref, wv_ref, fv_o)):
        acc = [[jnp.zeros((n_chunk, hid), F32) for _ in range(CMP_RATIO)] for _ in range(NSA_GROUPS)]
        for l in range(CMP_STRIDE):
            for si, src in enumerate(srcs):
                rows = src[pl.ds(l, n_chunk, stride=CMP_STRIDE), :]
                for gi in range(gpr):
                    g = si * gpr + gi
                    t = rows[:, gi * HEAD_DIM:(gi + 1) * HEAD_DIM]
                    for half in range(CMP_RATIO):
                        p = half * CMP_STRIDE + l
                        lhs = _bf(t + pos[p:p + 1, :])
                        acc[g][half] = acc[g][half] + _dot(lhs, w1[p * HEAD_DIM:(p + 1) * HEAD_DIM, :])
        for g in range(NSA_GROUPS):
            for half in range(CMP_RATIO):
                out[g, :, half * hid:(half + 1) * hid] = acc[g][half]


def _cmp1(kvc, pos_k, pos_v, w1k, w1v, ts):
    B, S, _ = kvc.shape
    G = NSA_GROUPS
    hid = w1k.shape[1]
    nch = ts // CMP_STRIDE
    full = lambda a: pl.BlockSpec(a.shape, lambda b, i: (0,) * a.ndim)
    src = lambda j: pl.BlockSpec((None, ts, LANES), lambda b, i: (b, i, j))
    outs = pl.BlockSpec((None, G, nch, CMP_RATIO * hid), lambda b, i: (b, 0, i, 0))
    return pl.pallas_call(
        functools.partial(_cmp1_body, n_chunk=nch), grid=(B, S // ts),
        in_specs=[src(0), src(1), src(2), src(3), full(pos_k), full(pos_v), full(w1k), full(w1v)],
        out_specs=[outs, outs],
        out_shape=[jax.ShapeDtypeStruct((B, G, S // CMP_STRIDE, CMP_RATIO * hid), F32)] * 2,
        compiler_params=_cp("parallel", "parallel"), name="cmp_stage1",
    )(kvc, kvc, kvc, kvc, pos_k, pos_v, w1k, w1v)


def _gelu_tanh(x):
    return 0.5 * x * (1.0 + jnp.tanh(math.sqrt(2.0 / math.pi) * (x + 0.044715 * (x * x * x))))


def _cmp2_body(fk_ref, fv_ref, w2kt_ref, w2v_ref, kt_o, v_o):
    nc = fk_ref.shape[0]
    hid = w2v_ref.shape[0]
    row = lax.broadcasted_iota(jnp.int32, (nc, hid), 0)

    def hidden(f):
        nxt = jnp.where(row == nc - 1, 0.0, pltpu.roll(f[:, hid:2 * hid], nc - 1, axis=0))
        return _bf(_gelu_tanh(f[:, 0:hid] + nxt))

    kt_o[...] = _bf(_dot_nt(w2kt_ref[...], hidden(fk_ref[...])))
    v_o[...] = _bf(_dot(hidden(fv_ref[...]), w2v_ref[...]))


def _cmp2(fk, fv, w2kt, w2v):
    B, G, NC, H2 = fk.shape
    fin = pl.BlockSpec((None, None, NC, H2), lambda b, g: (b, g, 0, 0))
    return pl.pallas_call(
        _cmp2_body, grid=(B, G),
        in_specs=[fin, fin, pl.BlockSpec(w2kt.shape, lambda b, g: (0, 0)),
                  pl.BlockSpec(w2v.shape, lambda b, g: (0, 0))],
        out_specs=[pl.BlockSpec((None, None, HEAD_DIM, NC), lambda b, g: (b, g, 0, 0)),
                   pl.BlockSpec((None, None, NC, HEAD_DIM), lambda b, g: (b, g, 0, 0))],
        out_shape=[jax.ShapeDtypeStruct((B, G, HEAD_DIM, NC), BF16),
                   jax.ShapeDtypeStruct((B, G, NC, HEAD_DIM), BF16)],
        compiler_params=_cp("parallel", "parallel"), name="cmp_stage2",
    )(fk, fv, w2kt, w2v)


def _head_queries(q_ref):
    return [q_ref[:, h * HEAD_DIM:(h + 1) * HEAD_DIM] for h in range(NSA_HPG)]


def _softmax_attend(q_ref, kt, bias, pv, o_ref, tq, want_psum):
    rb = SOFTMAX_ROWS
    n_blk = tq // rb
    qs = _head_queries(q_ref)
    scores = {0: _dot(qs[0], kt)}
    tot = [None] * n_blk
    for h in range(NSA_HPG):
        if h + 1 < NSA_HPG:
            scores[h + 1] = _dot(qs[h + 1], kt)
        s_h = scores.pop(h)
        blocks = []
        for b in range(n_blk):
            s = s_h[b * rb:(b + 1) * rb, :] + bias[b * rb:(b + 1) * rb, :]
            m = jnp.maximum(jnp.max(s, axis=-1, keepdims=True), 0.1 * NEG_INF)
            e = jnp.exp2(s - m)
            p = e * (1.0 / jnp.maximum(jnp.sum(e, axis=-1, keepdims=True), 1e-30))
            blocks.append(_bf(p))
            if want_psum:
                tot[b] = p if tot[b] is None else tot[b] + p
        o_ref[:, h * HEAD_DIM:(h + 1) * HEAD_DIM] = pv(jnp.concatenate(blocks, axis=0)).astype(o_ref.dtype)
    return tot


def _select_blocks(imp, t0):
    nb, tq = imp.shape
    blk = lax.broadcasted_iota(jnp.int32, (nb, tq), 0)
    cur = (t0 + lax.broadcasted_iota(jnp.int32, (nb, tq), 1)) // SLC_LEN
    forced = (blk == 0) | (blk == cur) | (blk == cur - 1)
    imp = jnp.where(forced, FORCED_SCORE, imp)
    imp = jnp.where(blk > cur, -1.0, imp)
    bits = lax.bitcast_convert_type(imp, jnp.int32)
    k_sel = float(min(TOP_N, nb))
    count = lambda mask: jnp.sum(jnp.where(mask, 1.0, 0.0), axis=0, keepdims=True)

    def refine(it, thr):
        cand = thr | jnp.left_shift(jnp.int32(1), 30 - it)
        return jnp.where(count(bits >= cand) >= k_sel, cand, thr)

    thr = lax.fori_loop(0, 31, refine, jnp.zeros((1, tq), jnp.int32))
    above = bits > thr
    tied = bits == thr
    tri = (lax.broadcasted_iota(jnp.int32, (nb, nb), 0) >= lax.broadcasted_iota(jnp.int32, (nb, nb), 1))
    rank = _dot(tri.astype(BF16), jnp.where(tied, 1.0, 0.0).astype(BF16))
    return above | (tied & (rank <= k_sel - count(above)))


def _cmp_attn_body(q_ref, kt_ref, v_ref, wimp_ref, o_ref, sel_ref, imp_ref, *, tq):
    i = pl.program_id(2)
    nc = kt_ref.shape[1]
    nb = wimp_ref.shape[1]
    step = min(CMP_COLS, nc)
    n_need = (i * tq + tq - CMP_LEN) // CMP_STRIDE + 1
    for ncv in range(step, nc + 1, step):
        @pl.when((n_need > ncv - step) & ((n_need <= ncv) | (ncv == nc)))
        def _():
            t = i * tq + lax.broadcasted_iota(jnp.int32, (tq, ncv), 0)
            cmp_end = lax.broadcasted_iota(jnp.int32, (tq, ncv), 1) * CMP_STRIDE + (CMP_LEN - 1)
            bias = jnp.where(cmp_end <= t, 0.0, NEG_INF)
            v = v_ref[0:ncv, :]
            tot = _softmax_attend(q_ref, kt_ref[:, 0:ncv], bias, lambda p: _dot(p, v), o_ref, tq, True)
            wimp = wimp_ref[0:ncv, :]
            imp_ref[...] = sum(_dot(part, wimp) for part in _split_bf16(jnp.concatenate(tot, axis=0), 3))

    bstep = min(SEL_BLOCKS, nb)
    b_need = (i * tq + tq - 1) // SLC_LEN + 1
    for nbv in range(bstep, nb + 1, bstep):
        @pl.when((b_need > nbv - bstep) & ((b_need <= nbv) | (nbv == nb)))
        def _():
            sel = _select_blocks(imp_ref[:, 0:nbv].T, i * tq)
            sel_ref[:, 0:nbv] = _bf(jnp.where(sel, 0.0, NEG_INF).T)
            if nbv < nb:
                sel_ref[:, nbv:nb] = jnp.full((tq, nb - nbv), NEG_INF, BF16)


def _cmp_attn(q, kct, vc, wimp, tq):
    B, S, _ = q.shape
    G = NSA_GROUPS
    NC = kct.shape[3]
    NB = wimp.shape[1]
    GD = NSA_HPG * HEAD_DIM
    return pl.pallas_call(
        functools.partial(_cmp_attn_body, tq=tq), grid=(B, G, S // tq),
        in_specs=[pl.BlockSpec((None, tq, GD), lambda b, g, i: (b, i, g)),
                  pl.BlockSpec((None, None, HEAD_DIM, NC), lambda b, g, i: (b, g, 0, 0)),
                  pl.BlockSpec((None, None, NC, HEAD_DIM), lambda b, g, i: (b, g, 0, 0)),
                  pl.BlockSpec(wimp.shape, lambda b, g, i: (0, 0))],
        out_specs=[pl.BlockSpec((None, tq, GD), lambda b, g, i: (b, i, g)),
                   pl.BlockSpec((None, None, tq, NB), lambda b, g, i: (b, g, i, 0))],
        out_shape=[jax.ShapeDtypeStruct((B, S, G * GD), BF16),
                   jax.ShapeDtypeStruct((B, G, S, NB), BF16)],
        scratch_shapes=[pltpu.VMEM((tq, NB), F32)],
        compiler_params=_cp("parallel", "parallel", "parallel"), name="cmp_attention_topk",
    )(q, kct, vc, wimp)


def _slc_attn_body(q_ref, selb_ref, kt_ref, vt_ref, e_ref, o_ref, lhs_ref, s0_ref, s1_ref, p0_ref, p1_ref,
                   alpha_ref, m_ref, acc_ref, *, tq, tk, wb):
    i = pl.program_id(2)
    rb = SOFTMAX_ROWS
    rows = NSA_HPG * tq
    tiles_per_window = wb * SLC_LEN // tk
    for h in range(NSA_HPG):
        lhs_ref[pl.ds(h * tq, tq), 0:HEAD_DIM] = q_ref[:, h * HEAD_DIM:(h + 1) * HEAD_DIM]
    lhs_ref[:, HEAD_DIM:LANES] = jnp.zeros((rows, LANES - HEAD_DIM), BF16)
    m_ref[...] = jnp.full(m_ref.shape, NEG_INF, F32)
    acc_ref[...] = jnp.zeros_like(acc_ref)
    s_refs = (s0_ref, s1_ref)
    p_refs = (p0_ref, p1_ref)
    p1_ref[...] = jnp.zeros((rows, tk), BF16)
    ones = jnp.ones((LANES - HEAD_DIM, tk), BF16)
    zeros = jnp.zeros((LANES - HEAD_DIM, tk), BF16)
    tile_keys = lambda j: pl.ds(pl.multiple_of(j * tk, tk), tk)

    def set_window(w):
        win = selb_ref[:, pl.ds(pl.multiple_of(w * wb, wb), wb)]
        for h in range(NSA_HPG):
            lhs_ref[pl.ds(h * tq, tq), LANES:LANES + wb] = win

    def scores(j, s_ref):
        @pl.when(j % tiles_per_window == 0)
        def _():
            set_window(j // tiles_per_window)
        ks = tile_keys(j)
        w0 = pl.multiple_of((j // tiles_per_window) * wb, wb)
        rhs = jnp.concatenate([kt_ref[:, ks], zeros, e_ref[pl.ds(w0, wb), ks]], axis=0)
        s_ref[...] = _dot(lhs_ref[...], rhs)

    def value_product(j, p_ref):
        vt = jnp.concatenate([vt_ref[:, tile_keys(j)], ones], axis=0)
        return _dot_nt(p_ref[...], vt)

    def tile(j, par, last_tile):
        if last_tile:
            t = i * tq + lax.broadcasted_iota(jnp.int32, (tq, tk), 0)
            causal = j * tk + lax.broadcasted_iota(jnp.int32, (tq, tk), 1) <= t
        else:
            scores(j + 1, s_refs[1 - par])
        pv_prev = value_product(jnp.maximum(j - 1, 0), p_refs[1 - par])
        for h in range(NSA_HPG):
            for b in range(tq // rb):
                rs = pl.ds(h * tq + b * rb, rb)
                s = s_refs[par][rs, :]
                if last_tile:
                    s = jnp.where(causal[b * rb:(b + 1) * rb, :], s, NEG_INF)
                m_old = m_ref[rs, :]
                m_new = jnp.maximum(m_old, jnp.max(s, axis=-1, keepdims=True))
                ps = [jnp.exp2(_bf(s[:, c * LANES:(c + 1) * LANES] - m_new)) for c in range(tk // LANES)]
                m_ref[rs, :] = m_new
                alpha_ref[rs, :] = jnp.exp2(m_old - m_new)
                p_refs[par][rs, :] = jnp.concatenate(ps, axis=1)
        acc_ref[...] = (acc_ref[...] + pv_prev) * alpha_ref[...]

    last = (i * tq + tq - 1) // tk
    scores(0, s_refs[0])

    def two_tiles(jj, carry):
        tile(2 * jj, 0, False)
        tile(2 * jj + 1, 1, False)
        return carry

    lax.fori_loop(0, last // 2, two_tiles, 0)

    @pl.when(last % 2 == 1)
    def _():
        tile(last - 1, 0, False)

    for par in range(2):
        @pl.when(last % 2 == par)
        def _():
            tile(last, par, True)
            acc = acc_ref[...] + value_product(last, p_refs[par])
            for h in range(NSA_HPG):
                a_h = acc[h * tq:(h + 1) * tq, :]
                o_ref[:, h * HEAD_DIM:(h + 1) * HEAD_DIM] = _bf(
                    a_h[:, :HEAD_DIM] / jnp.maximum(a_h[:, HEAD_DIM:], 1e-30))


def _slc_attn(q, selb, kvt, e, tq, tk, wb):
    B, S, _ = q.shape
    G = NSA_GROUPS
    NB = selb.shape[3]
    GD = NSA_HPG * HEAD_DIM
    rows = NSA_HPG * tq
    return pl.pallas_call(
        functools.partial(_slc_attn_body, tq=tq, tk=tk, wb=wb), grid=(B, G, S // tq),
        in_specs=[pl.BlockSpec((None, tq, GD), lambda b, g, i: (b, i, g)),
                  pl.BlockSpec((None, None, tq, NB), lambda b, g, i: (b, g, i, 0)),
                  pl.BlockSpec((None, HEAD_DIM, S), lambda b, g, i: (b, g, 0)),
                  pl.BlockSpec((None, HEAD_DIM, S), lambda b, g, i: (b, G + g, 0)),
                  pl.BlockSpec(e.shape, lambda b, g, i: (0, 0))],
        out_specs=pl.BlockSpec((None, tq, GD), lambda b, g, i: (b, i, g)),
        out_shape=jax.ShapeDtypeStruct((B, S, G * GD), BF16),
        scratch_shapes=[pltpu.VMEM((rows, LANES + wb), BF16), pltpu.VMEM((rows, tk), F32), pltpu.VMEM((rows, tk), F32),
                        pltpu.VMEM((rows, tk), BF16), pltpu.VMEM((rows, tk), BF16),
                        pltpu.VMEM((rows, LANES), F32), pltpu.VMEM((rows, LANES), F32),
                        pltpu.VMEM((rows, LANES), F32)],
        compiler_params=_cp("parallel", "parallel", "parallel"), name="slc_attention",
    )(q, selb, kvt, kvt, e)


def _win_attn_body(q_ref, kt_ref, vt_ref, o_ref, *, tq, span, n_sub):
    rb = SOFTMAX_ROWS
    ones = jnp.ones((LANES - HEAD_DIM, span), BF16)
    for u in range(n_sub):
        t0 = (pl.program_id(2) * n_sub + u) * tq
        start = pl.multiple_of(jnp.maximum(t0 - WINDOW, 0), LANES)
        ks = pl.ds(start, span)
        t = t0 + lax.broadcasted_iota(jnp.int32, (tq, span), 0)
        dist = t - (start + lax.broadcasted_iota(jnp.int32, (tq, span), 1))
        bias = jnp.where((dist >= 0) & (dist < WINDOW), 0.0, NEG_INF)
        kt = kt_ref[:, ks]
        vt = jnp.concatenate([vt_ref[:, ks], ones], axis=0)
        qs = _head_queries(q_ref.at[pl.ds(u * tq, tq), :])
        scores = {0: _dot(qs[0], kt)}
        for h in range(NSA_HPG):
            if h + 1 < NSA_HPG:
                scores[h + 1] = _dot(qs[h + 1], kt)
            s_h = scores.pop(h)
            blocks = []
            for b in range(tq // rb):
                s = s_h[b * rb:(b + 1) * rb, :] + bias[b * rb:(b + 1) * rb, :]
                blocks.append(jnp.exp2(_bf(s - jnp.max(s, axis=-1, keepdims=True))))
            acc = _dot_nt(jnp.concatenate(blocks, axis=0), vt)
            o_ref[pl.ds(u * tq, tq), h * HEAD_DIM:(h + 1) * HEAD_DIM] = _bf(acc[:, :HEAD_DIM] / acc[:, HEAD_DIM:])


def _win_attn(q, kvt, tq, n_sub):
    B, S, _ = q.shape
    G = NSA_GROUPS
    GD = NSA_HPG * HEAD_DIM
    span = min(tq + WINDOW, S)
    blk = tq * n_sub
    return pl.pallas_call(
        functools.partial(_win_attn_body, tq=tq, span=span, n_sub=n_sub), grid=(B, G, S // blk),
        in_specs=[pl.BlockSpec((None, blk, GD), lambda b, g, i: (b, i, g)),
                  pl.BlockSpec((None, HEAD_DIM, S), lambda b, g, i: (b, 2 * G + g, 0)),
                  pl.BlockSpec((None, HEAD_DIM, S), lambda b, g, i: (b, 3 * G + g, 0))],
        out_specs=pl.BlockSpec((None, blk, GD), lambda b, g, i: (b, i, g)),
        out_shape=jax.ShapeDtypeStruct((B, S, G * GD), BF16),
        compiler_params=_cp("parallel", "parallel", "parallel"), name="win_attention",
    )(q, kvt, kvt)


def _nsa_out_body(oc_ref, os_ref, ow_ref, gt_ref, ge_ref, w_ref, x_ref, mod_ref, g_ref, o_ref):
    D = oc_ref.shape[1]
    ge = ge_ref[...]
    gx = sum(_dot(part, ge) for part in _split_bf16(gt_ref[...], 2))
    o = gx[:, 0:D] * oc_ref[...] + gx[:, D:2 * D] * os_ref[...] + gx[:, 2 * D:3 * D] * ow_ref[...]
    y = _dot(_bf(o), w_ref[...])
    o_ref[...] = x_ref[...] + mod_ref[2:3, :] * _rms(y, g_ref[...])


def _nsa_out(oc, osl, ow, gates, gexp, w, x, mod, g, ts):
    B, S, D = x.shape
    tok = lambda n: pl.BlockSpec((None, ts, n), lambda b, i: (b, i, 0))
    full = lambda a: pl.BlockSpec(a.shape, lambda b, i: (0,) * a.ndim)
    return pl.pallas_call(
        _nsa_out_body, grid=(B, S // ts),
        in_specs=[tok(D), tok(D), tok(D), tok(gates.shape[2]), full(gexp), full(w), tok(D),
                  pl.BlockSpec((None, 6, D), lambda b, i: (b, 0, 0)), full(g)],
        out_specs=tok(D), out_shape=jax.ShapeDtypeStruct((B, S, D), F32),
        compiler_params=_cp("parallel", "parallel"), name="nsa_out",
    )(oc, osl, ow, gates, gexp, w, x, mod, g)


def _importance_matrix(nc, nb):
    n_cmp = nc - CMP_RATIO + 1
    w = np.zeros((nc, nb), np.float32)
    for m in range(SLC_RATIO):
        for n in range(CMP_RATIO):
            off = m - n + CMP_RATIO - 1
            for j in range(nb):
                src = SLC_RATIO * j + off - (CMP_RATIO - 1)
                if 0 <= src < n_cmp:
                    w[src, j] += 1.0
    return jnp.asarray(w, BF16)


def _block_expand(nb, s):
    return jnp.asarray(np.arange(nb)[:, None] == (np.arange(s)[None, :] // SLC_LEN), BF16)


def _gate_expand(n_in):
    hd = NSA_HEADS * HEAD_DIM
    w = np.zeros((n_in, 3 * hd), np.float32)
    for head in range(NSA_HEADS):
        for br in range(3):
            w[head * 3 + br, br * hd + head * HEAD_DIM: br * hd + (head + 1) * HEAD_DIM] = 1.0
    return jnp.asarray(w, BF16)


def kernel(x, c, ada_w, ada_b, norm_g, mlp_up, mlp_down, rw_mu, rw_r, rw_k, rw_v, rw_o, rw_w0, rw_wa, rw_wb,
           rw_a0, rw_aa, rw_ab, rw_ga, rw_gb, rw_kk, rw_ka, rw_rk, rw_lng, rw_lnb, kv_norm_g, kv_ada_w,
           kv_ada_b, kv_w, cmp_pos_k, cmp_pos_v, cmp_k_w1, cmp_k_w2, cmp_v_w1, cmp_v_w2, nsa_wqg, nsa_wo):
    B, S, D = x.shape
    depth = ada_w.shape[0]
    n_a = rw_mu.shape[0]
    bf = _bf
    row = lambda a: a.reshape(1, -1)
    ts = min(512, S)

    c8 = jnp.pad(c, ((0, 8 - B), (0, 0)))
    mods = _ada(c8, ada_w, ada_b)[:, :B].reshape(depth, B, 6, D)
    mkv = _ada(c8, kv_ada_w[None], kv_ada_b[None])[0, :B].reshape(B, 2, D)

    GD = NSA_GROUPS * HEAD_DIM
    HD = NSA_HEADS * HEAD_DIM
    shared = None
    for layer in range(depth):
        mod = mods[layer]
        ng = norm_g[layer]
        if layer < n_a:
            i = layer
            r, k, v, lw, kk, a, g = _rw_front(
                x, mod, row(ng[0]), rw_mu[i], bf(rw_r[i]), bf(rw_k[i]), bf(rw_v[i]), bf(rw_wa[i]),
                bf(rw_wb[i]), row(rw_w0[i]), bf(rw_aa[i]), bf(rw_ab[i]), row(rw_a0[i]), bf(rw_ga[i]),
                bf(rw_gb[i]), row(rw_kk[i]), row(rw_ka[i]), min(256, S))
            y = _rw_recurrence(r, k, v, lw, kk, a, g, row(rw_rk[i]), row(rw_lng[i]), row(rw_lnb[i]),
                               min(512, S), 2)
            x = _proj_res(y, bf(rw_o[i]), x, mod, row(ng[1]), ts)
        else:
            j = layer - n_a
            wqg = nsa_wqg[j]
            wq = bf(wqg[:, :HD])
            wg = bf(jnp.pad(wqg[:, HD:], ((0, 0), (0, LANES - 3 * NSA_HEADS))))
            if shared is None:
                wc = bf(kv_w[:, :2 * GD])
                wt = bf(kv_w[:, 2 * GD:].T)
            q, gates, kvc, kvt = _nsa_proj(x, mod, mkv, row(ng[0]), row(kv_norm_g), wq, wg, wc, wt, ts)
            if shared is None:
                fk, fv = _cmp1(kvc, cmp_pos_k, cmp_pos_v, bf(cmp_k_w1), bf(cmp_v_w1), min(1024, S))
                kct, vc = _cmp2(fk, fv, bf(cmp_k_w2.T), bf(cmp_v_w2))
                shared = (kct, vc, kvt)
            kct, vc, kvt_s = shared
            nc, nb = S // CMP_STRIDE, S // SLC_LEN
            tq = min(512, S)
            oc, sel = _cmp_attn(q, kct, vc, _importance_matrix(nc, nb), tq)
            osl = _slc_attn(q, sel, kvt_s, _block_expand(nb, S), min(512, S), min(512, S), min(LANES, nb))
            ow = _win_attn(q, kvt_s, min(256, S), 2)
            x = _nsa_out(oc, osl, ow, gates, _gate_expand(gates.shape[2]), bf(nsa_wo[j]), x, mod,
                         row(ng[1]), ts)
        x = _mlp(x, mod, row(ng[2]), row(ng[3]), bf(mlp_up[layer]), bf(mlp_down[layer]), min(1024, S),
                 1024)
    return x
```

```python
import functools
import math

import numpy as np
import jax
import jax.numpy as jnp
from jax import lax
from jax.experimental import pallas as pl
from jax.experimental.pallas import tpu as pltpu

F32 = jnp.float32
BF16 = jnp.bfloat16
HI = lax.Precision.HIGHEST

NORM_EPS = 1e-6
GN_EPS = 64e-5
RWKV_HEAD = 64
HEAD_DIM = 64
NSA_GROUPS = 4
NSA_HPG = 4
NSA_HEADS = 16
CMP_LEN = 32
CMP_STRIDE = 16
CMP_RATIO = CMP_LEN // CMP_STRIDE
SLC_LEN = 64
SLC_RATIO = SLC_LEN // CMP_STRIDE
TOP_N = 16
WINDOW = 512
FORCED_SCORE = 1e4
NEG_INF = -1e30
LOG2E = math.log2(math.e)

LANES = 128
VMEM_LIMIT = 56 * 1024 * 1024
REC_CHUNK = 64
SOFTMAX_ROWS = 32
CMP_COLS = 256
SEL_BLOCKS = 64


def _cp(*sem):
    return pltpu.CompilerParams(dimension_semantics=sem, vmem_limit_bytes=VMEM_LIMIT)


def _dot(a, b, **kw):
    return jnp.dot(a, b, preferred_element_type=F32, **kw)


def _dot_nt(a, b, **kw):
    return lax.dot_general(a, b, (((1,), (1,)), ((), ())), preferred_element_type=F32, **kw)


def _dot_tn(a, b, **kw):
    return lax.dot_general(a, b, (((0,), (0,)), ((), ())), preferred_element_type=F32, **kw)


def _bf(t):
    return t.astype(BF16)


def _split_bf16(x, parts):
    out = []
    for _ in range(parts - 1):
        hi = _bf(x)
        out.append(hi)
        x = x - hi.astype(F32)
    out.append(_bf(x))
    return out


def _rms(x, g):
    return x * lax.rsqrt(jnp.mean(x * x, axis=-1, keepdims=True) + NORM_EPS) * g


def _ada_body(c_ref, w_ref, b_ref, o_ref):
    c = c_ref[...]
    o_ref[...] = _dot(c * jax.nn.sigmoid(c), w_ref[...], precision=HI) + b_ref[...]


def _ada(c8, w, b):
    L, D, N = w.shape
    tn = 1024
    return pl.pallas_call(
        _ada_body, grid=(L, N // tn),
        in_specs=[pl.BlockSpec((8, D), lambda l, j: (0, 0)),
                  pl.BlockSpec((None, D, tn), lambda l, j: (l, 0, j)),
                  pl.BlockSpec((None, 1, tn), lambda l, j: (l, 0, j))],
        out_specs=pl.BlockSpec((None, 8, tn), lambda l, j: (l, 0, j)),
        out_shape=jax.ShapeDtypeStruct((L, 8, N), F32),
        compiler_params=_cp("parallel", "parallel"), name="ada",
    )(c8, w, b.reshape(L, 1, N))


def _rw_front_body(x_ref, xp_ref, mod_ref, g_ref, mu_ref, wr, wk, wv, wa, wb, w0, aa, ab, a0, ga, gb,
                   kk_ref, ka_ref, r_o, k_o, v_o, lw_o, kk_o, a_o, g_o):
    i = pl.program_id(1)
    g = g_ref[...]
    sh, sc = mod_ref[0:1, :], mod_ref[1:2, :]
    h = _rms(x_ref[...], g) * (1 + sc) + sh
    hp = _rms(xp_ref[...], g) * (1 + sc) + sh
    hp = jnp.where(i > 0, hp[7:8, :], 0.0)
    row = lax.broadcasted_iota(jnp.int32, h.shape, 0)
    hs = jnp.where(row == 0, hp, pltpu.roll(h, 1, axis=0))
    xx = hs - h
    mu = mu_ref[...]
    xr, xw, xk, xv, xa, xg = [_bf(h + xx * mu[j:j + 1, :]) for j in range(6)]
    r = _dot(xr, wr[...])
    k = _dot(xk, wk[...])
    v = _dot(xv, wv[...])
    wl = w0[...] + _dot(_bf(jnp.tanh(_dot(xw, wa[...]))), wb[...])
    lw = -jax.nn.sigmoid(wl) * math.exp(-0.5)
    a = jax.nn.sigmoid(a0[...] + _dot(_bf(_dot(xa, aa[...])), ab[...]))
    gate = _dot(_bf(jax.nn.sigmoid(_dot(xg, ga[...]))), gb[...])
    r_o[...] = _bf(r)
    k_o[...] = _bf(k * (1 + (a - 1) * ka_ref[...]))
    v_o[...] = _bf(v)
    lw_o[...] = lw
    kk_o[...] = _bf(k * kk_ref[...])
    a_o[...] = _bf(a)
    g_o[...] = _bf(gate)


def _rw_front(x, mod, g, mu, wr, wk, wv, wa, wb, w0, aa, ab, a0, ga, gb, kk, ka, ts):
    B, S, D = x.shape
    full = lambda a: pl.BlockSpec(a.shape, lambda b, i: (0,) * a.ndim)
    tok = pl.BlockSpec((None, ts, D), lambda b, i: (b, i, 0))
    ws = [wr, wk, wv, wa, wb, w0, aa, ab, a0, ga, gb, kk, ka]
    return pl.pallas_call(
        _rw_front_body, grid=(B, S // ts),
        in_specs=[tok,
                  pl.BlockSpec((None, 8, D), lambda b, i: (b, jnp.maximum(i * (ts // 8) - 1, 0), 0)),
                  pl.BlockSpec((None, 6, D), lambda b, i: (b, 0, 0)),
                  full(g), full(mu)] + [full(w) for w in ws],
        out_specs=[tok] * 7,
        out_shape=[jax.ShapeDtypeStruct((B, S, D), F32 if j == 3 else BF16) for j in range(7)],
        compiler_params=_cp("parallel", "parallel"), name="rwkv_front",
    )(x, x, mod, g, mu, *ws)


def _rec_body(r_ref, k_ref, v_ref, lw_ref, kk_ref, a_ref, g_ref, rk_ref, lng_ref, lnb_ref, o_ref, st_ref,
              *, n_chunks, n_pairs):
    C = REC_CHUNK
    L = LANES
    C2 = 2 * C

    @pl.when(pl.program_id(2) == 0)
    def _():
        st_ref[...] = jnp.zeros_like(st_ref)

    h0 = lax.broadcasted_iota(jnp.int32, (C, L), 1) < RWKV_HEAD
    rr = lax.broadcasted_iota(jnp.int32, (C2, C2), 0)
    cc = lax.broadcasted_iota(jnp.int32, (C2, C2), 1)
    same = rr // C == cc // C
    incl = same & (rr >= cc)
    strict = same & (rr > cc)
    eye = (rr == cc).astype(F32)
    lvl_masks = [(rr // sz == cc // sz) & (rr // (sz // 2) != cc // (sz // 2))
                 for sz in (2 ** e for e in range(1, int(math.log2(C)) + 1))]
    tri_b = (lax.broadcasted_iota(jnp.int32, (C, C), 0) >= lax.broadcasted_iota(jnp.int32, (C, C), 1)).astype(BF16)
    bd_b = (lax.broadcasted_iota(jnp.int32, (L, L), 0) // RWKV_HEAD
            == lax.broadcasted_iota(jnp.int32, (L, L), 1) // RWKV_HEAD).astype(BF16)
    units = [(pr, c) for pr in range(n_pairs) for c in range(n_chunks)]

    def head_sums(ts):
        tot = sum(_dot(p, bd_b) for p in _split_bf16(jnp.concatenate(ts, axis=0), 2))
        return [tot[n * C:(n + 1) * C] for n in range(len(ts))]

    def stack(t):
        zero = jnp.zeros_like(t)
        return jnp.concatenate([jnp.where(h0, t, zero), jnp.where(h0, zero, t)], axis=0)

    sl = lambda ref, pr, c: ref[pl.ds(c * C, C), pr * L:(pr + 1) * L].astype(F32)
    ld = [dict(r=sl(r_ref, pr, c), k=sl(k_ref, pr, c), v=sl(v_ref, pr, c), lw=sl(lw_ref, pr, c),
               kk=sl(kk_ref, pr, c), a=sl(a_ref, pr, c)) for pr, c in units]
    kkn = [d["kk"] / jnp.maximum(jnp.sqrt(ss), 1e-12) for d, ss in zip(ld, head_sums([d["kk"] * d["kk"] for d in ld]))]
    cl = [sum(_dot(tri_b, p) for p in _split_bf16(d["lw"], 3)) for d in ld]
    cl_end = [x[C - 1:C, :] for x in cl]
    e_neg = [jnp.exp(-x) for x in cl]
    e_end = [jnp.exp(ce - x) for x, ce in zip(cl, cl_end)]
    rt2 = [_bf(stack(d["r"] * jnp.exp(x))) for d, x in zip(ld, cl)]
    at2 = [_bf(stack(-kn * jnp.exp(x - d["lw"]))) for d, kn, x in zip(ld, kkn, cl)]
    bt2 = [_bf(stack(kn * d["a"] * e)) for d, kn, e in zip(ld, kkn, e_neg)]
    kt2 = [_bf(stack(d["k"] * e)) for d, e in zip(ld, e_neg)]
    bend2 = [_bf(stack(kn * d["a"] * e)) for d, kn, e in zip(ld, kkn, e_end)]
    kend2 = [_bf(stack(d["k"] * e)) for d, e in zip(ld, e_end)]
    v2 = [_bf(stack(d["v"])) for d in ld]
    decay = [jnp.exp(x) for x in cl_end]

    gm = [_dot_nt(jnp.concatenate([a_, r_], axis=0), jnp.concatenate([b_, k_], axis=0))
          for a_, r_, b_, k_ in zip(at2, rt2, bt2, kt2)]
    a_ab = [jnp.where(strict, g_[:C2, :C2], 0.0) for g_ in gm]
    a_ak = [_bf(jnp.where(strict, g_[:C2, C2:], 0.0)) for g_ in gm]
    a_rb = [_bf(jnp.where(incl, g_[C2:, :C2], 0.0)) for g_ in gm]
    a_rk = [_bf(jnp.where(incl, g_[C2:, C2:], 0.0)) for g_ in gm]

    tm = [eye + jnp.where(lvl_masks[0], a_, 0.0) for a_ in a_ab]
    for lm in lvl_masks[1:]:
        off = [_bf(jnp.where(lm, a_, 0.0)) for a_ in a_ab]
        half = [_bf(_dot(_bf(t_), o_)) for t_, o_ in zip(tm, off)]
        tm = [t_ + _dot(h_, _bf(t_)) for t_, h_ in zip(tm, half)]
    tm = [_bf(t_) for t_ in tm]

    w2 = [_bf(_dot(t_, a_)) for t_, a_ in zip(tm, at2)]
    akv = [_bf(_dot(a_, v_)) for a_, v_ in zip(a_ak, v2)]
    uh2 = [_bf(_dot(t_, x_)) for t_, x_ in zip(tm, akv)]
    rw2 = [_bf(r_.astype(F32) + _dot(a_, w_)) for r_, a_, w_ in zip(rt2, a_rb, w2)]
    yh2 = [_dot(a_, u_) + _dot(b_, v_) for a_, u_, b_, v_ in zip(a_rb, uh2, a_rk, v2)]
    ec = [_bf(_dot_tn(w_, b_)) for w_, b_ in zip(w2, bend2)]
    qc = [_dot_tn(jnp.concatenate([u_, v_], axis=0), jnp.concatenate([b_, k_], axis=0))
          for u_, v_, b_, k_ in zip(uh2, v2, bend2, kend2)]

    states = [st_ref[pr] for pr in range(n_pairs)]
    y2 = [None] * len(units)
    for c in range(n_chunks):
        for pr in range(n_pairs):
            u = pr * n_chunks + c
            n = states[pr]
            n_b = _bf(n)
            y2[u] = _dot_nt(rw2[u], n_b) + yh2[u]
            states[pr] = n * decay[u] + _dot(n_b, ec[u]) + qc[u]
    for pr in range(n_pairs):
        st_ref[pr] = states[pr]

    lanes = [slice(pr * L, (pr + 1) * L) for pr, _ in units]
    ys = [y_[:C] + y_[C:] for y_ in y2]
    ycs = [y - m * (1.0 / RWKV_HEAD) for y, m in zip(ys, head_sums(ys))]
    vars_ = head_sums([yc * yc for yc in ycs])
    rks = head_sums([d["r"] * d["k"] * rk_ref[:, ln] for d, ln in zip(ld, lanes)])
    for u, (pr, c) in enumerate(units):
        ln = lanes[u]
        yn = ycs[u] * lax.rsqrt(vars_[u] * (1.0 / RWKV_HEAD) + GN_EPS) * lng_ref[:, ln] + lnb_ref[:, ln]
        o_ref[pl.ds(c * C, C), ln] = _bf((yn + rks[u] * ld[u]["v"]) * sl(g_ref, pr, c))


def _rw_recurrence(r, k, v, lw, kk, a, g, rk, lng, lnb, tt, n_pairs):
    B, S, D = r.shape
    width = n_pairs * LANES
    tok = pl.BlockSpec((None, tt, width), lambda b, hp, t: (b, t, hp))
    vec = pl.BlockSpec((1, width), lambda b, hp, t: (0, hp))
    return pl.pallas_call(
        functools.partial(_rec_body, n_chunks=tt // REC_CHUNK, n_pairs=n_pairs),
        grid=(B, D // width, S // tt),
        in_specs=[tok] * 7 + [vec] * 3,
        out_specs=tok,
        out_shape=jax.ShapeDtypeStruct((B, S, D), BF16),
        scratch_shapes=[pltpu.VMEM((n_pairs, LANES, LANES), F32)],
        compiler_params=_cp("parallel", "parallel", "arbitrary"), name="rwkv_recurrence",
    )(r, k, v, lw, kk, a, g, rk, lng, lnb)


def _proj_res_body(y_ref, w_ref, x_ref, mod_ref, g_ref, o_ref):
    y = _dot(y_ref[...], w_ref[...])
    o_ref[...] = x_ref[...] + mod_ref[2:3, :] * _rms(y, g_ref[...])


def _proj_res(y, w, x, mod, g, ts):
    B, S, D = x.shape
    tok = pl.BlockSpec((None, ts, D), lambda b, i: (b, i, 0))
    return pl.pallas_call(
        _proj_res_body, grid=(B, S // ts),
        in_specs=[tok, pl.BlockSpec(w.shape, lambda b, i: (0, 0)), tok,
                  pl.BlockSpec((None, 6, D), lambda b, i: (b, 0, 0)),
                  pl.BlockSpec(g.shape, lambda b, i: (0, 0))],
        out_specs=tok, out_shape=jax.ShapeDtypeStruct((B, S, D), F32),
        compiler_params=_cp("parallel", "parallel"), name="proj_residual",
    )(y, w, x, mod, g)


def _mlp_body(x_ref, mod_ref, g2_ref, g3_ref, up_ref, dn_ref, o_ref, h_ref, acc_ref):
    f = pl.program_id(2)

    @pl.when(f == 0)
    def _():
        h = _rms(x_ref[...], g2_ref[...]) * (1 + mod_ref[4:5, :]) + mod_ref[3:4, :]
        h_ref[...] = _bf(h)
        acc_ref[...] = jnp.zeros_like(acc_ref)

    z = jnp.maximum(_dot(h_ref[...], up_ref[...]), 0.0)
    acc_ref[...] += _dot(_bf(z * z), dn_ref[...])

    @pl.when(f == pl.num_programs(2) - 1)
    def _():
        o_ref[...] = x_ref[...] + mod_ref[5:6, :] * _rms(acc_ref[...], g3_ref[...])


def _mlp(x, mod, g2, g3, up, dn, ts, tf):
    B, S, D = x.shape
    F = up.shape[1]
    tok = pl.BlockSpec((None, ts, D), lambda b, i, f: (b, i, 0))
    vec = pl.BlockSpec((1, D), lambda b, i, f: (0, 0))
    return pl.pallas_call(
        _mlp_body, grid=(B, S // ts, F // tf),
        in_specs=[tok, pl.BlockSpec((None, 6, D), lambda b, i, f: (b, 0, 0)), vec, vec,
                  pl.BlockSpec((D, tf), lambda b, i, f: (0, f)),
                  pl.BlockSpec((tf, D), lambda b, i, f: (f, 0))],
        out_specs=tok, out_shape=jax.ShapeDtypeStruct((B, S, D), F32),
        scratch_shapes=[pltpu.VMEM((ts, D), BF16), pltpu.VMEM((ts, D), F32)],
        compiler_params=_cp("parallel", "parallel", "arbitrary"), name="mlp",
    )(x, mod, g2, g3, up, dn)


def _nsa_proj_body(x_ref, mod_ref, mkv_ref, g_ref, gkv_ref, wq_ref, wg_ref, wc_ref, wt_ref,
                   q_o, gt_o, c_o, t_o):
    x = x_ref[...]
    xn = x * lax.rsqrt(jnp.mean(x * x, axis=-1, keepdims=True) + NORM_EPS)
    h = _bf((xn * g_ref[...]) * (1 + mod_ref[1:2, :]) + mod_ref[0:1, :])
    hkv = _bf((xn * gkv_ref[...]) * (1 + mkv_ref[1:2, :]) + mkv_ref[0:1, :])
    q_o[...] = _bf(_dot(h, wq_ref[...]) * (HEAD_DIM ** -0.5 * LOG2E))
    gt_o[...] = jax.nn.sigmoid(_dot(h, wg_ref[...]))
    c_o[...] = _dot(hkv, wc_ref[...])
    t_o[...] = _bf(_dot_nt(wt_ref[...], hkv))


def _nsa_proj(x, mod, mkv, g, gkv, wq, wg, wc, wt, ts):
    B, S, D = x.shape
    tok = lambda n: pl.BlockSpec((None, ts, n), lambda b, i: (b, i, 0))
    full = lambda a: pl.BlockSpec(a.shape, lambda b, i: (0,) * a.ndim)
    return pl.pallas_call(
        _nsa_proj_body, grid=(B, S // ts),
        in_specs=[tok(D), pl.BlockSpec((None, 6, D), lambda b, i: (b, 0, 0)),
                  pl.BlockSpec((None, 2, D), lambda b, i: (b, 0, 0)),
                  full(g), full(gkv), full(wq), full(wg), full(wc), full(wt)],
        out_specs=[tok(wq.shape[1]), tok(wg.shape[1]), tok(wc.shape[1]),
                   pl.BlockSpec((None, wt.shape[0], ts), lambda b, i: (b, 0, i))],
        out_shape=[jax.ShapeDtypeStruct((B, S, wq.shape[1]), BF16),
                   jax.ShapeDtypeStruct((B, S, wg.shape[1]), F32),
                   jax.ShapeDtypeStruct((B, S, wc.shape[1]), F32),
                   jax.ShapeDtypeStruct((B, wt.shape[0], S), BF16)],
        compiler_params=_cp("parallel", "parallel"), name="nsa_proj",
    )(x, mod, mkv, g, gkv, wq, wg, wc, wt)


def _cmp1_body(k0_ref, k1_ref, v0_ref, v1_ref, pk_ref, pv_ref, wk_ref, wv_ref, fk_o, fv_o, *, n_chunk):
    hid = wk_ref.shape[1]
    gpr = LANES // HEAD_DIM
    for srcs, pos, w1, out in (((k0_ref, k1_ref), pk_ref, wk_ref, fk_o), ((v0_ref, v1_ref), pv_ref, wv_ref, fv_o)):
        acc = [[jnp.zeros((n_chunk, hid), F32) for _ in range(CMP_RATIO)] for _ in range(NSA_GROUPS)]
        for l in range(CMP_STRIDE):
            for si, src in enumerate(srcs):
                rows = src[pl.ds(l, n_chunk, stride=CMP_STRIDE), :]
                for gi in range(gpr):
                    g = si * gpr + gi
                    t = rows[:, gi * HEAD_DIM:(gi + 1) * HEAD_DIM]
                    for half in range(CMP_RATIO):
                        p = half * CMP_STRIDE + l
                        lhs = _bf(t + pos[p:p + 1, :])
                        acc[g][half] = acc[g][half] + _dot(lhs, w1[p * HEAD_DIM:(p + 1) * HEAD_DIM, :])
        for g in range(NSA_GROUPS):
            for half in range(CMP_RATIO):
                out[g, :, half * hid:(half + 1) * hid] = acc[g][half]


def _cmp1(kvc, pos_k, pos_v, w1k, w1v, ts):
    B, S, _ = kvc.shape
    G = NSA_GROUPS
    hid = w1k.shape[1]
    nch = ts // CMP_STRIDE
    full = lambda a: pl.BlockSpec(a.shape, lambda b, i: (0,) * a.ndim)
    src = lambda j: pl.BlockSpec((None, ts, LANES), lambda b, i: (b, i, j))
    outs = pl.BlockSpec((None, G, nch, CMP_RATIO * hid), lambda b, i: (b, 0, i, 0))
    return pl.pallas_call(
        functools.partial(_cmp1_body, n_chunk=nch), grid=(B, S // ts),
        in_specs=[src(0), src(1), src(2), src(3), full(pos_k), full(pos_v), full(w1k), full(w1v)],
        out_specs=[outs, outs],
        out_shape=[jax.ShapeDtypeStruct((B, G, S // CMP_STRIDE, CMP_RATIO * hid), F32)] * 2,
        compiler_params=_cp("parallel", "parallel"), name="cmp_stage1",
    )(kvc, kvc, kvc, kvc, pos_k, pos_v, w1k, w1v)


def _gelu_tanh(x):
    return 0.5 * x * (1.0 + jnp.tanh(math.sqrt(2.0 / math.pi) * (x + 0.044715 * (x * x * x))))


def _cmp2_body(fk_ref, fv_ref, w2kt_ref, w2v_ref, kt_o, v_o):
    nc = fk_ref.shape[0]
    hid = w2v_ref.shape[0]
    row = lax.broadcasted_iota(jnp.int32, (nc, hid), 0)

    def hidden(f):
        nxt = jnp.where(row == nc - 1, 0.0, pltpu.roll(f[:, hid:2 * hid], nc - 1, axis=0))
        return _bf(_gelu_tanh(f[:, 0:hid] + nxt))

    kt_o[...] = _bf(_dot_nt(w2kt_ref[...], hidden(fk_ref[...])))
    v_o[...] = _bf(_dot(hidden(fv_ref[...]), w2v_ref[...]))


def _cmp2(fk, fv, w2kt, w2v):
    B, G, NC, H2 = fk.shape
    fin = pl.BlockSpec((None, None, NC, H2), lambda b, g: (b, g, 0, 0))
    return pl.pallas_call(
        _cmp2_body, grid=(B, G),
        in_specs=[fin, fin, pl.BlockSpec(w2kt.shape, lambda b, g: (0, 0)),
                  pl.BlockSpec(w2v.shape, lambda b, g: (0, 0))],
        out_specs=[pl.BlockSpec((None, None, HEAD_DIM, NC), lambda b, g: (b, g, 0, 0)),
                   pl.BlockSpec((None, None, NC, HEAD_DIM), lambda b, g: (b, g, 0, 0))],
        out_shape=[jax.ShapeDtypeStruct((B, G, HEAD_DIM, NC), BF16),
                   jax.ShapeDtypeStruct((B, G, NC, HEAD_DIM), BF16)],
        compiler_params=_cp("parallel", "parallel"), name="cmp_stage2",
    )(fk, fv, w2kt, w2v)


def _head_queries(q_ref):
    return [q_ref[:, h * HEAD_DIM:(h + 1) * HEAD_DIM] for h in range(NSA_HPG)]


def _softmax_attend(q_ref, kt, bias, pv, o_ref, tq):
    rb = SOFTMAX_ROWS
    n_blk = tq // rb
    qs = _head_queries(q_ref)
    scores = {0: _dot(qs[0], kt)}
    tot = [None] * n_blk
    for h in range(NSA_HPG):
        if h + 1 < NSA_HPG:
            scores[h + 1] = _dot(qs[h + 1], kt)
        s_h = scores.pop(h)
        blocks = []
        for b in range(n_blk):
            s = s_h[b * rb:(b + 1) * rb, :] + bias[b * rb:(b + 1) * rb, :]
            m = jnp.maximum(jnp.max(s, axis=-1, keepdims=True), 0.1 * NEG_INF)
            e = jnp.exp2(s - m)
            p = e * (1.0 / jnp.maximum(jnp.sum(e, axis=-1, keepdims=True), 1e-30))
            blocks.append(_bf(p))
            tot[b] = p if tot[b] is None else tot[b] + p
        o_ref[:, h * HEAD_DIM:(h + 1) * HEAD_DIM] = pv(jnp.concatenate(blocks, axis=0)).astype(o_ref.dtype)
    return tot


def _select_blocks(imp, t0):
    nb, tq = imp.shape
    blk = lax.broadcasted_iota(jnp.int32, (nb, tq), 0)
    cur = (t0 + lax.broadcasted_iota(jnp.int32, (nb, tq), 1)) // SLC_LEN
    forced = (blk == 0) | (blk == cur) | (blk == cur - 1)
    imp = jnp.where(forced, FORCED_SCORE, imp)
    imp = jnp.where(blk > cur, -1.0, imp)
    bits = lax.bitcast_convert_type(imp, jnp.int32)
    k_sel = float(min(TOP_N, nb))
    count = lambda mask: jnp.sum(jnp.where(mask, 1.0, 0.0), axis=0, keepdims=True)

    def refine(it, thr):
        cand = thr | jnp.left_shift(jnp.int32(1), 30 - it)
        return jnp.where(count(bits >= cand) >= k_sel, cand, thr)

    thr = lax.fori_loop(0, 31, refine, jnp.zeros((1, tq), jnp.int32))
    above = bits > thr
    tied = bits == thr
    tri = (lax.broadcasted_iota(jnp.int32, (nb, nb), 0) >= lax.broadcasted_iota(jnp.int32, (nb, nb), 1))
    rank = _dot(tri.astype(BF16), jnp.where(tied, 1.0, 0.0).astype(BF16))
    return above | (tied & (rank <= k_sel - count(above)))


def _cmp_attn_body(q_ref, kt_ref, v_ref, wimp_ref, o_ref, sel_ref, imp_ref, *, tq):
    i = pl.program_id(2)
    nc = kt_ref.shape[1]
    nb = wimp_ref.shape[1]
    step = min(CMP_COLS, nc)
    n_need = (i * tq + tq - CMP_LEN) // CMP_STRIDE + 1
    for ncv in range(step, nc + 1, step):
        @pl.when((n_need > ncv - step) & ((n_need <= ncv) | (ncv == nc)))
        def _():
            t = i * tq + lax.broadcasted_iota(jnp.int32, (tq, ncv), 0)
            cmp_end = lax.broadcasted_iota(jnp.int32, (tq, ncv), 1) * CMP_STRIDE + (CMP_LEN - 1)
            bias = jnp.where(cmp_end <= t, 0.0, NEG_INF)
            v = v_ref[0:ncv, :]
            tot = _softmax_attend(q_ref, kt_ref[:, 0:ncv], bias, lambda p: _dot(p, v), o_ref, tq)
            wimp = wimp_ref[0:ncv, :]
            imp_ref[...] = sum(_dot(part, wimp) for part in _split_bf16(jnp.concatenate(tot, axis=0), 3))

    bstep = min(SEL_BLOCKS, nb)
    b_need = (i * tq + tq - 1) // SLC_LEN + 1
    for nbv in range(bstep, nb + 1, bstep):
        @pl.when((b_need > nbv - bstep) & ((b_need <= nbv) | (nbv == nb)))
        def _():
            sel = _select_blocks(imp_ref[:, 0:nbv].T, i * tq)
            sel_ref[:, 0:nbv] = _bf(jnp.where(sel, 0.0, NEG_INF).T)
            if nbv < nb:
                sel_ref[:, nbv:nb] = jnp.full((tq, nb - nbv), NEG_INF, BF16)


def _cmp_attn(q, kct, vc, wimp, tq):
    B, S, _ = q.shape
    G = NSA_GROUPS
    NC = kct.shape[3]
    NB = wimp.shape[1]
    GD = NSA_HPG * HEAD_DIM
    return pl.pallas_call(
        functools.partial(_cmp_attn_body, tq=tq), grid=(B, G, S // tq),
        in_specs=[pl.BlockSpec((None, tq, GD), lambda b, g, i: (b, i, g)),
                  pl.BlockSpec((None, None, HEAD_DIM, NC), lambda b, g, i: (b, g, 0, 0)),
                  pl.BlockSpec((None, None, NC, HEAD_DIM), lambda b, g, i: (b, g, 0, 0)),
                  pl.BlockSpec(wimp.shape, lambda b, g, i: (0, 0))],
        out_specs=[pl.BlockSpec((None, tq, GD), lambda b, g, i: (b, i, g)),
                   pl.BlockSpec((None, None, tq, NB), lambda b, g, i: (b, g, i, 0))],
        out_shape=[jax.ShapeDtypeStruct((B, S, G * GD), BF16),
                   jax.ShapeDtypeStruct((B, G, S, NB), BF16)],
        scratch_shapes=[pltpu.VMEM((tq, NB), F32)],
        compiler_params=_cp("parallel", "parallel", "parallel"), name="cmp_attention_topk",
    )(q, kct, vc, wimp)


def _slc_attn_body(q_ref, selb_ref, kt_ref, vt_ref, e_ref, o_ref, lhs_ref, s0_ref, s1_ref, p0_ref, p1_ref,
                   alpha_ref, m_ref, acc_ref, *, tq, tk, wb):
    i = pl.program_id(2)
    rb = SOFTMAX_ROWS
    rows = NSA_HPG * tq
    tiles_per_window = wb * SLC_LEN // tk
    for h in range(NSA_HPG):
        lhs_ref[pl.ds(h * tq, tq), 0:HEAD_DIM] = q_ref[:, h * HEAD_DIM:(h + 1) * HEAD_DIM]
    lhs_ref[:, HEAD_DIM:LANES] = jnp.zeros((rows, LANES - HEAD_DIM), BF16)
    m_ref[...] = jnp.full(m_ref.shape, NEG_INF, F32)
    acc_ref[...] = jnp.zeros_like(acc_ref)
    s_refs = (s0_ref, s1_ref)
    p_refs = (p0_ref, p1_ref)
    p1_ref[...] = jnp.zeros((rows, tk), BF16)
    ones = jnp.ones((LANES - HEAD_DIM, tk), BF16)
    zeros = jnp.zeros((LANES - HEAD_DIM, tk), BF16)
    tile_keys = lambda j: pl.ds(pl.multiple_of(j * tk, tk), tk)

    def set_window(w):
        win = selb_ref[:, pl.ds(pl.multiple_of(w * wb, wb), wb)]
        for h in range(NSA_HPG):
            lhs_ref[pl.ds(h * tq, tq), LANES:LANES + wb] = win

    def scores(j, s_ref):
        @pl.when(j % tiles_per_window == 0)
        def _():
            set_window(j // tiles_per_window)
        ks = tile_keys(j)
        es = pl.ds(pl.multiple_of((j % tiles_per_window) * tk, tk), tk)
        rhs = jnp.concatenate([kt_ref[:, ks], zeros, e_ref[:, es]], axis=0)
        s_ref[...] = _dot(lhs_ref[...], rhs)

    def value_product(j, p_ref):
        vt = jnp.concatenate([vt_ref[:, tile_keys(j)], ones], axis=0)
        return _dot_nt(p_ref[...], vt)

    def tile(j, par, last_tile):
        if last_tile:
            t = i * tq + lax.broadcasted_iota(jnp.int32, (tq, tk), 0)
            causal = j * tk + lax.broadcasted_iota(jnp.int32, (tq, tk), 1) <= t
        else:
            scores(j + 1, s_refs[1 - par])
        pv_prev = value_product(jnp.maximum(j - 1, 0), p_refs[1 - par])
        for h in range(NSA_HPG):
            for b in range(tq // rb):
                rs = pl.ds(h * tq + b * rb, rb)
                s = s_refs[par][rs, :]
                if last_tile:
                    s = jnp.where(causal[b * rb:(b + 1) * rb, :], s, NEG_INF)
                m_old = m_ref[rs, :]
                m_new = jnp.maximum(m_old, jnp.max(s, axis=-1, keepdims=True))
                ps = [jnp.exp2(_bf(s[:, c * LANES:(c + 1) * LANES] - m_new)) for c in range(tk // LANES)]
                m_ref[rs, :] = m_new
                alpha_ref[rs, :] = jnp.exp2(m_old - m_new)
                p_refs[par][rs, :] = jnp.concatenate(ps, axis=1)
        acc_ref[...] = (acc_ref[...] + pv_prev) * alpha_ref[...]

    last = (i * tq + tq - 1) // tk
    scores(0, s_refs[0])

    def two_tiles(jj, carry):
        tile(2 * jj, 0, False)
        tile(2 * jj + 1, 1, False)
        return carry

    lax.fori_loop(0, last // 2, two_tiles, 0)

    @pl.when(last % 2 == 1)
    def _():
        tile(last - 1, 0, False)

    for par in range(2):
        @pl.when(last % 2 == par)
        def _():
            tile(last, par, True)
            acc = acc_ref[...] + value_product(last, p_refs[par])
            for h in range(NSA_HPG):
                a_h = acc[h * tq:(h + 1) * tq, :]
                o_ref[:, h * HEAD_DIM:(h + 1) * HEAD_DIM] = _bf(
                    a_h[:, :HEAD_DIM] / jnp.maximum(a_h[:, HEAD_DIM:], 1e-30))


def _slc_attn(q, selb, kvt, e, tq, tk, wb):
    B, S, _ = q.shape
    G = NSA_GROUPS
    NB = selb.shape[3]
    GD = NSA_HPG * HEAD_DIM
    rows = NSA_HPG * tq
    return pl.pallas_call(
        functools.partial(_slc_attn_body, tq=tq, tk=tk, wb=wb), grid=(B, G, S // tq),
        in_specs=[pl.BlockSpec((None, tq, GD), lambda b, g, i: (b, i, g)),
                  pl.BlockSpec((None, None, tq, NB), lambda b, g, i: (b, g, i, 0)),
                  pl.BlockSpec((None, HEAD_DIM, S), lambda b, g, i: (b, g, 0)),
                  pl.BlockSpec((None, HEAD_DIM, S), lambda b, g, i: (b, G + g, 0)),
                  pl.BlockSpec(e.shape, lambda b, g, i: (0, 0))],
        out_specs=pl.BlockSpec((None, tq, GD), lambda b, g, i: (b, i, g)),
        out_shape=jax.ShapeDtypeStruct((B, S, G * GD), BF16),
        scratch_shapes=[pltpu.VMEM((rows, LANES + wb), BF16), pltpu.VMEM((rows, tk), F32), pltpu.VMEM((rows, tk), F32),
                        pltpu.VMEM((rows, tk), BF16), pltpu.VMEM((rows, tk), BF16),
                        pltpu.VMEM((rows, LANES), F32), pltpu.VMEM((rows, LANES), F32),
                        pltpu.VMEM((rows, LANES), F32)],
        compiler_params=_cp("parallel", "parallel", "parallel"), name="slc_attention",
    )(q, selb, kvt, kvt, e)


def _win_attn_body(q_ref, kt_ref, vt_ref, o_ref, *, tq, span, n_sub):
    rb = SOFTMAX_ROWS
    ones = jnp.ones((LANES - HEAD_DIM, span), BF16)
    for u in range(n_sub):
        t0 = (pl.program_id(2) * n_sub + u) * tq
        start = pl.multiple_of(jnp.maximum(t0 - WINDOW, 0), LANES)
        ks = pl.ds(start, span)
        t = t0 + lax.broadcasted_iota(jnp.int32, (tq, span), 0)
        dist = t - (start + lax.broadcasted_iota(jnp.int32, (tq, span), 1))
        bias = jnp.where((dist >= 0) & (dist < WINDOW), 0.0, NEG_INF)
        kt = kt_ref[:, ks]
        vt = jnp.concatenate([vt_ref[:, ks], ones], axis=0)
        qs = _head_queries(q_ref.at[pl.ds(u * tq, tq), :])
        scores = {0: _dot(qs[0], kt)}
        for h in range(NSA_HPG):
            if h + 1 < NSA_HPG:
                scores[h + 1] = _dot(qs[h + 1], kt)
            s_h = scores.pop(h)
            blocks = []
            for b in range(tq // rb):
                s = s_h[b * rb:(b + 1) * rb, :] + bias[b * rb:(b + 1) * rb, :]
                blocks.append(jnp.exp2(_bf(s - jnp.max(s, axis=-1, keepdims=True))))
            acc = _dot_nt(jnp.concatenate(blocks, axis=0), vt)
            o_ref[pl.ds(u * tq, tq), h * HEAD_DIM:(h + 1) * HEAD_DIM] = _bf(acc[:, :HEAD_DIM] / acc[:, HEAD_DIM:])


def _win_attn(q, kvt, tq, n_sub):
    B, S, _ = q.shape
    G = NSA_GROUPS
    GD = NSA_HPG * HEAD_DIM
    span = min(tq + WINDOW, S)
    blk = tq * n_sub
    return pl.pallas_call(
        functools.partial(_win_attn_body, tq=tq, span=span, n_sub=n_sub), grid=(B, G, S // blk),
        in_specs=[pl.BlockSpec((None, blk, GD), lambda b, g, i: (b, i, g)),
                  pl.BlockSpec((None, HEAD_DIM, S), lambda b, g, i: (b, 2 * G + g, 0)),
                  pl.BlockSpec((None, HEAD_DIM, S), lambda b, g, i: (b, 3 * G + g, 0))],
        out_specs=pl.BlockSpec((None, blk, GD), lambda b, g, i: (b, i, g)),
        out_shape=jax.ShapeDtypeStruct((B, S, G * GD), BF16),
        compiler_params=_cp("parallel", "parallel", "parallel"), name="win_attention",
    )(q, kvt, kvt)


def _nsa_out_body(oc_ref, os_ref, ow_ref, gt_ref, ge_ref, w_ref, x_ref, mod_ref, g_ref, o_ref):
    D = oc_ref.shape[1]
    ge = ge_ref[...]
    gx = sum(_dot(part, ge) for part in _split_bf16(gt_ref[...], 2))
    o = gx[:, 0:D] * oc_ref[...] + gx[:, D:2 * D] * os_ref[...] + gx[:, 2 * D:3 * D] * ow_ref[...]
    y = _dot(_bf(o), w_ref[...])
    o_ref[...] = x_ref[...] + mod_ref[2:3, :] * _rms(y, g_ref[...])


def _nsa_out(oc, osl, ow, gates, gexp, w, x, mod, g, ts):
    B, S, D = x.shape
    tok = lambda n: pl.BlockSpec((None, ts, n), lambda b, i: (b, i, 0))
    full = lambda a: pl.BlockSpec(a.shape, lambda b, i: (0,) * a.ndim)
    return pl.pallas_call(
        _nsa_out_body, grid=(B, S // ts),
        in_specs=[tok(D), tok(D), tok(D), tok(gates.shape[2]), full(gexp), full(w), tok(D),
                  pl.BlockSpec((None, 6, D), lambda b, i: (b, 0, 0)), full(g)],
        out_specs=tok(D), out_shape=jax.ShapeDtypeStruct((B, S, D), F32),
        compiler_params=_cp("parallel", "parallel"), name="nsa_out",
    )(oc, osl, ow, gates, gexp, w, x, mod, g)


def _importance_matrix(nc, nb):
    n_cmp = nc - CMP_RATIO + 1
    w = np.zeros((nc, nb), np.float32)
    for m in range(SLC_RATIO):
        for n in range(CMP_RATIO):
            off = m - n + CMP_RATIO - 1
            for j in range(nb):
                src = SLC_RATIO * j + off - (CMP_RATIO - 1)
                if 0 <= src < n_cmp:
                    w[src, j] += 1.0
    return jnp.asarray(w, BF16)


def _block_expand(wb):
    return jnp.asarray(np.arange(wb)[:, None] == (np.arange(wb * SLC_LEN)[None, :] // SLC_LEN), BF16)


def _gate_expand(n_in):
    hd = NSA_HEADS * HEAD_DIM
    w = np.zeros((n_in, 3 * hd), np.float32)
    for head in range(NSA_HEADS):
        for br in range(3):
            w[head * 3 + br, br * hd + head * HEAD_DIM: br * hd + (head + 1) * HEAD_DIM] = 1.0
    return jnp.asarray(w, BF16)


def kernel(x, c, ada_w, ada_b, norm_g, mlp_up, mlp_down, rw_mu, rw_r, rw_k, rw_v, rw_o, rw_w0, rw_wa, rw_wb,
           rw_a0, rw_aa, rw_ab, rw_ga, rw_gb, rw_kk, rw_ka, rw_rk, rw_lng, rw_lnb, kv_norm_g, kv_ada_w,
           kv_ada_b, kv_w, cmp_pos_k, cmp_pos_v, cmp_k_w1, cmp_k_w2, cmp_v_w1, cmp_v_w2, nsa_wqg, nsa_wo):
    B, S, D = x.shape
    depth = ada_w.shape[0]
    n_a = rw_mu.shape[0]
    bf = _bf
    row = lambda a: a.reshape(1, -1)
    ts = min(512, S)

    c8 = jnp.pad(c, ((0, 8 - B), (0, 0)))
    mods = _ada(c8, ada_w, ada_b)[:, :B].reshape(depth, B, 6, D)
    mkv = _ada(c8, kv_ada_w[None], kv_ada_b[None])[0, :B].reshape(B, 2, D)

    GD = NSA_GROUPS * HEAD_DIM
    HD = NSA_HEADS * HEAD_DIM
    shared = None
    for layer in range(depth):
        mod = mods[layer]
        ng = norm_g[layer]
        if layer < n_a:
            i = layer
            r, k, v, lw, kk, a, g = _rw_front(
                x, mod, row(ng[0]), rw_mu[i], bf(rw_r[i]), bf(rw_k[i]), bf(rw_v[i]), bf(rw_wa[i]),
                bf(rw_wb[i]), row(rw_w0[i]), bf(rw_aa[i]), bf(rw_ab[i]), row(rw_a0[i]), bf(rw_ga[i]),
                bf(rw_gb[i]), row(rw_kk[i]), row(rw_ka[i]), min(256, S))
            y = _rw_recurrence(r, k, v, lw, kk, a, g, row(rw_rk[i]), row(rw_lng[i]), row(rw_lnb[i]),
                               min(512, S), 2)
            x = _proj_res(y, bf(rw_o[i]), x, mod, row(ng[1]), ts)
        else:
            j = layer - n_a
            wqg = nsa_wqg[j]
            wq = bf(wqg[:, :HD])
            wg = bf(jnp.pad(wqg[:, HD:], ((0, 0), (0, LANES - 3 * NSA_HEADS))))
            if shared is None:
                wc = bf(kv_w[:, :2 * GD])
                wt = bf(kv_w[:, 2 * GD:].T)
            q, gates, kvc, kvt = _nsa_proj(x, mod, mkv, row(ng[0]), row(kv_norm_g), wq, wg, wc, wt, ts)
            if shared is None:
                fk, fv = _cmp1(kvc, cmp_pos_k, cmp_pos_v, bf(cmp_k_w1), bf(cmp_v_w1), min(1024, S))
                kct, vc = _cmp2(fk, fv, bf(cmp_k_w2.T), bf(cmp_v_w2))
                shared = (kct, vc, kvt)
            kct, vc, kvt_s = shared
            nc, nb = S // CMP_STRIDE, S // SLC_LEN
            tq = min(512, S)
            oc, sel = _cmp_attn(q, kct, vc, _importance_matrix(nc, nb), tq)
            wb = min(LANES, nb)
            osl = _slc_attn(q, sel, kvt_s, _block_expand(wb), min(512, S), min(512, S), wb)
            ow = _win_attn(q, kvt_s, min(256, S), 2)
            x = _nsa_out(oc, osl, ow, gates, _gate_expand(gates.shape[2]), bf(nsa_wo[j]), x, mod,
                         row(ng[1]), ts)
        x = _mlp(x, mod, row(ng[2]), row(ng[3]), bf(mlp_up[layer]), bf(mlp_down[layer]), min(1024, S),
                 1024)
    return x
```

```python
import functools
import math

import numpy as np
import jax
import jax.numpy as jnp
from jax import lax
from jax.experimental import pallas as pl
from jax.experimental.pallas import tpu as pltpu

F32 = jnp.float32
BF16 = jnp.bfloat16
HI = lax.Precision.HIGHEST

NORM_EPS = 1e-6
GN_EPS = 64e-5
RWKV_HEAD = 64
HEAD_DIM = 64
NSA_GROUPS = 4
NSA_HPG = 4
NSA_HEADS = 16
CMP_LEN = 32
CMP_STRIDE = 16
CMP_RATIO = CMP_LEN // CMP_STRIDE
SLC_LEN = 64
SLC_RATIO = SLC_LEN // CMP_STRIDE
TOP_N = 16
WINDOW = 512
FORCED_SCORE = 1e4
NEG_INF = -1e30
LOG2E = math.log2(math.e)

LANES = 128
VMEM_LIMIT = 56 * 1024 * 1024
REC_CHUNK = 64
SOFTMAX_ROWS = 32
CMP_COLS = 256
SEL_BLOCKS = 64


def _cp(*sem):
    return pltpu.CompilerParams(dimension_semantics=sem, vmem_limit_bytes=VMEM_LIMIT)


def _dot(a, b, **kw):
    return jnp.dot(a, b, preferred_element_type=F32, **kw)


def _dot_nt(a, b, **kw):
    return lax.dot_general(a, b, (((1,), (1,)), ((), ())), preferred_element_type=F32, **kw)


def _dot_tn(a, b, **kw):
    return lax.dot_general(a, b, (((0,), (0,)), ((), ())), preferred_element_type=F32, **kw)


def _bf(t):
    return t.astype(BF16)


def _split_bf16(x, parts):
    out = []
    for _ in range(parts - 1):
        hi = _bf(x)
        out.append(hi)
        x = x - hi.astype(F32)
    out.append(_bf(x))
    return out


def _rms(x, g):
    return x * lax.rsqrt(jnp.mean(x * x, axis=-1, keepdims=True) + NORM_EPS) * g


def _ada_body(c_ref, w_ref, b_ref, o_ref):
    c = c_ref[...]
    o_ref[...] = _dot(c * jax.nn.sigmoid(c), w_ref[...], precision=HI) + b_ref[...]


def _ada(c8, w, b):
    L, D, N = w.shape
    tn = 1024
    return pl.pallas_call(
        _ada_body, grid=(L, N // tn),
        in_specs=[pl.BlockSpec((8, D), lambda l, j: (0, 0)),
                  pl.BlockSpec((None, D, tn), lambda l, j: (l, 0, j)),
                  pl.BlockSpec((None, 1, tn), lambda l, j: (l, 0, j))],
        out_specs=pl.BlockSpec((None, 8, tn), lambda l, j: (l, 0, j)),
        out_shape=jax.ShapeDtypeStruct((L, 8, N), F32),
        compiler_params=_cp("parallel", "parallel"), name="ada",
    )(c8, w, b.reshape(L, 1, N))


def _rw_front_body(x_ref, xp_ref, mod_ref, g_ref, mu_ref, wr, wk, wv, wa, wb, w0, aa, ab, a0, ga, gb,
                   kk_ref, ka_ref, r_o, k_o, v_o, lw_o, kk_o, a_o, g_o):
    i = pl.program_id(1)
    g = g_ref[...]
    sh, sc = mod_ref[0:1, :], mod_ref[1:2, :]
    h = _rms(x_ref[...], g) * (1 + sc) + sh
    hp = _rms(xp_ref[...], g) * (1 + sc) + sh
    hp = jnp.where(i > 0, hp[7:8, :], 0.0)
    row = lax.broadcasted_iota(jnp.int32, h.shape, 0)
    hs = jnp.where(row == 0, hp, pltpu.roll(h, 1, axis=0))
    xx = hs - h
    mu = mu_ref[...]
    xr, xw, xk, xv, xa, xg = [_bf(h + xx * mu[j:j + 1, :]) for j in range(6)]
    r = _dot(xr, wr[...])
    k = _dot(xk, wk[...])
    v = _dot(xv, wv[...])
    wl = w0[...] + _dot(_bf(jnp.tanh(_dot(xw, wa[...]))), wb[...])
    lw = -jax.nn.sigmoid(wl) * math.exp(-0.5)
    a = jax.nn.sigmoid(a0[...] + _dot(_bf(_dot(xa, aa[...])), ab[...]))
    gate = _dot(_bf(jax.nn.sigmoid(_dot(xg, ga[...]))), gb[...])
    r_o[...] = _bf(r)
    k_o[...] = _bf(k * (1 + (a - 1) * ka_ref[...]))
    v_o[...] = _bf(v)
    lw_o[...] = lw
    kk_o[...] = _bf(k * kk_ref[...])
    a_o[...] = _bf(a)
    g_o[...] = _bf(gate)


def _rw_front(x, mod, g, mu, wr, wk, wv, wa, wb, w0, aa, ab, a0, ga, gb, kk, ka, ts):
    B, S, D = x.shape
    full = lambda a: pl.BlockSpec(a.shape, lambda b, i: (0,) * a.ndim)
    tok = pl.BlockSpec((None, ts, D), lambda b, i: (b, i, 0))
    ws = [wr, wk, wv, wa, wb, w0, aa, ab, a0, ga, gb, kk, ka]
    return pl.pallas_call(
        _rw_front_body, grid=(B, S // ts),
        in_specs=[tok,
                  pl.BlockSpec((None, 8, D), lambda b, i: (b, jnp.maximum(i * (ts // 8) - 1, 0), 0)),
                  pl.BlockSpec((None, 6, D), lambda b, i: (b, 0, 0)),
                  full(g), full(mu)] + [full(w) for w in ws],
        out_specs=[tok] * 7,
        out_shape=[jax.ShapeDtypeStruct((B, S, D), F32 if j == 3 else BF16) for j in range(7)],
        compiler_params=_cp("parallel", "parallel"), name="rwkv_front",
    )(x, x, mod, g, mu, *ws)


def _rec_body(r_ref, k_ref, v_ref, lw_ref, kk_ref, a_ref, g_ref, rk_ref, lng_ref, lnb_ref, o_ref, st_ref,
              *, n_chunks, n_pairs):
    C = REC_CHUNK
    L = LANES
    C2 = 2 * C

    @pl.when(pl.program_id(2) == 0)
    def _():
        st_ref[...] = jnp.zeros_like(st_ref)

    h0 = lax.broadcasted_iota(jnp.int32, (C, L), 1) < RWKV_HEAD
    rr = lax.broadcasted_iota(jnp.int32, (C2, C2), 0)
    cc = lax.broadcasted_iota(jnp.int32, (C2, C2), 1)
    same = rr // C == cc // C
    incl = same & (rr >= cc)
    strict = same & (rr > cc)
    eye = (rr == cc).astype(F32)
    lvl_masks = [(rr // sz == cc // sz) & (rr // (sz // 2) != cc // (sz // 2))
                 for sz in (2 ** e for e in range(1, int(math.log2(C)) + 1))]
    tri_b = (lax.broadcasted_iota(jnp.int32, (C, C), 0) >= lax.broadcasted_iota(jnp.int32, (C, C), 1)).astype(BF16)
    bd_b = (lax.broadcasted_iota(jnp.int32, (L, L), 0) // RWKV_HEAD
            == lax.broadcasted_iota(jnp.int32, (L, L), 1) // RWKV_HEAD).astype(BF16)
    units = [(pr, c) for pr in range(n_pairs) for c in range(n_chunks)]

    def head_sums(ts):
        tot = sum(_dot(p, bd_b) for p in _split_bf16(jnp.concatenate(ts, axis=0), 2))
        return [tot[n * C:(n + 1) * C] for n in range(len(ts))]

    def stack(t):
        zero = jnp.zeros_like(t)
        return jnp.concatenate([jnp.where(h0, t, zero), jnp.where(h0, zero, t)], axis=0)

    sl = lambda ref, pr, c: ref[pl.ds(c * C, C), pr * L:(pr + 1) * L].astype(F32)
    ld = [dict(r=sl(r_ref, pr, c), k=sl(k_ref, pr, c), v=sl(v_ref, pr, c), lw=sl(lw_ref, pr, c),
               kk=sl(kk_ref, pr, c), a=sl(a_ref, pr, c)) for pr, c in units]
    kkn = [d["kk"] / jnp.maximum(jnp.sqrt(ss), 1e-12) for d, ss in zip(ld, head_sums([d["kk"] * d["kk"] for d in ld]))]
    cl = [sum(_dot(tri_b, p) for p in _split_bf16(d["lw"], 3)) for d in ld]
    cl_end = [x[C - 1:C, :] for x in cl]
    e_neg = [jnp.exp(-x) for x in cl]
    e_end = [jnp.exp(ce - x) for x, ce in zip(cl, cl_end)]
    rt2 = [_bf(stack(d["r"] * jnp.exp(x))) for d, x in zip(ld, cl)]
    at2 = [_bf(stack(-kn * jnp.exp(x - d["lw"]))) for d, kn, x in zip(ld, kkn, cl)]
    bt2 = [_bf(stack(kn * d["a"] * e)) for d, kn, e in zip(ld, kkn, e_neg)]
    kt2 = [_bf(stack(d["k"] * e)) for d, e in zip(ld, e_neg)]
    bend2 = [_bf(stack(kn * d["a"] * e)) for d, kn, e in zip(ld, kkn, e_end)]
    kend2 = [_bf(stack(d["k"] * e)) for d, e in zip(ld, e_end)]
    v2 = [_bf(stack(d["v"])) for d in ld]
    decay = [jnp.exp(x) for x in cl_end]

    gm = [_dot_nt(jnp.concatenate([a_, r_], axis=0), jnp.concatenate([b_, k_], axis=0))
          for a_, r_, b_, k_ in zip(at2, rt2, bt2, kt2)]
    a_ab = [jnp.where(strict, g_[:C2, :C2], 0.0) for g_ in gm]
    a_ak = [_bf(jnp.where(strict, g_[:C2, C2:], 0.0)) for g_ in gm]
    a_rb = [_bf(jnp.where(incl, g_[C2:, :C2], 0.0)) for g_ in gm]
    a_rk = [_bf(jnp.where(incl, g_[C2:, C2:], 0.0)) for g_ in gm]

    tm = [eye + jnp.where(lvl_masks[0], a_, 0.0) for a_ in a_ab]
    for lm in lvl_masks[1:]:
        off = [_bf(jnp.where(lm, a_, 0.0)) for a_ in a_ab]
        half = [_bf(_dot(_bf(t_), o_)) for t_, o_ in zip(tm, off)]
        tm = [t_ + _dot(h_, _bf(t_)) for t_, h_ in zip(tm, half)]
    tm = [_bf(t_) for t_ in tm]

    w2 = [_bf(_dot(t_, a_)) for t_, a_ in zip(tm, at2)]
    akv = [_bf(_dot(a_, v_)) for a_, v_ in zip(a_ak, v2)]
    uh2 = [_bf(_dot(t_, x_)) for t_, x_ in zip(tm, akv)]
    rw2 = [_bf(r_.astype(F32) + _dot(a_, w_)) for r_, a_, w_ in zip(rt2, a_rb, w2)]
    yh2 = [_dot(a_, u_) + _dot(b_, v_) for a_, u_, b_, v_ in zip(a_rb, uh2, a_rk, v2)]
    ec = [_bf(_dot_tn(w_, b_)) for w_, b_ in zip(w2, bend2)]
    qc = [_dot_tn(jnp.concatenate([u_, v_], axis=0), jnp.concatenate([b_, k_], axis=0))
          for u_, v_, b_, k_ in zip(uh2, v2, bend2, kend2)]

    states = [st_ref[pr] for pr in range(n_pairs)]
    y2 = [None] * len(units)
    for c in range(n_chunks):
        for pr in range(n_pairs):
            u = pr * n_chunks + c
            n = states[pr]
            n_b = _bf(n)
            y2[u] = _dot_nt(rw2[u], n_b) + yh2[u]
            states[pr] = n * decay[u] + _dot(n_b, ec[u]) + qc[u]
    for pr in range(n_pairs):
        st_ref[pr] = states[pr]

    lanes = [slice(pr * L, (pr + 1) * L) for pr, _ in units]
    ys = [y_[:C] + y_[C:] for y_ in y2]
    ycs = [y - m * (1.0 / RWKV_HEAD) for y, m in zip(ys, head_sums(ys))]
    vars_ = head_sums([yc * yc for yc in ycs])
    rks = head_sums([d["r"] * d["k"] * rk_ref[:, ln] for d, ln in zip(ld, lanes)])
    for u, (pr, c) in enumerate(units):
        ln = lanes[u]
        yn = ycs[u] * lax.rsqrt(vars_[u] * (1.0 / RWKV_HEAD) + GN_EPS) * lng_ref[:, ln] + lnb_ref[:, ln]
        o_ref[pl.ds(c * C, C), ln] = _bf((yn + rks[u] * ld[u]["v"]) * sl(g_ref, pr, c))


def _rw_recurrence(r, k, v, lw, kk, a, g, rk, lng, lnb, tt, n_pairs):
    B, S, D = r.shape
    width = n_pairs * LANES
    tok = pl.BlockSpec((None, tt, width), lambda b, hp, t: (b, t, hp))
    vec = pl.BlockSpec((1, width), lambda b, hp, t: (0, hp))
    return pl.pallas_call(
        functools.partial(_rec_body, n_chunks=tt // REC_CHUNK, n_pairs=n_pairs),
        grid=(B, D // width, S // tt),
        in_specs=[tok] * 7 + [vec] * 3,
        out_specs=tok,
        out_shape=jax.ShapeDtypeStruct((B, S, D), BF16),
        scratch_shapes=[pltpu.VMEM((n_pairs, LANES, LANES), F32)],
        compiler_params=_cp("parallel", "parallel", "arbitrary"), name="rwkv_recurrence",
    )(r, k, v, lw, kk, a, g, rk, lng, lnb)


def _mlp_body(*refs, fused_proj):
    if fused_proj:
        y_ref, wo_ref, g1_ref, x_ref, mod_ref, g2_ref, g3_ref, up_ref, dn_ref, o_ref, x1_ref, h_ref, acc_ref = refs
    else:
        x_ref, mod_ref, g2_ref, g3_ref, up_ref, dn_ref, o_ref, x1_ref, h_ref, acc_ref = refs
    f = pl.program_id(2)

    @pl.when(f == 0)
    def _():
        x1 = x_ref[...]
        if fused_proj:
            x1 = x1 + mod_ref[2:3, :] * _rms(_dot(y_ref[...], wo_ref[...]), g1_ref[...])
        x1_ref[...] = x1
        h = _rms(x1, g2_ref[...]) * (1 + mod_ref[4:5, :]) + mod_ref[3:4, :]
        h_ref[...] = _bf(h)
        acc_ref[...] = jnp.zeros_like(acc_ref)

    z = jnp.maximum(_dot(h_ref[...], up_ref[...]), 0.0)
    acc_ref[...] += _dot(_bf(z * z), dn_ref[...])

    @pl.when(f == pl.num_programs(2) - 1)
    def _():
        o_ref[...] = x1_ref[...] + mod_ref[5:6, :] * _rms(acc_ref[...], g3_ref[...])


def _mlp(x, mod, g2, g3, up, dn, ts, tf, proj=None):
    B, S, D = x.shape
    F = up.shape[1]
    tok = pl.BlockSpec((None, ts, D), lambda b, i, f: (b, i, 0))
    vec = pl.BlockSpec((1, D), lambda b, i, f: (0, 0))
    pre_specs, pre_args = [], []
    if proj is not None:
        y, wo, g1 = proj
        pre_specs = [tok, pl.BlockSpec(wo.shape, lambda b, i, f: (0, 0)), vec]
        pre_args = [y, wo, g1]
    return pl.pallas_call(
        functools.partial(_mlp_body, fused_proj=proj is not None), grid=(B, S // ts, F // tf),
        in_specs=pre_specs + [tok, pl.BlockSpec((None, 6, D), lambda b, i, f: (b, 0, 0)), vec, vec,
                              pl.BlockSpec((D, tf), lambda b, i, f: (0, f)),
                              pl.BlockSpec((tf, D), lambda b, i, f: (f, 0))],
        out_specs=tok, out_shape=jax.ShapeDtypeStruct((B, S, D), F32),
        scratch_shapes=[pltpu.VMEM((ts, D), F32), pltpu.VMEM((ts, D), BF16), pltpu.VMEM((ts, D), F32)],
        compiler_params=_cp("parallel", "parallel", "arbitrary"), name="mlp",
    )(*pre_args, x, mod, g2, g3, up, dn)


def _nsa_proj_body(x_ref, mod_ref, mkv_ref, g_ref, gkv_ref, wq_ref, wg_ref, wc_ref, wt_ref,
                   q_o, gt_o, c_o, t_o):
    x = x_ref[...]
    xn = x * lax.rsqrt(jnp.mean(x * x, axis=-1, keepdims=True) + NORM_EPS)
    h = _bf((xn * g_ref[...]) * (1 + mod_ref[1:2, :]) + mod_ref[0:1, :])
    hkv = _bf((xn * gkv_ref[...]) * (1 + mkv_ref[1:2, :]) + mkv_ref[0:1, :])
    q_o[...] = _bf(_dot(h, wq_ref[...]) * (HEAD_DIM ** -0.5 * LOG2E))
    gt_o[...] = jax.nn.sigmoid(_dot(h, wg_ref[...]))
    c_o[...] = _dot(hkv, wc_ref[...])
    t_o[...] = _bf(_dot_nt(wt_ref[...], hkv))


def _nsa_proj(x, mod, mkv, g, gkv, wq, wg, wc, wt, ts):
    B, S, D = x.shape
    tok = lambda n: pl.BlockSpec((None, ts, n), lambda b, i: (b, i, 0))
    full = lambda a: pl.BlockSpec(a.shape, lambda b, i: (0,) * a.ndim)
    return pl.pallas_call(
        _nsa_proj_body, grid=(B, S // ts),
        in_specs=[tok(D), pl.BlockSpec((None, 6, D), lambda b, i: (b, 0, 0)),
                  pl.BlockSpec((None, 2, D), lambda b, i: (b, 0, 0)),
                  full(g), full(gkv), full(wq), full(wg), full(wc), full(wt)],
        out_specs=[tok(wq.shape[1]), tok(wg.shape[1]), tok(wc.shape[1]),
                   pl.BlockSpec((None, wt.shape[0], ts), lambda b, i: (b, 0, i))],
        out_shape=[jax.ShapeDtypeStruct((B, S, wq.shape[1]), BF16),
                   jax.ShapeDtypeStruct((B, S, wg.shape[1]), F32),
                   jax.ShapeDtypeStruct((B, S, wc.shape[1]), F32),
                   jax.ShapeDtypeStruct((B, wt.shape[0], S), BF16)],
        compiler_params=_cp("parallel", "parallel"), name="nsa_proj",
    )(x, mod, mkv, g, gkv, wq, wg, wc, wt)


def _cmp1_body(k0_ref, k1_ref, v0_ref, v1_ref, pk_ref, pv_ref, wk_ref, wv_ref, fk_o, fv_o, *, n_chunk):
    hid = wk_ref.shape[1]
    gpr = LANES // HEAD_DIM
    for srcs, pos, w1, out in (((k0_ref, k1_ref), pk_ref, wk_ref, fk_o), ((v0_ref, v1_ref), pv_ref, wv_ref, fv_o)):
        acc = [[jnp.zeros((n_chunk, hid), F32) for _ in range(CMP_RATIO)] for _ in range(NSA_GROUPS)]
        for l in range(CMP_STRIDE):
            for si, src in enumerate(srcs):
                rows = src[pl.ds(l, n_chunk, stride=CMP_STRIDE), :]
                for gi in range(gpr):
                    g = si * gpr + gi
                    t = rows[:, gi * HEAD_DIM:(gi + 1) * HEAD_DIM]
                    for half in range(CMP_RATIO):
                        p = half * CMP_STRIDE + l
                        lhs = _bf(t + pos[p:p + 1, :])
                        acc[g][half] = acc[g][half] + _dot(lhs, w1[p * HEAD_DIM:(p + 1) * HEAD_DIM, :])
        for g in range(NSA_GROUPS):
            for half in range(CMP_RATIO):
                out[g, :, half * hid:(half + 1) * hid] = acc[g][half]


def _cmp1(kvc, pos_k, pos_v, w1k, w1v, ts):
    B, S, _ = kvc.shape
    G = NSA_GROUPS
    hid = w1k.shape[1]
    nch = ts // CMP_STRIDE
    full = lambda a: pl.BlockSpec(a.shape, lambda b, i: (0,) * a.ndim)
    src = lambda j: pl.BlockSpec((None, ts, LANES), lambda b, i: (b, i, j))
    outs = pl.BlockSpec((None, G, nch, CMP_RATIO * hid), lambda b, i: (b, 0, i, 0))
    return pl.pallas_call(
        functools.partial(_cmp1_body, n_chunk=nch), grid=(B, S // ts),
        in_specs=[src(0), src(1), src(2), src(3), full(pos_k), full(pos_v), full(w1k), full(w1v)],
        out_specs=[outs, outs],
        out_shape=[jax.ShapeDtypeStruct((B, G, S // CMP_STRIDE, CMP_RATIO * hid), F32)] * 2,
        compiler_params=_cp("parallel", "parallel"), name="cmp_stage1",
    )(kvc, kvc, kvc, kvc, pos_k, pos_v, w1k, w1v)


def _gelu_tanh(x):
    return 0.5 * x * (1.0 + jnp.tanh(math.sqrt(2.0 / math.pi) * (x + 0.044715 * (x * x * x))))


def _cmp2_body(fk_ref, fv_ref, w2kt_ref, w2v_ref, kt_o, v_o):
    nc = fk_ref.shape[0]
    hid = w2v_ref.shape[0]
    row = lax.broadcasted_iota(jnp.int32, (nc, hid), 0)

    def hidden(f):
        nxt = jnp.where(row == nc - 1, 0.0, pltpu.roll(f[:, hid:2 * hid], nc - 1, axis=0))
        return _bf(_gelu_tanh(f[:, 0:hid] + nxt))

    kt_o[...] = _bf(_dot_nt(w2kt_ref[...], hidden(fk_ref[...])))
    v_o[...] = _bf(_dot(hidden(fv_ref[...]), w2v_ref[...]))


def _cmp2(fk, fv, w2kt, w2v):
    B, G, NC, H2 = fk.shape
    fin = pl.BlockSpec((None, None, NC, H2), lambda b, g: (b, g, 0, 0))
    return pl.pallas_call(
        _cmp2_body, grid=(B, G),
        in_specs=[fin, fin, pl.BlockSpec(w2kt.shape, lambda b, g: (0, 0)),
                  pl.BlockSpec(w2v.shape, lambda b, g: (0, 0))],
        out_specs=[pl.BlockSpec((None, None, HEAD_DIM, NC), lambda b, g: (b, g, 0, 0)),
                   pl.BlockSpec((None, None, NC, HEAD_DIM), lambda b, g: (b, g, 0, 0))],
        out_shape=[jax.ShapeDtypeStruct((B, G, HEAD_DIM, NC), BF16),
                   jax.ShapeDtypeStruct((B, G, NC, HEAD_DIM), BF16)],
        compiler_params=_cp("parallel", "parallel"), name="cmp_stage2",
    )(fk, fv, w2kt, w2v)


def _head_queries(q_ref):
    return [q_ref[:, h * HEAD_DIM:(h + 1) * HEAD_DIM] for h in range(NSA_HPG)]


def _softmax_attend(q_ref, kt, bias, pv, o_ref, tq):
    rb = SOFTMAX_ROWS
    n_blk = tq // rb
    qs = _head_queries(q_ref)
    scores = {0: _dot(qs[0], kt)}
    tot = [None] * n_blk
    for h in range(NSA_HPG):
        if h + 1 < NSA_HPG:
            scores[h + 1] = _dot(qs[h + 1], kt)
        s_h = scores.pop(h)
        blocks = []
        for b in range(n_blk):
            s = s_h[b * rb:(b + 1) * rb, :] + bias[b * rb:(b + 1) * rb, :]
            m = jnp.maximum(jnp.max(s, axis=-1, keepdims=True), 0.1 * NEG_INF)
            e = jnp.exp2(s - m)
            p = e * (1.0 / jnp.maximum(jnp.sum(e, axis=-1, keepdims=True), 1e-30))
            blocks.append(_bf(p))
            tot[b] = p if tot[b] is None else tot[b] + p
        o_ref[:, h * HEAD_DIM:(h + 1) * HEAD_DIM] = pv(jnp.concatenate(blocks, axis=0)).astype(o_ref.dtype)
    return tot


def _select_blocks(imp, t0):
    nb, tq = imp.shape
    blk = lax.broadcasted_iota(jnp.int32, (nb, tq), 0)
    cur = (t0 + lax.broadcasted_iota(jnp.int32, (nb, tq), 1)) // SLC_LEN
    forced = (blk == 0) | (blk == cur) | (blk == cur - 1)
    imp = jnp.where(forced, FORCED_SCORE, imp)
    imp = jnp.where(blk > cur, -1.0, imp)
    bits = lax.bitcast_convert_type(imp, jnp.int32)
    k_sel = float(min(TOP_N, nb))
    count = lambda mask: jnp.sum(jnp.where(mask, 1.0, 0.0), axis=0, keepdims=True)

    def refine(it, thr):
        cand = thr | jnp.left_shift(jnp.int32(1), 30 - it)
        return jnp.where(count(bits >= cand) >= k_sel, cand, thr)

    thr = lax.fori_loop(0, 31, refine, jnp.zeros((1, tq), jnp.int32))
    above = bits > thr
    tied = bits == thr
    tri = (lax.broadcasted_iota(jnp.int32, (nb, nb), 0) >= lax.broadcasted_iota(jnp.int32, (nb, nb), 1))
    rank = _dot(tri.astype(BF16), jnp.where(tied, 1.0, 0.0).astype(BF16))
    return above | (tied & (rank <= k_sel - count(above)))


def _cmp_attn_body(q_ref, kt_ref, v_ref, wimp_ref, o_ref, sel_ref, imp_ref, *, tq):
    i = pl.program_id(2)
    nc = kt_ref.shape[1]
    nb = wimp_ref.shape[1]
    step = min(CMP_COLS, nc)
    n_need = (i * tq + tq - CMP_LEN) // CMP_STRIDE + 1
    for ncv in range(step, nc + 1, step):
        @pl.when((n_need > ncv - step) & ((n_need <= ncv) | (ncv == nc)))
        def _():
            t = i * tq + lax.broadcasted_iota(jnp.int32, (tq, ncv), 0)
            cmp_end = lax.broadcasted_iota(jnp.int32, (tq, ncv), 1) * CMP_STRIDE + (CMP_LEN - 1)
            bias = jnp.where(cmp_end <= t, 0.0, NEG_INF)
            v = v_ref[0:ncv, :]
            tot = _softmax_attend(q_ref, kt_ref[:, 0:ncv], bias, lambda p: _dot(p, v), o_ref, tq)
            wimp = wimp_ref[0:ncv, :]
            imp_ref[...] = sum(_dot(part, wimp) for part in _split_bf16(jnp.concatenate(tot, axis=0), 3))

    bstep = min(SEL_BLOCKS, nb)
    b_need = (i * tq + tq - 1) // SLC_LEN + 1
    for nbv in range(bstep, nb + 1, bstep):
        @pl.when((b_need > nbv - bstep) & ((b_need <= nbv) | (nbv == nb)))
        def _():
            sel = _select_blocks(imp_ref[:, 0:nbv].T, i * tq)
            sel_ref[:, 0:nbv] = _bf(jnp.where(sel, 0.0, NEG_INF).T)
            if nbv < nb:
                sel_ref[:, nbv:nb] = jnp.full((tq, nb - nbv), NEG_INF, BF16)


def _cmp_attn(q, kct, vc, wimp, tq):
    B, S, _ = q.shape
    G = NSA_GROUPS
    NC = kct.shape[3]
    NB = wimp.shape[1]
    GD = NSA_HPG * HEAD_DIM
    return pl.pallas_call(
        functools.partial(_cmp_attn_body, tq=tq), grid=(B, G, S // tq),
        in_specs=[pl.BlockSpec((None, tq, GD), lambda b, g, i: (b, i, g)),
                  pl.BlockSpec((None, None, HEAD_DIM, NC), lambda b, g, i: (b, g, 0, 0)),
                  pl.BlockSpec((None, None, NC, HEAD_DIM), lambda b, g, i: (b, g, 0, 0)),
                  pl.BlockSpec(wimp.shape, lambda b, g, i: (0, 0))],
        out_specs=[pl.BlockSpec((None, tq, GD), lambda b, g, i: (b, i, g)),
                   pl.BlockSpec((None, None, tq, NB), lambda b, g, i: (b, g, i, 0))],
        out_shape=[jax.ShapeDtypeStruct((B, S, G * GD), BF16),
                   jax.ShapeDtypeStruct((B, G, S, NB), BF16)],
        scratch_shapes=[pltpu.VMEM((tq, NB), F32)],
        compiler_params=_cp("parallel", "parallel", "parallel"), name="cmp_attention_topk",
    )(q, kct, vc, wimp)


def _slc_attn_body(q_ref, selb_ref, kt_ref, vt_ref, e_ref, o_ref, lhs_ref, s0_ref, s1_ref, p0_ref, p1_ref,
                   alpha_ref, m_ref, acc_ref, *, tq, tk, wb):
    i = pl.program_id(2)
    rb = SOFTMAX_ROWS
    rows = NSA_HPG * tq
    tiles_per_window = wb * SLC_LEN // tk
    for h in range(NSA_HPG):
        lhs_ref[pl.ds(h * tq, tq), 0:HEAD_DIM] = q_ref[:, h * HEAD_DIM:(h + 1) * HEAD_DIM]
    lhs_ref[:, HEAD_DIM:LANES] = jnp.zeros((rows, LANES - HEAD_DIM), BF16)
    m_ref[...] = jnp.full(m_ref.shape, NEG_INF, F32)
    acc_ref[...] = jnp.zeros_like(acc_ref)
    s_refs = (s0_ref, s1_ref)
    p_refs = (p0_ref, p1_ref)
    p1_ref[...] = jnp.zeros((rows, tk), BF16)
    ones = jnp.ones((LANES - HEAD_DIM, tk), BF16)
    zeros = jnp.zeros((LANES - HEAD_DIM, tk), BF16)
    tile_keys = lambda j: pl.ds(pl.multiple_of(j * tk, tk), tk)

    def set_window(w):
        win = selb_ref[:, pl.ds(pl.multiple_of(w * wb, wb), wb)]
        for h in range(NSA_HPG):
            lhs_ref[pl.ds(h * tq, tq), LANES:LANES + wb] = win

    def scores(j, s_ref):
        @pl.when(j % tiles_per_window == 0)
        def _():
            set_window(j // tiles_per_window)
        ks = tile_keys(j)
        es = pl.ds(pl.multiple_of((j % tiles_per_window) * tk, tk), tk)
        rhs = jnp.concatenate([kt_ref[:, ks], zeros, e_ref[:, es]], axis=0)
        s_ref[...] = _dot(lhs_ref[...], rhs)

    def value_product(j, p_ref):
        vt = jnp.concatenate([vt_ref[:, tile_keys(j)], ones], axis=0)
        return _dot_nt(p_ref[...], vt)

    def tile(j, par, last_tile):
        if last_tile:
            t = i * tq + lax.broadcasted_iota(jnp.int32, (tq, tk), 0)
            causal = j * tk + lax.broadcasted_iota(jnp.int32, (tq, tk), 1) <= t
        else:
            scores(j + 1, s_refs[1 - par])
        pv_prev = value_product(jnp.maximum(j - 1, 0), p_refs[1 - par])
        for h in range(NSA_HPG):
            for b in range(tq // rb):
                rs = pl.ds(h * tq + b * rb, rb)
                s = s_refs[par][rs, :]
                if last_tile:
                    s = jnp.where(causal[b * rb:(b + 1) * rb, :], s, NEG_INF)
                m_old = m_ref[rs, :]
                m_new = jnp.maximum(m_old, jnp.max(s, axis=-1, keepdims=True))
                ps = [jnp.exp2(_bf(s[:, c * LANES:(c + 1) * LANES] - m_new)) for c in range(tk // LANES)]
                m_ref[rs, :] = m_new
                alpha_ref[rs, :] = jnp.exp2(m_old - m_new)
                p_refs[par][rs, :] = jnp.concatenate(ps, axis=1)
        acc_ref[...] = (acc_ref[...] + pv_prev) * alpha_ref[...]

    last = (i * tq + tq - 1) // tk
    scores(0, s_refs[0])

    def two_tiles(jj, carry):
        tile(2 * jj, 0, False)
        tile(2 * jj + 1, 1, False)
        return carry

    lax.fori_loop(0, last // 2, two_tiles, 0)

    @pl.when(last % 2 == 1)
    def _():
        tile(last - 1, 0, False)

    for par in range(2):
        @pl.when(last % 2 == par)
        def _():
            tile(last, par, True)
            acc = acc_ref[...] + value_product(last, p_refs[par])
            for h in range(NSA_HPG):
                a_h = acc[h * tq:(h + 1) * tq, :]
                o_ref[:, h * HEAD_DIM:(h + 1) * HEAD_DIM] = _bf(
                    a_h[:, :HEAD_DIM] / jnp.maximum(a_h[:, HEAD_DIM:], 1e-30))


def _slc_attn(q, selb, kvt, e, tq, tk, wb):
    B, S, _ = q.shape
    G = NSA_GROUPS
    NB = selb.shape[3]
    GD = NSA_HPG * HEAD_DIM
    rows = NSA_HPG * tq
    return pl.pallas_call(
        functools.partial(_slc_attn_body, tq=tq, tk=tk, wb=wb), grid=(B, G, S // tq),
        in_specs=[pl.BlockSpec((None, tq, GD), lambda b, g, i: (b, i, g)),
                  pl.BlockSpec((None, None, tq, NB), lambda b, g, i: (b, g, i, 0)),
                  pl.BlockSpec((None, HEAD_DIM, S), lambda b, g, i: (b, g, 0)),
                  pl.BlockSpec((None, HEAD_DIM, S), lambda b, g, i: (b, G + g, 0)),
                  pl.BlockSpec(e.shape, lambda b, g, i: (0, 0))],
        out_specs=pl.BlockSpec((None, tq, GD), lambda b, g, i: (b, i, g)),
        out_shape=jax.ShapeDtypeStruct((B, S, G * GD), BF16),
        scratch_shapes=[pltpu.VMEM((rows, LANES + wb), BF16), pltpu.VMEM((rows, tk), F32), pltpu.VMEM((rows, tk), F32),
                        pltpu.VMEM((rows, tk), BF16), pltpu.VMEM((rows, tk), BF16),
                        pltpu.VMEM((rows, LANES), F32), pltpu.VMEM((rows, LANES), F32),
                        pltpu.VMEM((rows, LANES), F32)],
        compiler_params=_cp("parallel", "parallel", "parallel"), name="slc_attention",
    )(q, selb, kvt, kvt, e)


def _win_attn_body(q_ref, kt_ref, vt_ref, o_ref, *, tq, span, n_sub):
    rb = SOFTMAX_ROWS
    ones = jnp.ones((LANES - HEAD_DIM, span), BF16)
    for u in range(n_sub):
        t0 = (pl.program_id(2) * n_sub + u) * tq
        start = pl.multiple_of(jnp.maximum(t0 - WINDOW, 0), LANES)
        ks = pl.ds(start, span)
        t = t0 + lax.broadcasted_iota(jnp.int32, (tq, span), 0)
        dist = t - (start + lax.broadcasted_iota(jnp.int32, (tq, span), 1))
        bias = jnp.where((dist >= 0) & (dist < WINDOW), 0.0, NEG_INF)
        kt = kt_ref[:, ks]
        vt = jnp.concatenate([vt_ref[:, ks], ones], axis=0)
        qs = _head_queries(q_ref.at[pl.ds(u * tq, tq), :])
        scores = {0: _dot(qs[0], kt)}
        for h in range(NSA_HPG):
            if h + 1 < NSA_HPG:
                scores[h + 1] = _dot(qs[h + 1], kt)
            s_h = scores.pop(h)
            blocks = []
            for b in range(tq // rb):
                s = s_h[b * rb:(b + 1) * rb, :] + bias[b * rb:(b + 1) * rb, :]
                blocks.append(jnp.exp2(_bf(s - jnp.max(s, axis=-1, keepdims=True))))
            acc = _dot_nt(jnp.concatenate(blocks, axis=0), vt)
            o_ref[pl.ds(u * tq, tq), h * HEAD_DIM:(h + 1) * HEAD_DIM] = _bf(acc[:, :HEAD_DIM] / acc[:, HEAD_DIM:])


def _win_attn(q, kvt, tq, n_sub):
    B, S, _ = q.shape
    G = NSA_GROUPS
    GD = NSA_HPG * HEAD_DIM
    span = min(tq + WINDOW, S)
    blk = tq * n_sub
    return pl.pallas_call(
        functools.partial(_win_attn_body, tq=tq, span=span, n_sub=n_sub), grid=(B, G, S // blk),
        in_specs=[pl.BlockSpec((None, blk, GD), lambda b, g, i: (b, i, g)),
                  pl.BlockSpec((None, HEAD_DIM, S), lambda b, g, i: (b, 2 * G + g, 0)),
                  pl.BlockSpec((None, HEAD_DIM, S), lambda b, g, i: (b, 3 * G + g, 0))],
        out_specs=pl.BlockSpec((None, blk, GD), lambda b, g, i: (b, i, g)),
        out_shape=jax.ShapeDtypeStruct((B, S, G * GD), BF16),
        compiler_params=_cp("parallel", "parallel", "parallel"), name="win_attention",
    )(q, kvt, kvt)


def _nsa_out_body(oc_ref, os_ref, ow_ref, gt_ref, ge_ref, w_ref, x_ref, mod_ref, g_ref, o_ref):
    D = oc_ref.shape[1]
    ge = ge_ref[...]
    gx = sum(_dot(part, ge) for part in _split_bf16(gt_ref[...], 2))
    o = gx[:, 0:D] * oc_ref[...] + gx[:, D:2 * D] * os_ref[...] + gx[:, 2 * D:3 * D] * ow_ref[...]
    y = _dot(_bf(o), w_ref[...])
    o_ref[...] = x_ref[...] + mod_ref[2:3, :] * _rms(y, g_ref[...])


def _nsa_out(oc, osl, ow, gates, gexp, w, x, mod, g, ts):
    B, S, D = x.shape
    tok = lambda n: pl.BlockSpec((None, ts, n), lambda b, i: (b, i, 0))
    full = lambda a: pl.BlockSpec(a.shape, lambda b, i: (0,) * a.ndim)
    return pl.pallas_call(
        _nsa_out_body, grid=(B, S // ts),
        in_specs=[tok(D), tok(D), tok(D), tok(gates.shape[2]), full(gexp), full(w), tok(D),
                  pl.BlockSpec((None, 6, D), lambda b, i: (b, 0, 0)), full(g)],
        out_specs=tok(D), out_shape=jax.ShapeDtypeStruct((B, S, D), F32),
        compiler_params=_cp("parallel", "parallel"), name="nsa_out",
    )(oc, osl, ow, gates, gexp, w, x, mod, g)


def _importance_matrix(nc, nb):
    n_cmp = nc - CMP_RATIO + 1
    w = np.zeros((nc, nb), np.float32)
    for m in range(SLC_RATIO):
        for n in range(CMP_RATIO):
            off = m - n + CMP_RATIO - 1
            for j in range(nb):
                src = SLC_RATIO * j + off - (CMP_RATIO - 1)
                if 0 <= src < n_cmp:
                    w[src, j] += 1.0
    return jnp.asarray(w, BF16)


def _block_expand(wb):
    return jnp.asarray(np.arange(wb)[:, None] == (np.arange(wb * SLC_LEN)[None, :] // SLC_LEN), BF16)


def _gate_expand(n_in):
    hd = NSA_HEADS * HEAD_DIM
    w = np.zeros((n_in, 3 * hd), np.float32)
    for head in range(NSA_HEADS):
        for br in range(3):
            w[head * 3 + br, br * hd + head * HEAD_DIM: br * hd + (head + 1) * HEAD_DIM] = 1.0
    return jnp.asarray(w, BF16)


def kernel(x, c, ada_w, ada_b, norm_g, mlp_up, mlp_down, rw_mu, rw_r, rw_k, rw_v, rw_o, rw_w0, rw_wa, rw_wb,
           rw_a0, rw_aa, rw_ab, rw_ga, rw_gb, rw_kk, rw_ka, rw_rk, rw_lng, rw_lnb, kv_norm_g, kv_ada_w,
           kv_ada_b, kv_w, cmp_pos_k, cmp_pos_v, cmp_k_w1, cmp_k_w2, cmp_v_w1, cmp_v_w2, nsa_wqg, nsa_wo):
    B, S, D = x.shape
    depth = ada_w.shape[0]
    n_a = rw_mu.shape[0]
    bf = _bf
    row = lambda a: a.reshape(1, -1)
    ts = min(512, S)

    c8 = jnp.pad(c, ((0, 8 - B), (0, 0)))
    mods = _ada(c8, ada_w, ada_b)[:, :B].reshape(depth, B, 6, D)
    mkv = _ada(c8, kv_ada_w[None], kv_ada_b[None])[0, :B].reshape(B, 2, D)

    GD = NSA_GROUPS * HEAD_DIM
    HD = NSA_HEADS * HEAD_DIM
    shared = None
    for layer in range(depth):
        mod = mods[layer]
        ng = norm_g[layer]
        if layer < n_a:
            i = layer
            r, k, v, lw, kk, a, g = _rw_front(
                x, mod, row(ng[0]), rw_mu[i], bf(rw_r[i]), bf(rw_k[i]), bf(rw_v[i]), bf(rw_wa[i]),
                bf(rw_wb[i]), row(rw_w0[i]), bf(rw_aa[i]), bf(rw_ab[i]), row(rw_a0[i]), bf(rw_ga[i]),
                bf(rw_gb[i]), row(rw_kk[i]), row(rw_ka[i]), min(256, S))
            y = _rw_recurrence(r, k, v, lw, kk, a, g, row(rw_rk[i]), row(rw_lng[i]), row(rw_lnb[i]),
                               min(512, S), 2)
            proj = (y, bf(rw_o[i]), row(ng[1]))
        else:
            j = layer - n_a
            wqg = nsa_wqg[j]
            wq = bf(wqg[:, :HD])
            wg = bf(jnp.pad(wqg[:, HD:], ((0, 0), (0, LANES - 3 * NSA_HEADS))))
            if shared is None:
                wc = bf(kv_w[:, :2 * GD])
                wt = bf(kv_w[:, 2 * GD:].T)
            q, gates, kvc, kvt = _nsa_proj(x, mod, mkv, row(ng[0]), row(kv_norm_g), wq, wg, wc, wt, ts)
            if shared is None:
                fk, fv = _cmp1(kvc, cmp_pos_k, cmp_pos_v, bf(cmp_k_w1), bf(cmp_v_w1), min(1024, S))
                kct, vc = _cmp2(fk, fv, bf(cmp_k_w2.T), bf(cmp_v_w2))
                shared = (kct, vc, kvt)
            kct, vc, kvt_s = shared
            nc, nb = S // CMP_STRIDE, S // SLC_LEN
            tq = min(512, S)
            oc, sel = _cmp_attn(q, kct, vc, _importance_matrix(nc, nb), tq)
            wb = min(LANES, nb)
            osl = _slc_attn(q, sel, kvt_s, _block_expand(wb), min(512, S), min(512, S), wb)
            ow = _win_attn(q, kvt_s, min(256, S), 2)
            x = _nsa_out(oc, osl, ow, gates, _gate_expand(gates.shape[2]), bf(nsa_wo[j]), x, mod,
                         row(ng[1]), ts)
            proj = None
        x = _mlp(x, mod, row(ng[2]), row(ng[3]), bf(mlp_up[layer]), bf(mlp_down[layer]), min(1024, S),
                 1024, proj)
    return x
```
